```python
import math
import jax, jax.numpy as jnp
from jax import lax
import numpy as np

D_MODEL = 1024
BATCH = 4
SEQ = 4096
DEPTH = 4
DEC_BATCH = 128
DEC_SEQ = 4
PAST_LEN = 8192
PAGE_SIZE = 128

MIX_WIDTH = D_MODEL
WIDTH_A = MIX_WIDTH // 2
HEAD_DIM_A = 128
N_HEADS_A = WIDTH_A // HEAD_DIM_A
CONV_WIDTH = 4
CONV_CH = 3 * WIDTH_A
CHUNK = 64
WIDTH_B = MIX_WIDTH - WIDTH_A
HEAD_DIM_B = 64
N_Q_HEADS_B = WIDTH_B // HEAD_DIM_B
N_KV_HEADS_B = 2
GQA_GROUP = N_Q_HEADS_B // N_KV_HEADS_B
KV_WIDTH_B = N_KV_HEADS_B * HEAD_DIM_B
WINDOW = 128
D_FF = 4 * D_MODEL
N_MOD = 6
IN_COLS = CONV_CH + WIDTH_A + 2 * N_HEADS_A + WIDTH_B + 2 * KV_WIDTH_B
EPS = 1e-6

kernel_name = 'hymba_gdn_swa_sink_decoder'


def rmsnorm(x, w):
    xf = x.astype(jnp.float32)
    y = xf * lax.rsqrt(jnp.mean(xf * xf, axis=-1, keepdims=True) + EPS)
    return (y * w.astype(jnp.float32)).astype(x.dtype)


def l2norm(x):
    return x * lax.rsqrt(jnp.sum(x * x, axis=-1, keepdims=True) + EPS)


def gated_delta_rule(q, k, v, beta, g, s0, chunk):
    b, t, h, dk = q.shape
    dv = v.shape[-1]
    n = t // chunk

    def blk(a):
        a = a.reshape((b, n, chunk, h) + a.shape[3:])
        return jnp.moveaxis(a, 3, 1)

    q, k, v, beta, g = blk(q), blk(k), blk(v), blk(beta), blk(g)
    gc = jnp.cumsum(g, axis=-1)
    idx = jnp.arange(chunk)
    lower = idx[:, None] >= idx[None, :]
    strict = idx[:, None] > idx[None, :]
    decay = jnp.exp(jnp.where(lower, gc[..., :, None] - gc[..., None, :], -jnp.inf))
    kb = k * beta[..., None]
    lmat = jnp.where(strict, jnp.einsum('bhnid,bhnjd->bhnij', kb, k) * decay, 0.0)
    amat = lmat + jnp.eye(chunk, dtype=lmat.dtype)
    rhs = jnp.concatenate([v * beta[..., None], kb * jnp.exp(gc)[..., None]], axis=-1)
    sol = lax.linalg.triangular_solve(amat, rhs, left_side=True, lower=True, unit_diagonal=True)
    u, w = sol[..., :dv], sol[..., dv:]
    attn = jnp.einsum('bhnid,bhnjd->bhnij', q, k) * decay
    qg = q * jnp.exp(gc)[..., None]
    kg = k * jnp.exp(gc[..., -1:] - gc)[..., None]
    g_last = jnp.exp(gc[..., -1])

    def step(s, xs):
        qg_n, kg_n, u_n, w_n, attn_n, gl_n = xs
        v_new = u_n - jnp.einsum('bhck,bhkv->bhcv', w_n, s)
        o = jnp.einsum('bhck,bhkv->bhcv', qg_n, s) + jnp.einsum('bhij,bhjv->bhiv', attn_n, v_new)
        s = s * gl_n[..., None, None] + jnp.einsum('bhck,bhcv->bhkv', kg_n, v_new)
        return s, o

    xs = tuple(jnp.moveaxis(a, 2, 0) for a in (qg, kg, u, w, attn, g_last))
    s_fin, o = lax.scan(step, s0, xs)
    o = jnp.moveaxis(jnp.moveaxis(o, 0, 2), 1, 3).reshape(b, t, h, dv)
    return o, s_fin


def sink_attention(q, k, v, sinks, mask):
    s = jnp.einsum('...qhgd,...khd->...hgqk', q, k).astype(jnp.float32) * (HEAD_DIM_B ** -0.5)
    s = jnp.where(mask, s, -jnp.inf)
    sk = sinks.astype(jnp.float32)[..., None, None]
    m = jnp.maximum(jnp.max(s, axis=-1, keepdims=True), sk)
    p = jnp.exp(s - m)
    p = (p / (jnp.sum(p, axis=-1, keepdims=True) + jnp.exp(sk - m))).astype(v.dtype)
    return jnp.einsum('...hgqk,...khd->...qhgd', p, v)


def swa_prompt(q, k, v, sinks):
    b, t = q.shape[:2]
    nb = t // WINDOW
    qb = q.reshape(b, nb, WINDOW, N_KV_HEADS_B, GQA_GROUP, HEAD_DIM_B)
    kb = k.reshape(b, nb, WINDOW, N_KV_HEADS_B, HEAD_DIM_B)
    vb = v.reshape(b, nb, WINDOW, N_KV_HEADS_B, HEAD_DIM_B)

    def with_prev(a):
        prev = jnp.concatenate([jnp.zeros_like(a[:, :1]), a[:, :-1]], axis=1)
        return jnp.concatenate([prev, a], axis=2)

    kc, vc = with_prev(kb), with_prev(vb)
    qi = jnp.arange(WINDOW)[:, None]
    kj = jnp.arange(2 * WINDOW)[None, :]
    band = (kj > qi) & (kj <= qi + WINDOW)
    real = (jnp.arange(nb)[:, None, None] > 0) | (kj[None] >= WINDOW)
    mask = (band[None] & real)[:, None, None]
    o = sink_attention(qb, kc, vc, sinks, mask)
    return o.reshape(b, t, WIDTH_B), k[:, -WINDOW:], v[:, -WINDOW:]


def swa_sample(q, k, v, kbuf, vbuf, sinks):
    b, t = q.shape[:2]
    kc = jnp.concatenate([kbuf.astype(k.dtype), k], axis=1)
    vc = jnp.concatenate([vbuf.astype(v.dtype), v], axis=1)
    qi = jnp.arange(t)[:, None]
    kj = jnp.arange(WINDOW + t)[None, :]
    mask = (kj > qi) & (kj <= qi + WINDOW)
    o = sink_attention(q, kc, vc, sinks, mask)
    return o.reshape(b, t, WIDTH_B), kc[:, -WINDOW:], vc[:, -WINDOW:]


def trunk_layer(x, c, conv_buf, ssm0, kbuf, vbuf,
                g_pre_mix, g_post_mix, g_pre_ffn, g_post_ffn, w_ada, b_ada,
                w_in, conv_w, a_log, dt_bias, gdn_norm_w, sinks, w_o, w_up, w_down):
    prompt = conv_buf is None
    b, t, _ = x.shape
    dt = x.dtype
    f32 = jnp.float32
    mod = (jax.nn.silu(c) @ w_ada + b_ada).reshape(b, 1, N_MOD, D_MODEL)
    shift_m, scale_m, gate_m = mod[:, :, 0], mod[:, :, 1], mod[:, :, 2]
    shift_f, scale_f, gate_f = mod[:, :, 3], mod[:, :, 4], mod[:, :, 5]

    h = rmsnorm(x, g_pre_mix) * (1 + scale_m) + shift_m
    proj = h @ w_in
    o1 = CONV_CH
    o2 = o1 + WIDTH_A
    o3 = o2 + N_HEADS_A
    o4 = o3 + N_HEADS_A
    o5 = o4 + WIDTH_B
    o6 = o5 + KV_WIDTH_B
    xc, z, b_lg, a_lg, q_b, k_b, v_b = jnp.split(proj, [o1, o2, o3, o4, o5, o6], axis=-1)

    if prompt:
        xp = jnp.concatenate([jnp.zeros((b, CONV_WIDTH - 1, CONV_CH), dt), xc], axis=1)
    else:
        xp = jnp.concatenate([conv_buf.astype(dt), xc], axis=1)
    new_conv = xp[:, -(CONV_WIDTH - 1):]
    xconv = jax.nn.silu(sum(xp[:, i:i + t] * conv_w[i] for i in range(CONV_WIDTH)))
    qa, ka, va = jnp.split(xconv, 3, axis=-1)
    qa = l2norm(qa.reshape(b, t, N_HEADS_A, HEAD_DIM_A).astype(f32)) * (HEAD_DIM_A ** -0.5)
    ka = l2norm(ka.reshape(b, t, N_HEADS_A, HEAD_DIM_A).astype(f32))
    va = va.reshape(b, t, N_HEADS_A, HEAD_DIM_A).astype(f32)
    beta = jax.nn.sigmoid(b_lg.astype(f32))
    g = -jnp.exp(a_log.astype(f32)) * jax.nn.softplus(a_lg.astype(f32) + dt_bias.astype(f32))
    if prompt:
        s0 = jnp.zeros((b, N_HEADS_A, HEAD_DIM_A, HEAD_DIM_A), f32)
        chunk = min(CHUNK, t)
    else:
        s0 = ssm0.astype(f32)
        chunk = t
    o_a, new_ssm = gated_delta_rule(qa, ka, va, beta, g, s0, chunk)
    o_a = rmsnorm(o_a, gdn_norm_w) * jax.nn.silu(z.reshape(b, t, N_HEADS_A, HEAD_DIM_A).astype(f32))
    o_a = o_a.astype(dt).reshape(b, t, WIDTH_A)

    q_b = q_b.reshape(b, t, N_KV_HEADS_B, GQA_GROUP, HEAD_DIM_B)
    k_b = k_b.reshape(b, t, N_KV_HEADS_B, HEAD_DIM_B)
    v_b = v_b.reshape(b, t, N_KV_HEADS_B, HEAD_DIM_B)
    sk = sinks.reshape(N_KV_HEADS_B, GQA_GROUP)
    if prompt:
        o_b, new_k, new_v = swa_prompt(q_b, k_b, v_b, sk)
    else:
        o_b, new_k, new_v = swa_sample(q_b, k_b, v_b, kbuf, vbuf, sk)

    mix = jnp.concatenate([o_a, o_b], axis=-1) @ w_o
    x = x + gate_m * rmsnorm(mix, g_post_mix)

    h = rmsnorm(x, g_pre_ffn) * (1 + scale_f) + shift_f
    y = jnp.square(jax.nn.relu(h @ w_up)) @ w_down
    x = x + gate_f * rmsnorm(y, g_post_ffn)
    return x, new_conv, new_ssm.astype(dt), new_k, new_v


def setup_inputs(seed: int = 0) -> dict:
    key = jax.random.key(seed)
    ks = jax.random.split(key, 24)
    f32 = jnp.float32

    def nrm(k, shape, scale=1.0):
        return jax.random.normal(k, shape, f32) * scale

    def gain(k, shape):
        return 1.0 + 0.05 * jax.random.normal(k, shape, f32)

    dt0 = jnp.exp(jax.random.uniform(ks[17], (DEPTH, N_HEADS_A), f32, math.log(1e-3), math.log(1e-1)))
    return {
        'x_prompt': nrm(ks[0], (BATCH, SEQ, D_MODEL)),
        'x_sample': nrm(ks[1], (DEC_BATCH, DEC_SEQ, D_MODEL)),
        'state_conv': nrm(ks[2], (DEPTH, DEC_BATCH, CONV_WIDTH - 1, CONV_CH)),
        'state_ssm': nrm(ks[3], (DEPTH, DEC_BATCH, N_HEADS_A, HEAD_DIM_A, HEAD_DIM_A), 0.1),
        'cache_swa_k': nrm(ks[4], (DEPTH, DEC_BATCH, WINDOW, N_KV_HEADS_B, HEAD_DIM_B)),
        'cache_swa_v': nrm(ks[5], (DEPTH, DEC_BATCH, WINDOW, N_KV_HEADS_B, HEAD_DIM_B)),
        'c_prompt': nrm(ks[6], (BATCH, D_MODEL)),
        'c_sample': nrm(ks[7], (DEC_BATCH, D_MODEL)),
        'g_pre_mix': gain(ks[8], (DEPTH, D_MODEL)),
        'g_post_mix': gain(ks[9], (DEPTH, D_MODEL)),
        'g_pre_ffn': gain(ks[10], (DEPTH, D_MODEL)),
        'g_post_ffn': gain(ks[11], (DEPTH, D_MODEL)),
        'w_ada': nrm(ks[12], (DEPTH, D_MODEL, N_MOD * D_MODEL), D_MODEL ** -0.5),
        'b_ada': nrm(ks[13], (DEPTH, N_MOD * D_MODEL), 0.01),
        'w_in': nrm(ks[14], (DEPTH, D_MODEL, IN_COLS), D_MODEL ** -0.5),
        'conv_w': nrm(ks[15], (DEPTH, CONV_WIDTH, CONV_CH), CONV_WIDTH ** -0.5),
        'a_log': jnp.log(jax.random.uniform(ks[16], (DEPTH, N_HEADS_A), f32, 1.0, 16.0)),
        'dt_bias': dt0 + jnp.log(-jnp.expm1(-dt0)),
        'gdn_norm_w': gain(ks[18], (DEPTH, HEAD_DIM_A)),
        'sinks': nrm(ks[19], (DEPTH, N_Q_HEADS_B), 0.5),
        'w_o': nrm(ks[20], (DEPTH, MIX_WIDTH, D_MODEL), MIX_WIDTH ** -0.5),
        'w_up': nrm(ks[21], (DEPTH, D_MODEL, D_FF), D_MODEL ** -0.5),
        'w_down': nrm(ks[22], (DEPTH, D_FF, D_MODEL), D_FF ** -0.5),
    }


def reference(x_prompt, x_sample, state_conv, state_ssm, cache_swa_k, cache_swa_v, c_prompt, c_sample,
              g_pre_mix, g_post_mix, g_pre_ffn, g_post_ffn, w_ada, b_ada, w_in, conv_w, a_log, dt_bias,
              gdn_norm_w, sinks, w_o, w_up, w_down):
    yp, ys = x_prompt, x_sample
    pc, ps, pk, pv = [], [], [], []
    sc, ss, skk, svv = [], [], [], []
    for l in range(DEPTH):
        lp = (g_pre_mix[l], g_post_mix[l], g_pre_ffn[l], g_post_ffn[l], w_ada[l], b_ada[l],
              w_in[l], conv_w[l], a_log[l], dt_bias[l], gdn_norm_w[l], sinks[l], w_o[l], w_up[l], w_down[l])
        yp, c1, s1, k1, v1 = trunk_layer(yp, c_prompt, None, None, None, None, *lp)
        ys, c2, s2, k2, v2 = trunk_layer(ys, c_sample, state_conv[l], state_ssm[l],
                                         cache_swa_k[l], cache_swa_v[l], *lp)
        pc.append(c1); ps.append(s1); pk.append(k1); pv.append(v1)
        sc.append(c2); ss.append(s2); skk.append(k2); svv.append(v2)
    return (yp, ys, jnp.stack(pc), jnp.stack(ps), jnp.stack(pk), jnp.stack(pv),
            jnp.stack(sc), jnp.stack(ss), jnp.stack(skk), jnp.stack(svv))
```

```python
import functools

import jax
import jax.numpy as jnp
from jax import lax
from jax.experimental import pallas as pl
from jax.experimental.pallas import tpu as pltpu

F32 = jnp.float32
BF16 = jnp.bfloat16

D_MODEL = 1024
N_MOD = 6
N_HEADS_A = 4
HEAD_DIM_A = 128
WIDTH_A = N_HEADS_A * HEAD_DIM_A
CONV_WIDTH = 4
CONV_CH = 3 * WIDTH_A
N_Q_HEADS_B = 8
N_KV_HEADS_B = 2
GQA_GROUP = N_Q_HEADS_B // N_KV_HEADS_B
HEAD_DIM_B = 64
WIDTH_B = N_Q_HEADS_B * HEAD_DIM_B
KV_WIDTH_B = N_KV_HEADS_B * HEAD_DIM_B
WINDOW = 128
D_FF = 4 * D_MODEL
EPS = 1e-6
NEG_BIG = -1e30

OFF_XC = 0
OFF_Z = OFF_XC + CONV_CH
OFF_QB = OFF_Z + WIDTH_A
OFF_KB = OFF_QB + WIDTH_B
OFF_VB = OFF_KB + KV_WIDTH_B
OFF_BA = OFF_VB + KV_WIDTH_B
LANE = 128
PROJ_COLS = OFF_BA + LANE

GDN_CHUNK = 64
GDN_BLOCK = 512
TOKEN_TILE = 512
SAMPLE_GROUP = 16
VMEM_LIMIT = 56 * 1024 * 1024


def _silu(x):
    return x * jax.nn.sigmoid(x)


def _softplus(x):
    return jnp.maximum(x, 0.0) + jnp.log(1.0 + jnp.exp(-jnp.abs(x)))


def _rms(x, w):
    return x * lax.rsqrt(jnp.mean(x * x, axis=-1, keepdims=True) + EPS) * w


def _bdot(a, b):
    return jnp.dot(a.astype(BF16), b.astype(BF16), preferred_element_type=F32)


def _bdot_nt(a, b):
    return lax.dot_general(a.astype(BF16), b.astype(BF16), (((1,), (1,)), ((), ())),
                           preferred_element_type=F32)


def _bdot_tn(a, b):
    return lax.dot_general(a.astype(BF16), b.astype(BF16), (((0,), (0,)), ((), ())),
                           preferred_element_type=F32)


def _split3(x):
    hi = x.astype(BF16)
    r1 = x - hi.astype(F32)
    mid = r1.astype(BF16)
    lo = (r1 - mid.astype(F32)).astype(BF16)
    return hi, mid, lo


def _cparams(sem):
    return pltpu.CompilerParams(dimension_semantics=sem, vmem_limit_bytes=VMEM_LIMIT)


def _mod_kernel(c_ref, w_ref, b_ref, o_ref):
    s = _silu(c_ref[...]).astype(BF16)
    o_ref[0] = jnp.dot(s, w_ref[0].astype(BF16), preferred_element_type=F32) + b_ref[0]


def _modulation(c_all, w_ada, b_ada):
    depth = w_ada.shape[0]
    rows = c_all.shape[0]
    ncol = N_MOD * D_MODEL
    tn = 1536
    return pl.pallas_call(
        _mod_kernel,
        grid=(depth, ncol // tn),
        in_specs=[
            pl.BlockSpec((rows, D_MODEL), lambda l, j: (0, 0)),
            pl.BlockSpec((1, D_MODEL, tn), lambda l, j: (l, 0, j)),
            pl.BlockSpec((1, 1, tn), lambda l, j: (l, 0, j)),
        ],
        out_specs=pl.BlockSpec((1, rows, tn), lambda l, j: (l, 0, j)),
        out_shape=jax.ShapeDtypeStruct((depth, rows, ncol), F32),
        compiler_params=_cparams(("arbitrary", "arbitrary")),
        name="adaln_mod",
    )(c_all, w_ada, b_ada.reshape(depth, 1, ncol))


def _premix_kernel(x_ref, mod_ref, g_ref, w_ref, o_ref):
    x = x_ref[...]
    h = _rms(x, g_ref[...]) * (1.0 + mod_ref[0, 1]) + mod_ref[0, 0]
    o_ref[...] = jnp.dot(h.astype(BF16), w_ref[...], preferred_element_type=F32)


def _premix(x, mod, g, w, tm, rep):
    n = x.shape[0]
    r = mod.shape[2]
    return pl.pallas_call(
        _premix_kernel,
        grid=(n // tm,),
        in_specs=[
            pl.BlockSpec((tm, D_MODEL), lambda i: (i, 0)),
            pl.BlockSpec((1, N_MOD, r, D_MODEL), lambda i: (i // rep, 0, 0, 0)),
            pl.BlockSpec((1, D_MODEL), lambda i: (0, 0)),
            pl.BlockSpec((D_MODEL, PROJ_COLS), lambda i: (0, 0)),
        ],
        out_specs=pl.BlockSpec((tm, PROJ_COLS), lambda i: (i, 0)),
        out_shape=jax.ShapeDtypeStruct((n, PROJ_COLS), F32),
        compiler_params=_cparams(("arbitrary",)),
        name="premix_proj",
    )(x, mod, g.reshape(1, D_MODEL), w)


def _ffn_kernel(x_ref, oa_ref, ob_ref, mod_ref, gpm_ref, gpf_ref, gqf_ref, wo_ref, wup_ref, wdn_ref, o_ref):
    x = x_ref[...]
    o_mix = jnp.concatenate([oa_ref[...], ob_ref[...]], axis=1).astype(BF16)
    mix = jnp.dot(o_mix, wo_ref[...], preferred_element_type=F32)
    x1 = x + mod_ref[0, 2] * _rms(mix, gpm_ref[...])
    h = (_rms(x1, gpf_ref[...]) * (1.0 + mod_ref[0, 4]) + mod_ref[0, 3]).astype(BF16)
    ck = 1024
    acc = jnp.zeros(x.shape, F32)
    for c in range(D_FF // ck):
        u = jnp.dot(h, wup_ref[:, c * ck:(c + 1) * ck], preferred_element_type=F32)
        a = jnp.square(jnp.maximum(u, 0.0)).astype(BF16)
        acc = acc + jnp.dot(a, wdn_ref[c * ck:(c + 1) * ck, :], preferred_element_type=F32)
    o_ref[...] = x1 + mod_ref[0, 5] * _rms(acc, gqf_ref[...])


def _ffn(x, oa, ob, mod, g_post_mix, g_pre_ffn, g_post_ffn, wo, wup, wdn, tm, rep):
    n = x.shape[0]
    r = mod.shape[2]
    const = lambda i: (0, 0)
    return pl.pallas_call(
        _ffn_kernel,
        grid=(n // tm,),
        in_specs=[
            pl.BlockSpec((tm, D_MODEL), lambda i: (i, 0)),
            pl.BlockSpec((tm, WIDTH_A), lambda i: (i, 0)),
            pl.BlockSpec((tm, WIDTH_B), lambda i: (i, 0)),
            pl.BlockSpec((1, N_MOD, r, D_MODEL), lambda i: (i // rep, 0, 0, 0)),
            pl.BlockSpec((1, D_MODEL), const),
            pl.BlockSpec((1, D_MODEL), const),
            pl.BlockSpec((1, D_MODEL), const),
            pl.BlockSpec((D_MODEL, D_MODEL), const, pipeline_mode=pl.Buffered(1)),
            pl.BlockSpec((D_MODEL, D_FF), const, pipeline_mode=pl.Buffered(1)),
            pl.BlockSpec((D_FF, D_MODEL), const, pipeline_mode=pl.Buffered(1)),
        ],
        out_specs=pl.BlockSpec((tm, D_MODEL), lambda i: (i, 0)),
        out_shape=jax.ShapeDtypeStruct((n, D_MODEL), F32),
        compiler_params=_cparams(("arbitrary",)),
        name="outproj_ffn",
    )(x, oa, ob, mod, g_post_mix.reshape(1, D_MODEL), g_pre_ffn.reshape(1, D_MODEL),
      g_post_ffn.reshape(1, D_MODEL), wo, wup, wdn)


def _lane_bcast(x, col, width=HEAD_DIM_A):
    return jnp.broadcast_to(x[:, col:col + 1], (x.shape[0], width))


def _gdn_prompt_kernel(xc_ref, z_ref, ba_ref, cw_ref, hp_ref, gnw_ref, o_ref, ssm_ref,
                       s_scr, xbuf, xs_scr, *, chunk, block):
    C = chunk
    H = N_HEADS_A
    HD = HEAD_DIM_A
    PW = H * C
    j = pl.program_id(1)
    nj = pl.num_programs(1)

    @pl.when(j == 0)
    def _():
        s_scr[...] = jnp.zeros(s_scr.shape, F32)
        xbuf[0:8, :] = jnp.zeros((8, CONV_CH), F32)

    @pl.when(j > 0)
    def _():
        xbuf[0:8, :] = xbuf[block:block + 8, :]

    xbuf[8:block + 8, :] = xc_ref[...]
    base = 8 - (CONV_WIDTH - 1)
    conv = xbuf[base:base + block, :] * cw_ref[0:1, :]
    for i in range(1, CONV_WIDTH):
        conv = conv + xbuf[base + i:base + i + block, :] * cw_ref[i:i + 1, :]
    xs_scr[...] = _silu(conv)

    ri = lax.broadcasted_iota(jnp.int32, (C, PW), 0)
    li = lax.broadcasted_iota(jnp.int32, (C, PW), 1)
    lc = li % C
    lower = ri >= lc
    strict = ri > lc
    eye = ri == lc
    rb = lax.broadcasted_iota(jnp.int32, (PW, PW), 0) // C
    cb = lax.broadcasted_iota(jnp.int32, (PW, PW), 1) // C
    bd_mask = rb == cb
    kbd_mask = (lax.broadcasted_iota(jnp.int32, (PW, WIDTH_A), 0) // C
                == lax.broadcasted_iota(jnp.int32, (PW, WIDTH_A), 1) // HD)
    rhs_mask = (lax.broadcasted_iota(jnp.int32, (PW, 2 * WIDTH_A), 0) // C
                == (lax.broadcasted_iota(jnp.int32, (PW, 2 * WIDTH_A), 1) % WIDTH_A) // HD)
    tri = (lax.broadcasted_iota(jnp.int32, (C, C), 0)
           >= lax.broadcasted_iota(jnp.int32, (C, C), 1)).astype(BF16)
    lane128 = lax.broadcasted_iota(jnp.int32, (C, LANE), 1)

    neg_a = -jnp.exp(hp_ref[0:1, :])
    dt_b = hp_ref[1:2, :]
    gnw = gnw_ref[...]

    def bd(y):
        return jnp.where(bd_mask, jnp.concatenate([y] * H, axis=0), 0.0)

    def pmm(x, y):
        return _bdot(x, bd(y))

    def pack_cols(cols):
        per_tile = LANE // C
        tiles = []
        for t in range(H // per_tile):
            acc = cols[t * per_tile]
            for u in range(1, per_tile):
                acc = jnp.where(lane128 < u * C, acc, cols[t * per_tile + u])
            tiles.append(acc)
        return jnp.concatenate(tiles, axis=1)

    def step(i, carry):
        r0 = pl.multiple_of(i * C, C)
        xs = xs_scr[pl.ds(r0, C), :]
        ba = ba_ref[pl.ds(r0, C), :]
        beta_all = jax.nn.sigmoid(ba)
        g_all = neg_a * _softplus(ba + dt_b)
        g_hi, g_mid, g_lo = _split3(g_all)
        g3 = jnp.dot(tri, jnp.concatenate([g_hi, g_mid, g_lo], axis=1), preferred_element_type=F32)
        gc_all = g3[:, 0:LANE] + g3[:, LANE:2 * LANE] + g3[:, 2 * LANE:3 * LANE]

        q, k, v, kb, beta_c, gc_c, eg_c = [], [], [], [], [], [], []
        for h in range(H):
            qh = xs[:, h * HD:(h + 1) * HD]
            kh = xs[:, WIDTH_A + h * HD:WIDTH_A + (h + 1) * HD]
            vh = xs[:, 2 * WIDTH_A + h * HD:2 * WIDTH_A + (h + 1) * HD]
            qh = qh * lax.rsqrt(jnp.sum(qh * qh, axis=-1, keepdims=True) + EPS) * (HD ** -0.5)
            kh = kh * lax.rsqrt(jnp.sum(kh * kh, axis=-1, keepdims=True) + EPS)
            bh = _lane_bcast(beta_all, h)
            gh = _lane_bcast(gc_all, N_HEADS_A + h)
            q.append(qh)
            k.append(kh)
            v.append(vh)
            kb.append(kh * bh)
            beta_c.append(bh)
            gc_c.append(gh)
            eg_c.append(jnp.exp(gh))

        gcp = pack_cols(gc_c)
        gc_row = jnp.sum(jnp.where(eye, gcp, 0.0), axis=0, keepdims=True)
        decay = jnp.exp(jnp.where(lower, gcp - gc_row, NEG_BIG))

        k_p = jnp.concatenate(k, axis=1)
        kbd = jnp.where(kbd_mask, jnp.concatenate([k_p] * H, axis=0), 0.0)
        lhs = jnp.concatenate([jnp.concatenate(kb, axis=1), jnp.concatenate(q, axis=1)], axis=0)
        kq = _bdot_nt(lhs, kbd)
        lmat = jnp.where(strict, kq[0:C] * decay, 0.0)
        attn = kq[C:2 * C] * decay

        base_blk = min(16, C)
        same = (ri // base_blk) == (lc // base_blk)
        ld = jnp.where(same, lmat, 0.0)
        p = jnp.where(eye, 1.0, 0.0) - ld
        pw = ld
        span = 2
        while span < base_blk:
            pw = pmm(pw, pw)
            p = p + pmm(p, pw)
            span *= 2
        blk = base_blk
        while blk < C:
            nxt = 2 * blk
            same_n = (ri // nxt) == (lc // nxt)
            m = jnp.where(jnp.logical_and(same_n, jnp.logical_not(same)), lmat, 0.0)
            p = p - pmm(p, pmm(m, p))
            same = same_n
            blk = nxt

        rhs = jnp.concatenate([v[h] * beta_c[h] for h in range(H)]
                              + [kb[h] * eg_c[h] for h in range(H)], axis=1)
        rhs_bd = jnp.where(rhs_mask, jnp.concatenate([rhs] * H, axis=0), 0.0)
        sol = _bdot(p, rhs_bd)

        vn, qs = [], []
        for h in range(H):
            sh = s_scr[h]
            u_h = sol[:, h * HD:(h + 1) * HD]
            w_h = sol[:, WIDTH_A + h * HD:WIDTH_A + (h + 1) * HD]
            r = _bdot(jnp.concatenate([w_h, q[h] * eg_c[h]], axis=0), sh)
            vn.append(u_h - r[0:C])
            qs.append(r[C:2 * C])
        vn_p = jnp.concatenate(vn, axis=1)
        vn_bd = jnp.where(kbd_mask, jnp.concatenate([vn_p] * H, axis=0), 0.0)
        o_p = jnp.concatenate(qs, axis=1) + _bdot(attn, vn_bd)

        for h in range(H):
            g_last = gc_all[C - 1:C, N_HEADS_A + h:N_HEADS_A + h + 1]
            kg = k[h] * jnp.exp(g_last - gc_c[h])
            s_scr[h] = s_scr[h] * jnp.exp(g_last) + _bdot_tn(kg, vn[h])
            o_h = o_p[:, h * HD:(h + 1) * HD]
            z_h = z_ref[pl.ds(r0, C), h * HD:(h + 1) * HD]
            o_ref[pl.ds(r0, C), h * HD:(h + 1) * HD] = _rms(o_h, gnw) * _silu(z_h)
        return carry

    lax.fori_loop(0, block // C, step, 0)

    @pl.when(j == nj - 1)
    def _():
        ssm_ref[0] = s_scr[...]


def _gdn_prompt(proj, conv_w, hp, gnw, batch, seq):
    blk = min(GDN_BLOCK, seq)
    nblk = seq // blk
    kern = functools.partial(_gdn_prompt_kernel, chunk=min(GDN_CHUNK, seq), block=blk)
    return pl.pallas_call(
        kern,
        grid=(batch, nblk),
        in_specs=[
            pl.BlockSpec((blk, CONV_CH), lambda b, j: (b * nblk + j, OFF_XC // CONV_CH)),
            pl.BlockSpec((blk, WIDTH_A), lambda b, j: (b * nblk + j, OFF_Z // WIDTH_A)),
            pl.BlockSpec((blk, LANE), lambda b, j: (b * nblk + j, OFF_BA // LANE)),
            pl.BlockSpec((CONV_WIDTH, CONV_CH), lambda b, j: (0, 0)),
            pl.BlockSpec((8, LANE), lambda b, j: (0, 0)),
            pl.BlockSpec((1, HEAD_DIM_A), lambda b, j: (0, 0)),
        ],
        out_specs=[
            pl.BlockSpec((blk, WIDTH_A), lambda b, j: (b * nblk + j, 0)),
            pl.BlockSpec((1, N_HEADS_A, HEAD_DIM_A, HEAD_DIM_A), lambda b, j: (b, 0, 0, 0)),
        ],
        out_shape=[
            jax.ShapeDtypeStruct((batch * seq, WIDTH_A), F32),
            jax.ShapeDtypeStruct((batch, N_HEADS_A, HEAD_DIM_A, HEAD_DIM_A), F32),
        ],
        scratch_shapes=[
            pltpu.VMEM((N_HEADS_A, HEAD_DIM_A, HEAD_DIM_A), F32),
            pltpu.VMEM((blk + 8, CONV_CH), F32),
            pltpu.VMEM((blk, CONV_CH), F32),
        ],
        compiler_params=_cparams(("arbitrary", "arbitrary")),
        name="gdn_prompt",
    )(proj, proj, proj, conv_w, hp, gnw.reshape(1, HEAD_DIM_A))


def _sink_column(sink_ref, h, shape, axis, period):
    g = (lax.broadcasted_iota(jnp.int32, shape, axis) // period) % GQA_GROUP
    col = jnp.full(shape, sink_ref[h * GQA_GROUP], F32)
    for gg in range(1, GQA_GROUP):
        col = jnp.where(g == gg, sink_ref[h * GQA_GROUP + gg], col)
    return col


def _swa_prompt_kernel(sink_ref, q_ref, kc_ref, kp_ref, vc_ref, vp_ref, o_ref):
    W = WINDOW
    j = pl.program_id(1)
    q = q_ref[...]
    kcat = jnp.concatenate([kp_ref[...], kc_ref[...]], axis=0).astype(BF16)
    vcat = jnp.concatenate([vp_ref[...], vc_ref[...]], axis=0).astype(BF16)
    rows = GQA_GROUP * W
    qi = lax.broadcasted_iota(jnp.int32, (rows, 2 * W), 0) % W
    kj = lax.broadcasted_iota(jnp.int32, (rows, 2 * W), 1)
    first_key = jnp.where(j > 0, 0, W)
    mask = (kj > qi) & (kj <= qi + W) & (kj >= first_key)
    for h in range(N_KV_HEADS_B):
        k_h = kcat[:, h * HEAD_DIM_B:(h + 1) * HEAD_DIM_B]
        v_h = vcat[:, h * HEAD_DIM_B:(h + 1) * HEAD_DIM_B]
        qs = jnp.concatenate(
            [q[:, (h * GQA_GROUP + g) * HEAD_DIM_B:(h * GQA_GROUP + g + 1) * HEAD_DIM_B]
             for g in range(GQA_GROUP)], axis=0)
        s = _bdot_nt(qs, k_h) * (HEAD_DIM_B ** -0.5)
        s = jnp.where(mask, s, NEG_BIG)
        sk = _sink_column(sink_ref, h, (rows, 1), 0, W)
        m = jnp.maximum(jnp.max(s, axis=-1, keepdims=True), sk)
        p = jnp.exp(s - m)
        p = p / (jnp.sum(p, axis=-1, keepdims=True) + jnp.exp(sk - m))
        o = jnp.dot(p.astype(BF16), v_h, preferred_element_type=F32)
        for g in range(GQA_GROUP):
            c0 = (h * GQA_GROUP + g) * HEAD_DIM_B
            o_ref[:, c0:c0 + HEAD_DIM_B] = o[g * W:(g + 1) * W, :]


def _swa_prompt(proj, sinks, batch, seq):
    nb = seq // WINDOW
    kcol = OFF_KB // KV_WIDTH_B
    vcol = OFF_VB // KV_WIDTH_B
    return pl.pallas_call(
        _swa_prompt_kernel,
        grid=(batch, nb),
        in_specs=[
            pl.BlockSpec(memory_space=pltpu.SMEM),
            pl.BlockSpec((WINDOW, WIDTH_B), lambda b, j: (b * nb + j, OFF_QB // WIDTH_B)),
            pl.BlockSpec((WINDOW, KV_WIDTH_B), lambda b, j: (b * nb + j, kcol)),
            pl.BlockSpec((WINDOW, KV_WIDTH_B), lambda b, j: (b * nb + jnp.maximum(j - 1, 0), kcol)),
            pl.BlockSpec((WINDOW, KV_WIDTH_B), lambda b, j: (b * nb + j, vcol)),
            pl.BlockSpec((WINDOW, KV_WIDTH_B), lambda b, j: (b * nb + jnp.maximum(j - 1, 0), vcol)),
        ],
        out_specs=pl.BlockSpec((WINDOW, WIDTH_B), lambda b, j: (b * nb + j, 0)),
        out_shape=jax.ShapeDtypeStruct((batch * seq, WIDTH_B), F32),
        compiler_params=_cparams(("arbitrary", "arbitrary")),
        name="swa_prompt",
    )(sinks, proj, proj, proj, proj, proj)


def _gdn_sample_kernel(xc_ref, z_ref, ba_ref, cs_ref, s0_ref, cw_ref, hp_ref, gnw_ref,
                       o_ref, s1_ref, lhs_scr, u_scr, kg_scr, res_scr, gl_scr, *, steps, group):
    T = steps
    GB = group
    H = N_HEADS_A
    HD = HEAD_DIM_A
    R = 2 * T

    u_scr[...] = jnp.zeros(u_scr.shape, F32)
    kg_scr[...] = jnp.zeros(kg_scr.shape, F32)

    xp = [cs_ref[i] for i in range(CONV_WIDTH - 1)] + [xc_ref[t] for t in range(T)]
    xs = []
    for t in range(T):
        acc = xp[t] * cw_ref[0:1, :]
        for i in range(1, CONV_WIDTH):
            acc = acc + xp[t + i] * cw_ref[i:i + 1, :]
        xs.append(_silu(acc))

    neg_a = -jnp.exp(hp_ref[0:1, :])
    dt_b = hp_ref[1:2, :]
    beta_all = [jax.nn.sigmoid(ba_ref[t]) for t in range(T)]
    g_all = [neg_a * _softplus(ba_ref[t] + dt_b) for t in range(T)]
    gc_all = [g_all[0]]
    for t in range(1, T):
        gc_all.append(gc_all[t - 1] + g_all[t])

    u_keep, attn_keep = [], []
    for h in range(H):
        q, k, v, kb, beta, gc = [], [], [], [], [], []
        for t in range(T):
            qh = xs[t][:, h * HD:(h + 1) * HD]
            kh = xs[t][:, WIDTH_A + h * HD:WIDTH_A + (h + 1) * HD]
            vh = xs[t][:, 2 * WIDTH_A + h * HD:2 * WIDTH_A + (h + 1) * HD]
            qh = qh * lax.rsqrt(jnp.sum(qh * qh, axis=-1, keepdims=True) + EPS) * (HD ** -0.5)
            kh = kh * lax.rsqrt(jnp.sum(kh * kh, axis=-1, keepdims=True) + EPS)
            bt = beta_all[t][:, h:h + 1]
            q.append(qh)
            k.append(kh)
            v.append(vh)
            kb.append(kh * bt)
            beta.append(bt)
            gc.append(gc_all[t][:, H + h:H + h + 1])
        u, w, attn = [], [], []
        for t in range(T):
            ut = v[t] * beta[t]
            wt = kb[t] * jnp.exp(gc[t])
            arow = []
            for s in range(t + 1):
                dec = jnp.exp(gc[t] - gc[s])
                arow.append(jnp.sum(q[t] * k[s], axis=-1, keepdims=True) * dec)
                if s < t:
                    l_ts = jnp.sum(kb[t] * k[s], axis=-1, keepdims=True) * dec
                    ut = ut - l_ts * u[s]
                    wt = wt - l_ts * w[s]
            u.append(ut)
            w.append(wt)
            attn.append(arow)
        g_last = gc[T - 1]
        for t in range(T):
            row0 = h * GB * R
            lhs_scr[pl.ds(row0 + t, GB, stride=R), :] = w[t]
            lhs_scr[pl.ds(row0 + T + t, GB, stride=R), :] = q[t] * jnp.exp(gc[t])
            u_scr[pl.ds(row0 + t, GB, stride=R), :] = u[t]
            kg_scr[pl.ds(row0 + t, GB, stride=R), :] = k[t] * jnp.exp(g_last - gc[t])
        gl_scr[h] = jnp.broadcast_to(jnp.exp(g_last), (GB, HD))
        u_keep.append(u)
        attn_keep.append(attn)

    def body(b, carry):
        for h in range(H):
            row = pl.multiple_of((h * GB + b) * R, R)
            s = s0_ref[b, h]
            res = _bdot(lhs_scr[pl.ds(row, R), :], s)
            res_scr[pl.ds(row, R), :] = res
            vn = u_scr[pl.ds(row, R), :] - res
            upd = _bdot_tn(kg_scr[pl.ds(row, R), :], vn)
            s1_ref[b, h] = s * gl_scr[h, pl.ds(b, 1), :] + upd
        return carry

    lax.fori_loop(0, GB, body, 0)

    gnw = gnw_ref[...]
    for h in range(H):
        row0 = h * GB * R
        vn = []
        for t in range(T):
            ws_t = res_scr[pl.ds(row0 + t, GB, stride=R), :]
            vn.append(u_keep[h][t] - ws_t)
        for t in range(T):
            o_t = res_scr[pl.ds(row0 + T + t, GB, stride=R), :]
            for s in range(t + 1):
                o_t = o_t + attn_keep[h][t][s] * vn[s]
            z_t = z_ref[t][:, h * HD:(h + 1) * HD]
            o_ref[t, :, h * HD:(h + 1) * HD] = _rms(o_t, gnw) * _silu(z_t)


def _gdn_sample(proj3, cs_tm, s0, conv_w, hp, gnw):
    steps, nseq, _ = proj3.shape
    gb = min(SAMPLE_GROUP, nseq)
    rows = N_HEADS_A * gb * 2 * steps
    kern = functools.partial(_gdn_sample_kernel, steps=steps, group=gb)
    return pl.pallas_call(
        kern,
        grid=(nseq // gb,),
        in_specs=[
            pl.BlockSpec((steps, gb, CONV_CH), lambda i: (0, i, OFF_XC // CONV_CH)),
            pl.BlockSpec((steps, gb, WIDTH_A), lambda i: (0, i, OFF_Z // WIDTH_A)),
            pl.BlockSpec((steps, gb, LANE), lambda i: (0, i, OFF_BA // LANE)),
            pl.BlockSpec((CONV_WIDTH - 1, gb, CONV_CH), lambda i: (0, i, 0)),
            pl.BlockSpec((gb, N_HEADS_A, HEAD_DIM_A, HEAD_DIM_A), lambda i: (i, 0, 0, 0)),
            pl.BlockSpec((CONV_WIDTH, CONV_CH), lambda i: (0, 0)),
            pl.BlockSpec((8, LANE), lambda i: (0, 0)),
            pl.BlockSpec((1, HEAD_DIM_A), lambda i: (0, 0)),
        ],
        out_specs=[
            pl.BlockSpec((steps, gb, WIDTH_A), lambda i: (0, i, 0)),
            pl.BlockSpec((gb, N_HEADS_A, HEAD_DIM_A, HEAD_DIM_A), lambda i: (i, 0, 0, 0)),
        ],
        out_shape=[
            jax.ShapeDtypeStruct((steps, nseq, WIDTH_A), F32),
            jax.ShapeDtypeStruct(s0.shape, F32),
        ],
        scratch_shapes=[
            pltpu.VMEM((rows, HEAD_DIM_A), F32),
            pltpu.VMEM((rows, HEAD_DIM_A), F32),
            pltpu.VMEM((rows, HEAD_DIM_A), F32),
            pltpu.VMEM((rows, HEAD_DIM_A), F32),
            pltpu.VMEM((N_HEADS_A, gb, HEAD_DIM_A), F32),
        ],
        compiler_params=_cparams(("arbitrary",)),
        name="gdn_sample",
    )(proj3, proj3, proj3, cs_tm, s0, conv_w, hp, gnw.reshape(1, HEAD_DIM_A))


def _swa_sample_kernel(sink_ref, q_ref, kc_ref, vc_ref, kn_ref, vn_ref, o_ref, *, steps):
    T = steps
    W = WINDOW
    rows = T * GQA_GROUP
    gb = q_ref.shape[0]
    t_of_row = lax.broadcasted_iota(jnp.int32, (1, rows, 1), 1) // GQA_GROUP
    kj = lax.broadcasted_iota(jnp.int32, (1, 1, W), 2)
    cmask = kj > t_of_row
    for h in range(N_KV_HEADS_B):
        lo, hi = h * HEAD_DIM_B, (h + 1) * HEAD_DIM_B
        qh = q_ref[:, h]
        qh_r = qh.astype(BF16)
        kch = kc_ref[:, :, lo:hi].astype(BF16)
        vch = vc_ref[:, :, lo:hi].astype(BF16)
        sc = jnp.einsum("bqd,bkd->bqk", qh_r, kch, preferred_element_type=F32) * (HEAD_DIM_B ** -0.5)
        sc = jnp.where(cmask, sc, NEG_BIG)
        sn = []
        for s in range(T):
            kn_s = kn_ref[:, s:s + 1, lo:hi]
            v = jnp.sum(qh * kn_s, axis=-1, keepdims=True) * (HEAD_DIM_B ** -0.5)
            sn.append(jnp.where(t_of_row >= s, v, NEG_BIG))
        sk = _sink_column(sink_ref, h, (1, rows, 1), 1, 1)
        m = jnp.maximum(jnp.max(sc, axis=-1, keepdims=True), sk)
        for s in range(T):
            m = jnp.maximum(m, sn[s])
        pc = jnp.exp(sc - m)
        pn = [jnp.exp(sn[s] - m) for s in range(T)]
        den = jnp.sum(pc, axis=-1, keepdims=True) + jnp.exp(sk - m)
        for s in range(T):
            den = den + pn[s]
        inv = 1.0 / den
        o = jnp.einsum("bqk,bkd->bqd", (pc * inv).astype(BF16), vch, preferred_element_type=F32)
        for s in range(T):
            o = o + (pn[s] * inv) * vn_ref[:, s:s + 1, lo:hi]
        o_ref[:, h] = o


def _swa_sample(q_r, kc, vc, kn, vn, sinks):
    nseq, _, rows, _ = q_r.shape
    steps = kn.shape[1]
    gb = min(SAMPLE_GROUP, nseq)
    kern = functools.partial(_swa_sample_kernel, steps=steps)
    return pl.pallas_call(
        kern,
        grid=(nseq // gb,),
        in_specs=[
            pl.BlockSpec(memory_space=pltpu.SMEM),
            pl.BlockSpec((gb, N_KV_HEADS_B, rows, HEAD_DIM_B), lambda i: (i, 0, 0, 0)),
            pl.BlockSpec((gb, WINDOW, KV_WIDTH_B), lambda i: (i, 0, 0)),
            pl.BlockSpec((gb, WINDOW, KV_WIDTH_B), lambda i: (i, 0, 0)),
            pl.BlockSpec((gb, steps, KV_WIDTH_B), lambda i: (i, 0, 0)),
            pl.BlockSpec((gb, steps, KV_WIDTH_B), lambda i: (i, 0, 0)),
        ],
        out_specs=pl.BlockSpec((gb, N_KV_HEADS_B, rows, HEAD_DIM_B), lambda i: (i, 0, 0, 0)),
        out_shape=jax.ShapeDtypeStruct(q_r.shape, F32),
        compiler_params=_cparams(("arbitrary",)),
        name="swa_sample",
    )(sinks, q_r, kc, vc, kn, vn)


def _prompt_layer(x, mod, lw, batch, seq):
    tm = min(TOKEN_TILE, seq)
    rep = seq // tm
    proj = _premix(x, mod, lw["g_pre_mix"], lw["w_in"], tm, rep)
    o_a, ssm = _gdn_prompt(proj, lw["conv_w"], lw["hp"], lw["gdn_norm_w"], batch, seq)
    o_b = _swa_prompt(proj, lw["sinks"], batch, seq)
    x = _ffn(x, o_a, o_b, mod, lw["g_post_mix"], lw["g_pre_ffn"], lw["g_post_ffn"],
             lw["w_o"], lw["w_up"], lw["w_down"], tm, rep)
    p3 = proj.reshape(batch, seq, PROJ_COLS)
    conv = p3[:, seq - (CONV_WIDTH - 1):, OFF_XC:OFF_XC + CONV_CH]
    k_new = p3[:, seq - WINDOW:, OFF_KB:OFF_KB + KV_WIDTH_B].reshape(batch, WINDOW, N_KV_HEADS_B, HEAD_DIM_B)
    v_new = p3[:, seq - WINDOW:, OFF_VB:OFF_VB + KV_WIDTH_B].reshape(batch, WINDOW, N_KV_HEADS_B, HEAD_DIM_B)
    return x, conv, ssm, k_new, v_new


def _sample_layer(x, mod, lw, conv_buf, ssm0, kbuf, vbuf, nseq, steps):
    proj = _premix(x, mod, lw["g_pre_mix"], lw["w_in"], nseq, steps)
    p3 = proj.reshape(steps, nseq, PROJ_COLS)
    cs_tm = jnp.transpose(conv_buf, (1, 0, 2))
    o_a, ssm = _gdn_sample(p3, cs_tm, ssm0, lw["conv_w"], lw["hp"], lw["gdn_norm_w"])
    q_r = p3[:, :, OFF_QB:OFF_QB + WIDTH_B].reshape(steps, nseq, N_KV_HEADS_B, GQA_GROUP, HEAD_DIM_B)
    q_r = jnp.transpose(q_r, (1, 2, 0, 3, 4)).reshape(nseq, N_KV_HEADS_B, steps * GQA_GROUP, HEAD_DIM_B)
    k_new = jnp.transpose(p3[:, :, OFF_KB:OFF_KB + KV_WIDTH_B], (1, 0, 2))
    v_new = jnp.transpose(p3[:, :, OFF_VB:OFF_VB + KV_WIDTH_B], (1, 0, 2))
    o_b = _swa_sample(q_r, kbuf.reshape(nseq, WINDOW, KV_WIDTH_B), vbuf.reshape(nseq, WINDOW, KV_WIDTH_B),
                      k_new, v_new, lw["sinks"])
    o_b = o_b.reshape(nseq, N_KV_HEADS_B, steps, GQA_GROUP, HEAD_DIM_B)
    o_b = jnp.transpose(o_b, (2, 0, 1, 3, 4)).reshape(steps * nseq, WIDTH_B)
    x = _ffn(x, o_a.reshape(steps * nseq, WIDTH_A), o_b, mod, lw["g_post_mix"], lw["g_pre_ffn"],
             lw["g_post_ffn"], lw["w_o"], lw["w_up"], lw["w_down"], nseq, steps)
    xc = p3[:, :, OFF_XC:OFF_XC + CONV_CH]
    xp = jnp.concatenate([cs_tm, xc], axis=0)
    conv = jnp.transpose(xp[xp.shape[0] - (CONV_WIDTH - 1):], (1, 0, 2))
    k_out = jnp.concatenate([kbuf, k_new.reshape(nseq, steps, N_KV_HEADS_B, HEAD_DIM_B)], axis=1)[:, -WINDOW:]
    v_out = jnp.concatenate([vbuf, v_new.reshape(nseq, steps, N_KV_HEADS_B, HEAD_DIM_B)], axis=1)[:, -WINDOW:]
    return x, conv, ssm, k_out, v_out


def _pack_w_in(w_in):
    o1 = CONV_CH + WIDTH_A
    o2 = o1 + 2 * N_HEADS_A
    pad = jnp.zeros(w_in.shape[:-1] + (PROJ_COLS - OFF_BA - 2 * N_HEADS_A,), w_in.dtype)
    return jnp.concatenate([w_in[..., :o1], w_in[..., o2:], w_in[..., o1:o2], pad], axis=-1).astype(BF16)


def _head_params(a_log, dt_bias):
    depth = a_log.shape[0]
    hp = jnp.zeros((depth, 8, LANE), F32)
    hp = hp.at[:, 0, N_HEADS_A:2 * N_HEADS_A].set(a_log.astype(F32))
    hp = hp.at[:, 1, N_HEADS_A:2 * N_HEADS_A].set(dt_bias.astype(F32))
    return hp


def kernel(x_prompt, x_sample, state_conv, state_ssm, cache_swa_k, cache_swa_v, c_prompt, c_sample,
           g_pre_mix, g_post_mix, g_pre_ffn, g_post_ffn, w_ada, b_ada, w_in, conv_w, a_log, dt_bias,
           gdn_norm_w, sinks, w_o, w_up, w_down):
    depth = w_ada.shape[0]
    batch, seq, _ = x_prompt.shape
    nseq, steps, _ = x_sample.shape

    c_all = jnp.concatenate([c_prompt, c_sample], axis=0)
    rows = c_all.shape[0]
    rows_p = -(-rows // 8) * 8
    c_all = jnp.pad(c_all, ((0, rows_p - rows), (0, 0)))
    mod_all = _modulation(c_all, w_ada, b_ada).reshape(depth, rows_p, N_MOD, D_MODEL)
    mod_p = mod_all[:, :batch].reshape(depth, batch, N_MOD, 1, D_MODEL)
    mod_s = jnp.transpose(mod_all[:, batch:batch + nseq], (0, 2, 1, 3)).reshape(depth, 1, N_MOD, nseq, D_MODEL)

    w_in_p = _pack_w_in(w_in)
    hp = _head_params(a_log, dt_bias)
    w_o_b = w_o.astype(BF16)
    w_up_b = w_up.astype(BF16)
    w_dn_b = w_down.astype(BF16)

    yp = x_prompt.reshape(batch * seq, D_MODEL)
    ys = jnp.transpose(x_sample, (1, 0, 2)).reshape(steps * nseq, D_MODEL)
    outs_p, outs_s = [], []
    for l in range(depth):
        lw = dict(g_pre_mix=g_pre_mix[l], g_post_mix=g_post_mix[l], g_pre_ffn=g_pre_ffn[l],
                  g_post_ffn=g_post_ffn[l], w_in=w_in_p[l], conv_w=conv_w[l], hp=hp[l],
                  gdn_norm_w=gdn_norm_w[l], sinks=sinks[l], w_o=w_o_b[l], w_up=w_up_b[l], w_down=w_dn_b[l])
        yp, c1, s1, k1, v1 = _prompt_layer(yp, mod_p[l], lw, batch, seq)
        ys, c2, s2, k2, v2 = _sample_layer(ys, mod_s[l], lw, state_conv[l], state_ssm[l],
                                           cache_swa_k[l], cache_swa_v[l], nseq, steps)
        outs_p.append((c1, s1, k1, v1))
        outs_s.append((c2, s2, k2, v2))

    y_prompt = yp.reshape(batch, seq, D_MODEL)
    y_sample = jnp.transpose(ys.reshape(steps, nseq, D_MODEL), (1, 0, 2))
    stack = lambda outs, i: jnp.stack([o[i] for o in outs])
    return (y_prompt, y_sample,
            stack(outs_p, 0), stack(outs_p, 1), stack(outs_p, 2), stack(outs_p, 3),
            stack(outs_s, 0), stack(outs_s, 1), stack(outs_s, 2), stack(outs_s, 3))
```

```python
import functools

import jax
import jax.numpy as jnp
from jax import lax
from jax.experimental import pallas as pl
from jax.experimental.pallas import tpu as pltpu

F32 = jnp.float32
BF16 = jnp.bfloat16

D_MODEL = 1024
N_MOD = 6
N_HEADS_A = 4
HEAD_DIM_A = 128
WIDTH_A = N_HEADS_A * HEAD_DIM_A
CONV_WIDTH = 4
CONV_CH = 3 * WIDTH_A
N_Q_HEADS_B = 8
N_KV_HEADS_B = 2
GQA_GROUP = N_Q_HEADS_B // N_KV_HEADS_B
HEAD_DIM_B = 64
WIDTH_B = N_Q_HEADS_B * HEAD_DIM_B
KV_WIDTH_B = N_KV_HEADS_B * HEAD_DIM_B
WINDOW = 128
D_FF = 4 * D_MODEL
EPS = 1e-6
NEG_BIG = -1e30

OFF_XC = 0
OFF_Z = OFF_XC + CONV_CH
OFF_QB = OFF_Z + WIDTH_A
OFF_KB = OFF_QB + WIDTH_B
OFF_VB = OFF_KB + KV_WIDTH_B
OFF_BA = OFF_VB + KV_WIDTH_B
LANE = 128
PROJ_COLS = OFF_BA + LANE

GDN_CHUNK = 64
GDN_BLOCK = 512
GDN_UNROLL = 4
TOKEN_TILE = 512
SAMPLE_GROUP = 16
VMEM_LIMIT = 56 * 1024 * 1024


def _silu(x):
    return x * jax.nn.sigmoid(x)


def _softplus(x):
    return jnp.maximum(x, 0.0) + jnp.log(1.0 + jnp.exp(-jnp.abs(x)))


def _rms(x, w):
    return x * lax.rsqrt(jnp.mean(x * x, axis=-1, keepdims=True) + EPS) * w


def _bdot(a, b):
    return jnp.dot(a.astype(BF16), b.astype(BF16), preferred_element_type=F32)


def _bdot_nt(a, b):
    return lax.dot_general(a.astype(BF16), b.astype(BF16), (((1,), (1,)), ((), ())),
                           preferred_element_type=F32)


def _bdot_tn(a, b):
    return lax.dot_general(a.astype(BF16), b.astype(BF16), (((0,), (0,)), ((), ())),
                           preferred_element_type=F32)


def _split3(x):
    hi = x.astype(BF16)
    r1 = x - hi.astype(F32)
    mid = r1.astype(BF16)
    lo = (r1 - mid.astype(F32)).astype(BF16)
    return hi, mid, lo


def _cparams(sem, flags=None):
    return pltpu.CompilerParams(dimension_semantics=sem, vmem_limit_bytes=VMEM_LIMIT, flags=flags)


def _mod_kernel(c_ref, w_ref, b_ref, o_ref):
    s = _silu(c_ref[...]).astype(BF16)
    o_ref[0] = jnp.dot(s, w_ref[0].astype(BF16), preferred_element_type=F32) + b_ref[0]


def _modulation(c_all, w_ada, b_ada):
    depth = w_ada.shape[0]
    rows = c_all.shape[0]
    ncol = N_MOD * D_MODEL
    tn = 1536
    return pl.pallas_call(
        _mod_kernel,
        grid=(depth, ncol // tn),
        in_specs=[
            pl.BlockSpec((rows, D_MODEL), lambda l, j: (0, 0)),
            pl.BlockSpec((1, D_MODEL, tn), lambda l, j: (l, 0, j)),
            pl.BlockSpec((1, 1, tn), lambda l, j: (l, 0, j)),
        ],
        out_specs=pl.BlockSpec((1, rows, tn), lambda l, j: (l, 0, j)),
        out_shape=jax.ShapeDtypeStruct((depth, rows, ncol), F32),
        compiler_params=_cparams(("arbitrary", "arbitrary")),
        name="adaln_mod",
    )(c_all, w_ada, b_ada.reshape(depth, 1, ncol))


def _premix_kernel(x_ref, mod_ref, g_ref, w_ref, o_ref):
    x = x_ref[...]
    h = _rms(x, g_ref[...]) * (1.0 + mod_ref[0, 1]) + mod_ref[0, 0]
    o_ref[...] = jnp.dot(h.astype(BF16), w_ref[...], preferred_element_type=F32)


def _premix(x, mod, g, w, layer, tm, rep):
    n = x.shape[0]
    r = mod.shape[3]
    return pl.pallas_call(
        _premix_kernel,
        grid=(n // tm,),
        in_specs=[
            pl.BlockSpec((tm, D_MODEL), lambda i: (i, 0)),
            pl.BlockSpec((None, 1, N_MOD, r, D_MODEL), lambda i: (layer, i // rep, 0, 0, 0)),
            pl.BlockSpec((None, 1, D_MODEL), lambda i: (layer, 0, 0)),
            pl.BlockSpec((None, D_MODEL, PROJ_COLS), lambda i: (layer, 0, 0)),
        ],
        out_specs=pl.BlockSpec((tm, PROJ_COLS), lambda i: (i, 0)),
        out_shape=jax.ShapeDtypeStruct((n, PROJ_COLS), F32),
        compiler_params=_cparams(("arbitrary",)),
        name="premix_proj",
    )(x, mod, g, w)


def _ffn_kernel(x_ref, oa_ref, ob_ref, mod_ref, gpm_ref, gpf_ref, gqf_ref, wo_ref, wup_ref, wdn_ref, o_ref):
    x = x_ref[...]
    o_mix = jnp.concatenate([oa_ref[...], ob_ref[...]], axis=1).astype(BF16)
    mix = jnp.dot(o_mix, wo_ref[...], preferred_element_type=F32)
    x1 = x + mod_ref[0, 2] * _rms(mix, gpm_ref[...])
    h = (_rms(x1, gpf_ref[...]) * (1.0 + mod_ref[0, 4]) + mod_ref[0, 3]).astype(BF16)
    ck = 1024
    acc = jnp.zeros(x.shape, F32)
    for c in range(D_FF // ck):
        u = jnp.dot(h, wup_ref[:, c * ck:(c + 1) * ck], preferred_element_type=F32)
        a = jnp.square(jnp.maximum(u, 0.0)).astype(BF16)
        acc = acc + jnp.dot(a, wdn_ref[c * ck:(c + 1) * ck, :], preferred_element_type=F32)
    o_ref[...] = x1 + mod_ref[0, 5] * _rms(acc, gqf_ref[...])


def _ffn(x, oa, ob, mod, g_post_mix, g_pre_ffn, g_post_ffn, wo, wup, wdn, layer, tm, rep):
    n = x.shape[0]
    r = mod.shape[3]
    const = lambda i: (layer, 0, 0)
    return pl.pallas_call(
        _ffn_kernel,
        grid=(n // tm,),
        in_specs=[
            pl.BlockSpec((tm, D_MODEL), lambda i: (i, 0)),
            pl.BlockSpec((tm, WIDTH_A), lambda i: (i, 0)),
            pl.BlockSpec((tm, WIDTH_B), lambda i: (i, 0)),
            pl.BlockSpec((None, 1, N_MOD, r, D_MODEL), lambda i: (layer, i // rep, 0, 0, 0)),
            pl.BlockSpec((None, 1, D_MODEL), const),
            pl.BlockSpec((None, 1, D_MODEL), const),
            pl.BlockSpec((None, 1, D_MODEL), const),
            pl.BlockSpec((None, D_MODEL, D_MODEL), const, pipeline_mode=pl.Buffered(1)),
            pl.BlockSpec((None, D_MODEL, D_FF), const, pipeline_mode=pl.Buffered(1)),
            pl.BlockSpec((None, D_FF, D_MODEL), const, pipeline_mode=pl.Buffered(1)),
        ],
        out_specs=pl.BlockSpec((tm, D_MODEL), lambda i: (i, 0)),
        out_shape=jax.ShapeDtypeStruct((n, D_MODEL), F32),
        compiler_params=_cparams(("arbitrary",)),
        name="outproj_ffn",
    )(x, oa, ob, mod, g_post_mix, g_pre_ffn, g_post_ffn, wo, wup, wdn)


def _lane_bcast(x, col, width=HEAD_DIM_A):
    return jnp.broadcast_to(x[:, col:col + 1], (x.shape[0], width))


def _gdn_prompt_kernel(xc_ref, z_ref, ba_ref, cw_ref, hp_ref, gnw_ref, o_ref, ssm_ref,
                       s_scr, xbuf, *, chunk, block, group):
    C = chunk
    G = group
    H = N_HEADS_A
    HD = HEAD_DIM_A
    PW = H * C
    j = pl.program_id(1)
    nj = pl.num_programs(1)

    @pl.when(j == 0)
    def _():
        s_scr[...] = jnp.zeros(s_scr.shape, F32)
        xbuf[0:8, :] = jnp.zeros((8, CONV_CH), F32)

    @pl.when(j > 0)
    def _():
        xbuf[0:8, :] = xbuf[block:block + 8, :]

    xbuf[8:block + 8, :] = xc_ref[...]

    ri = lax.broadcasted_iota(jnp.int32, (C, PW), 0)
    li = lax.broadcasted_iota(jnp.int32, (C, PW), 1)
    lc = li % C
    lower = ri >= lc
    strict = ri > lc
    eye = ri == lc
    rb = lax.broadcasted_iota(jnp.int32, (PW, PW), 0) // C
    cb = lax.broadcasted_iota(jnp.int32, (PW, PW), 1) // C
    bd_mask = rb == cb
    kbd_mask = (lax.broadcasted_iota(jnp.int32, (PW, WIDTH_A), 0) // C
                == lax.broadcasted_iota(jnp.int32, (PW, WIDTH_A), 1) // HD)
    rhs_mask = (lax.broadcasted_iota(jnp.int32, (PW, 2 * WIDTH_A), 0) // C
                == (lax.broadcasted_iota(jnp.int32, (PW, 2 * WIDTH_A), 1) % WIDTH_A) // HD)
    tri = (lax.broadcasted_iota(jnp.int32, (C, C), 0)
           >= lax.broadcasted_iota(jnp.int32, (C, C), 1)).astype(BF16)
    lane128 = lax.broadcasted_iota(jnp.int32, (C, LANE), 1)

    neg_a = -jnp.exp(hp_ref[0:1, :])
    dt_b = hp_ref[1:2, :]
    gnw = gnw_ref[...]

    def bd(y):
        return jnp.where(bd_mask, jnp.concatenate([y] * H, axis=0), 0.0)

    def pmm(x, y):
        return _bdot(x, bd(y))

    def pack_cols(cols):
        per_tile = LANE // C
        tiles = []
        for t in range(H // per_tile):
            acc = cols[t * per_tile]
            for u in range(1, per_tile):
                acc = jnp.where(lane128 < u * C, acc, cols[t * per_tile + u])
            tiles.append(acc)
        return jnp.concatenate(tiles, axis=1)

    def prepare(r0):
        win = xbuf[pl.ds(r0, C + 8), :]
        conv = win[8:] * cw_ref[CONV_WIDTH - 1:CONV_WIDTH, :]
        for s in range(1, CONV_WIDTH):
            conv = conv + pltpu.roll(win, s, 0)[8:] * cw_ref[CONV_WIDTH - 1 - s:CONV_WIDTH - s, :]
        xs = _silu(conv)
        ba = ba_ref[pl.ds(r0, C), :]
        beta_all = jax.nn.sigmoid(ba)
        g_all = neg_a * _softplus(ba + dt_b)
        g_hi, g_mid, g_lo = _split3(g_all)
        g3 = jnp.dot(tri, jnp.concatenate([g_hi, g_mid, g_lo], axis=1), preferred_element_type=F32)
        gc_all = g3[:, 0:LANE] + g3[:, LANE:2 * LANE] + g3[:, 2 * LANE:3 * LANE]

        q, k, v, kb, beta_c, gc_c, eg_c = [], [], [], [], [], [], []
        for h in range(H):
            qh = xs[:, h * HD:(h + 1) * HD]
            kh = xs[:, WIDTH_A + h * HD:WIDTH_A + (h + 1) * HD]
            vh = xs[:, 2 * WIDTH_A + h * HD:2 * WIDTH_A + (h + 1) * HD]
            qh = qh * lax.rsqrt(jnp.sum(qh * qh, axis=-1, keepdims=True) + EPS) * (HD ** -0.5)
            kh = kh * lax.rsqrt(jnp.sum(kh * kh, axis=-1, keepdims=True) + EPS)
            bh = _lane_bcast(beta_all, h)
            gh = _lane_bcast(gc_all, N_HEADS_A + h)
            q.append(qh)
            k.append(kh)
            v.append(vh)
            kb.append(kh * bh)
            beta_c.append(bh)
            gc_c.append(gh)
            eg_c.append(jnp.exp(gh))

        gcp = pack_cols(gc_c)
        gc_row = jnp.sum(jnp.where(eye, gcp, 0.0), axis=0, keepdims=True)
        decay = jnp.exp(jnp.where(lower, gcp - gc_row, NEG_BIG))

        k_p = jnp.concatenate(k, axis=1)
        kbd = jnp.where(kbd_mask, jnp.concatenate([k_p] * H, axis=0), 0.0)
        lhs = jnp.concatenate([jnp.concatenate(kb, axis=1), jnp.concatenate(q, axis=1)], axis=0)
        kq = _bdot_nt(lhs, kbd)
        lmat = jnp.where(strict, kq[0:C] * decay, 0.0)
        rhs = jnp.concatenate([v[h] * beta_c[h] for h in range(H)]
                              + [kb[h] * eg_c[h] for h in range(H)], axis=1)
        g_last = [gc_all[C - 1:C, N_HEADS_A + h:N_HEADS_A + h + 1] for h in range(H)]
        return dict(lmat=lmat, attn=kq[C:2 * C] * decay, rhs=rhs, g_last=g_last,
                    qg=[q[h] * eg_c[h] for h in range(H)],
                    kg=[k[h] * jnp.exp(g_last[h] - gc_c[h]) for h in range(H)])

    base_blk = min(16, C)
    same_base = (ri // base_blk) == (lc // base_blk)

    def step(i, carry):
        r0s = [pl.multiple_of((i * G + g) * C, C) for g in range(G)]
        ch = [prepare(r0) for r0 in r0s]
        pw = [jnp.where(same_base, c["lmat"], 0.0) for c in ch]
        p = [jnp.where(eye, 1.0, 0.0) - x for x in pw]
        span = 2
        while span < base_blk:
            pw = [pmm(x, x) for x in pw]
            p = [x + pmm(x, y) for x, y in zip(p, pw)]
            span *= 2
        blk = base_blk
        same = same_base
        while blk < C:
            nxt = 2 * blk
            same_n = (ri // nxt) == (lc // nxt)
            off = jnp.logical_and(same_n, jnp.logical_not(same))
            t = [pmm(jnp.where(off, c["lmat"], 0.0), x) for c, x in zip(ch, p)]
            p = [x - pmm(x, y) for x, y in zip(p, t)]
            same = same_n
            blk = nxt

        sol = [_bdot(x, jnp.where(rhs_mask, jnp.concatenate([c["rhs"]] * H, axis=0), 0.0))
               for c, x in zip(ch, p)]
        asol = [_bdot(c["attn"], jnp.where(rhs_mask, jnp.concatenate([x] * H, axis=0), 0.0))
                for c, x in zip(ch, sol)]
        ab = [[_bdot_tn(c["kg"][h], jnp.concatenate(
                   [x[:, WIDTH_A + h * HD:WIDTH_A + (h + 1) * HD], x[:, h * HD:(h + 1) * HD]], axis=1))
               for h in range(H)] for c, x in zip(ch, sol)]
        for g in range(G):
            for h in range(H):
                q_eff = ch[g]["qg"][h] - asol[g][:, WIDTH_A + h * HD:WIDTH_A + (h + 1) * HD]
                sh = s_scr[h]
                o_h = _bdot(q_eff, sh) + asol[g][:, h * HD:(h + 1) * HD]
                s_scr[h] = (sh * jnp.exp(ch[g]["g_last"][h]) - _bdot(ab[g][h][:, 0:HD], sh)
                            + ab[g][h][:, HD:2 * HD])
                z_h = z_ref[pl.ds(r0s[g], C), h * HD:(h + 1) * HD]
                o_ref[pl.ds(r0s[g], C), h * HD:(h + 1) * HD] = _rms(o_h, gnw) * _silu(z_h)
        return carry

    lax.fori_loop(0, block // (G * C), step, 0)

    @pl.when(j == nj - 1)
    def _():
        ssm_ref[0] = s_scr[...]


def _gdn_prompt(proj, conv_w, hp, gnw, layer, batch, seq):
    blk = min(GDN_BLOCK, seq)
    nblk = seq // blk
    chunk = min(GDN_CHUNK, seq)
    kern = functools.partial(_gdn_prompt_kernel, chunk=chunk, block=blk, group=min(GDN_UNROLL, blk // chunk))
    return pl.pallas_call(
        kern,
        grid=(batch, nblk),
        in_specs=[
            pl.BlockSpec((blk, CONV_CH), lambda b, j: (b * nblk + j, OFF_XC // CONV_CH)),
            pl.BlockSpec((blk, WIDTH_A), lambda b, j: (b * nblk + j, OFF_Z // WIDTH_A)),
            pl.BlockSpec((blk, LANE), lambda b, j: (b * nblk + j, OFF_BA // LANE)),
            pl.BlockSpec((None, CONV_WIDTH, CONV_CH), lambda b, j: (layer, 0, 0)),
            pl.BlockSpec((None, 8, LANE), lambda b, j: (layer, 0, 0)),
            pl.BlockSpec((None, 1, HEAD_DIM_A), lambda b, j: (layer, 0, 0)),
        ],
        out_specs=[
            pl.BlockSpec((blk, WIDTH_A), lambda b, j: (b * nblk + j, 0)),
            pl.BlockSpec((1, N_HEADS_A, HEAD_DIM_A, HEAD_DIM_A), lambda b, j: (b, 0, 0, 0)),
        ],
        out_shape=[
            jax.ShapeDtypeStruct((batch * seq, WIDTH_A), F32),
            jax.ShapeDtypeStruct((batch, N_HEADS_A, HEAD_DIM_A, HEAD_DIM_A), F32),
        ],
        scratch_shapes=[
            pltpu.VMEM((N_HEADS_A, HEAD_DIM_A, HEAD_DIM_A), F32),
            pltpu.VMEM((blk + 8, CONV_CH), F32),
        ],
        compiler_params=_cparams(("arbitrary", "arbitrary")),
        name="gdn_prompt",
    )(proj, proj, proj, conv_w, hp, gnw)


def _sink_column(sink_ref, layer, h, shape, axis, period):
    g = (lax.broadcasted_iota(jnp.int32, shape, axis) // period) % GQA_GROUP
    col = jnp.full(shape, sink_ref[layer, h * GQA_GROUP], F32)
    for gg in range(1, GQA_GROUP):
        col = jnp.where(g == gg, sink_ref[layer, h * GQA_GROUP + gg], col)
    return col


def _swa_prompt_kernel(sink_ref, q_ref, kc_ref, kp_ref, vc_ref, vp_ref, o_ref, *, layer):
    W = WINDOW
    j = pl.program_id(1)
    q = q_ref[...]
    kcat = jnp.concatenate([kp_ref[...], kc_ref[...]], axis=0).astype(BF16)
    vcat = jnp.concatenate([vp_ref[...], vc_ref[...]], axis=0).astype(BF16)
    rows = GQA_GROUP * W
    qi = lax.broadcasted_iota(jnp.int32, (rows, 2 * W), 0) % W
    kj = lax.broadcasted_iota(jnp.int32, (rows, 2 * W), 1)
    first_key = jnp.where(j > 0, 0, W)
    mask = (kj > qi) & (kj <= qi + W) & (kj >= first_key)
    for h in range(N_KV_HEADS_B):
        k_h = kcat[:, h * HEAD_DIM_B:(h + 1) * HEAD_DIM_B]
        v_h = vcat[:, h * HEAD_DIM_B:(h + 1) * HEAD_DIM_B]
        qs = jnp.concatenate(
            [q[:, (h * GQA_GROUP + g) * HEAD_DIM_B:(h * GQA_GROUP + g + 1) * HEAD_DIM_B]
             for g in range(GQA_GROUP)], axis=0)
        s = _bdot_nt(qs, k_h) * (HEAD_DIM_B ** -0.5)
        s = jnp.where(mask, s, NEG_BIG)
        sk = _sink_column(sink_ref, layer, h, (rows, 1), 0, W)
        m = jnp.maximum(jnp.max(s, axis=-1, keepdims=True), sk)
        p = jnp.exp(s - m)
        p = p / (jnp.sum(p, axis=-1, keepdims=True) + jnp.exp(sk - m))
        o = jnp.dot(p.astype(BF16), v_h, preferred_element_type=F32)
        for g in range(GQA_GROUP):
            c0 = (h * GQA_GROUP + g) * HEAD_DIM_B
            o_ref[:, c0:c0 + HEAD_DIM_B] = o[g * W:(g + 1) * W, :]


def _swa_prompt(proj, sinks, layer, batch, seq):
    nb = seq // WINDOW
    kcol = OFF_KB // KV_WIDTH_B
    vcol = OFF_VB // KV_WIDTH_B
    return pl.pallas_call(
        functools.partial(_swa_prompt_kernel, layer=layer),
        grid=(batch, nb),
        in_specs=[
            pl.BlockSpec(memory_space=pltpu.SMEM),
            pl.BlockSpec((WINDOW, WIDTH_B), lambda b, j: (b * nb + j, OFF_QB // WIDTH_B)),
            pl.BlockSpec((WINDOW, KV_WIDTH_B), lambda b, j: (b * nb + j, kcol)),
            pl.BlockSpec((WINDOW, KV_WIDTH_B), lambda b, j: (b * nb + jnp.maximum(j - 1, 0), kcol)),
            pl.BlockSpec((WINDOW, KV_WIDTH_B), lambda b, j: (b * nb + j, vcol)),
            pl.BlockSpec((WINDOW, KV_WIDTH_B), lambda b, j: (b * nb + jnp.maximum(j - 1, 0), vcol)),
        ],
        out_specs=pl.BlockSpec((WINDOW, WIDTH_B), lambda b, j: (b * nb + j, 0)),
        out_shape=jax.ShapeDtypeStruct((batch * seq, WIDTH_B), F32),
        compiler_params=_cparams(("arbitrary", "arbitrary")),
        name="swa_prompt",
    )(sinks, proj, proj, proj, proj, proj)


def _gdn_sample_kernel(xc_ref, z_ref, ba_ref, cs_ref, s0_ref, cw_ref, hp_ref, gnw_ref,
                       o_ref, s1_ref, lhs_scr, u_scr, kg_scr, res_scr, gl_scr, *, steps, group):
    T = steps
    GB = group
    H = N_HEADS_A
    HD = HEAD_DIM_A
    R = 2 * T

    u_scr[...] = jnp.zeros(u_scr.shape, F32)
    kg_scr[...] = jnp.zeros(kg_scr.shape, F32)

    xp = [cs_ref[i] for i in range(CONV_WIDTH - 1)] + [xc_ref[t] for t in range(T)]
    xs = []
    for t in range(T):
        acc = xp[t] * cw_ref[0:1, :]
        for i in range(1, CONV_WIDTH):
            acc = acc + xp[t + i] * cw_ref[i:i + 1, :]
        xs.append(_silu(acc))

    neg_a = -jnp.exp(hp_ref[0:1, :])
    dt_b = hp_ref[1:2, :]
    beta_all = [jax.nn.sigmoid(ba_ref[t]) for t in range(T)]
    g_all = [neg_a * _softplus(ba_ref[t] + dt_b) for t in range(T)]
    gc_all = [g_all[0]]
    for t in range(1, T):
        gc_all.append(gc_all[t - 1] + g_all[t])

    u_keep, attn_keep = [], []
    for h in range(H):
        q, k, v, kb, beta, gc = [], [], [], [], [], []
        for t in range(T):
            qh = xs[t][:, h * HD:(h + 1) * HD]
            kh = xs[t][:, WIDTH_A + h * HD:WIDTH_A + (h + 1) * HD]
            vh = xs[t][:, 2 * WIDTH_A + h * HD:2 * WIDTH_A + (h + 1) * HD]
            qh = qh * lax.rsqrt(jnp.sum(qh * qh, axis=-1, keepdims=True) + EPS) * (HD ** -0.5)
            kh = kh * lax.rsqrt(jnp.sum(kh * kh, axis=-1, keepdims=True) + EPS)
            bt = beta_all[t][:, h:h + 1]
            q.append(qh)
            k.append(kh)
            v.append(vh)
            kb.append(kh * bt)
            beta.append(bt)
            gc.append(gc_all[t][:, H + h:H + h + 1])
        u, w, attn = [], [], []
        for t in range(T):
            ut = v[t] * beta[t]
            wt = kb[t] * jnp.exp(gc[t])
            arow = []
            for s in range(t + 1):
                dec = jnp.exp(gc[t] - gc[s])
                arow.append(jnp.sum(q[t] * k[s], axis=-1, keepdims=True) * dec)
                if s < t:
                    l_ts = jnp.sum(kb[t] * k[s], axis=-1, keepdims=True) * dec
                    ut = ut - l_ts * u[s]
                    wt = wt - l_ts * w[s]
            u.append(ut)
            w.append(wt)
            attn.append(arow)
        g_last = gc[T - 1]
        for t in range(T):
            row0 = h * GB * R
            lhs_scr[pl.ds(row0 + t, GB, stride=R), :] = w[t]
            lhs_scr[pl.ds(row0 + T + t, GB, stride=R), :] = q[t] * jnp.exp(gc[t])
            u_scr[pl.ds(row0 + t, GB, stride=R), :] = u[t]
            kg_scr[pl.ds(row0 + t, GB, stride=R), :] = k[t] * jnp.exp(g_last - gc[t])
        gl_scr[h] = jnp.broadcast_to(jnp.exp(g_last), (GB, HD))
        u_keep.append(u)
        attn_keep.append(attn)

    def body(b, carry):
        for h in range(H):
            row = pl.multiple_of((h * GB + b) * R, R)
            s = s0_ref[b, h]
            res = _bdot(lhs_scr[pl.ds(row, R), :], s)
            res_scr[pl.ds(row, R), :] = res
            vn = u_scr[pl.ds(row, R), :] - res
            upd = _bdot_tn(kg_scr[pl.ds(row, R), :], vn)
            s1_ref[b, h] = s * gl_scr[h, pl.ds(b, 1), :] + upd
        return carry

    lax.fori_loop(0, GB, body, 0)

    gnw = gnw_ref[...]
    for h in range(H):
        row0 = h * GB * R
        vn = []
        for t in range(T):
            ws_t = res_scr[pl.ds(row0 + t, GB, stride=R), :]
            vn.append(u_keep[h][t] - ws_t)
        for t in range(T):
            o_t = res_scr[pl.ds(row0 + T + t, GB, stride=R), :]
            for s in range(t + 1):
                o_t = o_t + attn_keep[h][t][s] * vn[s]
            z_t = z_ref[t][:, h * HD:(h + 1) * HD]
            o_ref[t, :, h * HD:(h + 1) * HD] = _rms(o_t, gnw) * _silu(z_t)


def _gdn_sample(proj3, cs_tm, s0, conv_w, hp, gnw, layer):
    steps, nseq, _ = proj3.shape
    gb = min(SAMPLE_GROUP, nseq)
    rows = N_HEADS_A * gb * 2 * steps
    kern = functools.partial(_gdn_sample_kernel, steps=steps, group=gb)
    return pl.pallas_call(
        kern,
        grid=(nseq // gb,),
        in_specs=[
            pl.BlockSpec((steps, gb, CONV_CH), lambda i: (0, i, OFF_XC // CONV_CH)),
            pl.BlockSpec((steps, gb, WIDTH_A), lambda i: (0, i, OFF_Z // WIDTH_A)),
            pl.BlockSpec((steps, gb, LANE), lambda i: (0, i, OFF_BA // LANE)),
            pl.BlockSpec((None, CONV_WIDTH - 1, gb, CONV_CH), lambda i: (layer, 0, i, 0)),
            pl.BlockSpec((None, gb, N_HEADS_A, HEAD_DIM_A, HEAD_DIM_A), lambda i: (layer, i, 0, 0, 0)),
            pl.BlockSpec((None, CONV_WIDTH, CONV_CH), lambda i: (layer, 0, 0)),
            pl.BlockSpec((None, 8, LANE), lambda i: (layer, 0, 0)),
            pl.BlockSpec((None, 1, HEAD_DIM_A), lambda i: (layer, 0, 0)),
        ],
        out_specs=[
            pl.BlockSpec((steps, gb, WIDTH_A), lambda i: (0, i, 0)),
            pl.BlockSpec((gb, N_HEADS_A, HEAD_DIM_A, HEAD_DIM_A), lambda i: (i, 0, 0, 0)),
        ],
        out_shape=[
            jax.ShapeDtypeStruct((steps, nseq, WIDTH_A), F32),
            jax.ShapeDtypeStruct(s0.shape[1:], F32),
        ],
        scratch_shapes=[
            pltpu.VMEM((rows, HEAD_DIM_A), F32),
            pltpu.VMEM((rows, HEAD_DIM_A), F32),
            pltpu.VMEM((rows, HEAD_DIM_A), F32),
            pltpu.VMEM((rows, HEAD_DIM_A), F32),
            pltpu.VMEM((N_HEADS_A, gb, HEAD_DIM_A), F32),
        ],
        compiler_params=_cparams(("arbitrary",)),
        name="gdn_sample",
    )(proj3, proj3, proj3, cs_tm, s0, conv_w, hp, gnw)


def _swa_sample_kernel(sink_ref, q_ref, kc_ref, vc_ref, kn_ref, vn_ref, o_ref, *, steps, layer):
    T = steps
    W = WINDOW
    rows = T * GQA_GROUP
    gb = q_ref.shape[0]
    t_of_row = lax.broadcasted_iota(jnp.int32, (1, rows, 1), 1) // GQA_GROUP
    kj = lax.broadcasted_iota(jnp.int32, (1, 1, W), 2)
    cmask = kj > t_of_row
    for h in range(N_KV_HEADS_B):
        lo, hi = h * HEAD_DIM_B, (h + 1) * HEAD_DIM_B
        qh = q_ref[:, h]
        qh_r = qh.astype(BF16)
        kch = kc_ref[:, :, lo:hi].astype(BF16)
        vch = vc_ref[:, :, lo:hi].astype(BF16)
        sc = jnp.einsum("bqd,bkd->bqk", qh_r, kch, preferred_element_type=F32) * (HEAD_DIM_B ** -0.5)
        sc = jnp.where(cmask, sc, NEG_BIG)
        sn = []
        for s in range(T):
            kn_s = kn_ref[:, s:s + 1, lo:hi]
            v = jnp.sum(qh * kn_s, axis=-1, keepdims=True) * (HEAD_DIM_B ** -0.5)
            sn.append(jnp.where(t_of_row >= s, v, NEG_BIG))
        sk = _sink_column(sink_ref, layer, h, (1, rows, 1), 1, 1)
        m = jnp.maximum(jnp.max(sc, axis=-1, keepdims=True), sk)
        for s in range(T):
            m = jnp.maximum(m, sn[s])
        pc = jnp.exp(sc - m)
        pn = [jnp.exp(sn[s] - m) for s in range(T)]
        den = jnp.sum(pc, axis=-1, keepdims=True) + jnp.exp(sk - m)
        for s in range(T):
            den = den + pn[s]
        inv = 1.0 / den
        o = jnp.einsum("bqk,bkd->bqd", (pc * inv).astype(BF16), vch, preferred_element_type=F32)
        for s in range(T):
            o = o + (pn[s] * inv) * vn_ref[:, s:s + 1, lo:hi]
        o_ref[:, h] = o


def _swa_sample(q_r, kc, vc, kn, vn, sinks, layer):
    nseq, _, rows, _ = q_r.shape
    steps = kn.shape[1]
    gb = min(SAMPLE_GROUP, nseq)
    kern = functools.partial(_swa_sample_kernel, steps=steps, layer=layer)
    return pl.pallas_call(
        kern,
        grid=(nseq // gb,),
        in_specs=[
            pl.BlockSpec(memory_space=pltpu.SMEM),
            pl.BlockSpec((gb, N_KV_HEADS_B, rows, HEAD_DIM_B), lambda i: (i, 0, 0, 0)),
            pl.BlockSpec((None, gb, WINDOW, KV_WIDTH_B), lambda i: (layer, i, 0, 0)),
            pl.BlockSpec((None, gb, WINDOW, KV_WIDTH_B), lambda i: (layer, i, 0, 0)),
            pl.BlockSpec((gb, steps, KV_WIDTH_B), lambda i: (i, 0, 0)),
            pl.BlockSpec((gb, steps, KV_WIDTH_B), lambda i: (i, 0, 0)),
        ],
        out_specs=pl.BlockSpec((gb, N_KV_HEADS_B, rows, HEAD_DIM_B), lambda i: (i, 0, 0, 0)),
        out_shape=jax.ShapeDtypeStruct(q_r.shape, F32),
        compiler_params=_cparams(("arbitrary",)),
        name="swa_sample",
    )(sinks, q_r, kc, vc, kn, vn)


def _prompt_layer(x, p, layer, batch, seq):
    tm = min(TOKEN_TILE, seq)
    rep = seq // tm
    proj = _premix(x, p["mod_p"], p["g_pre_mix"], p["w_in"], layer, tm, rep)
    o_a, ssm = _gdn_prompt(proj, p["conv_w"], p["hp"], p["gdn_norm_w"], layer, batch, seq)
    o_b = _swa_prompt(proj, p["sinks"], layer, batch, seq)
    x = _ffn(x, o_a, o_b, p["mod_p"], p["g_post_mix"], p["g_pre_ffn"], p["g_post_ffn"],
             p["w_o"], p["w_up"], p["w_down"], layer, tm, rep)
    p3 = proj.reshape(batch, seq, PROJ_COLS)
    conv = p3[:, seq - (CONV_WIDTH - 1):, OFF_XC:OFF_XC + CONV_CH]
    k_new = p3[:, seq - WINDOW:, OFF_KB:OFF_KB + KV_WIDTH_B].reshape(batch, WINDOW, N_KV_HEADS_B, HEAD_DIM_B)
    v_new = p3[:, seq - WINDOW:, OFF_VB:OFF_VB + KV_WIDTH_B].reshape(batch, WINDOW, N_KV_HEADS_B, HEAD_DIM_B)
    return x, conv, ssm, k_new, v_new


def _sample_layer(x, p, layer, nseq, steps):
    proj = _premix(x, p["mod_s"], p["g_pre_mix"], p["w_in"], layer, nseq, steps)
    p3 = proj.reshape(steps, nseq, PROJ_COLS)
    o_a, ssm = _gdn_sample(p3, p["conv_tm"], p["state_ssm"], p["conv_w"], p["hp"], p["gdn_norm_w"], layer)
    q_r = p3[:, :, OFF_QB:OFF_QB + WIDTH_B].reshape(steps, nseq, N_KV_HEADS_B, GQA_GROUP, HEAD_DIM_B)
    q_r = jnp.transpose(q_r, (1, 2, 0, 3, 4)).reshape(nseq, N_KV_HEADS_B, steps * GQA_GROUP, HEAD_DIM_B)
    k_new = jnp.transpose(p3[:, :, OFF_KB:OFF_KB + KV_WIDTH_B], (1, 0, 2))
    v_new = jnp.transpose(p3[:, :, OFF_VB:OFF_VB + KV_WIDTH_B], (1, 0, 2))
    o_b = _swa_sample(q_r, p["cache_k"], p["cache_v"], k_new, v_new, p["sinks"], layer)
    o_b = o_b.reshape(nseq, N_KV_HEADS_B, steps, GQA_GROUP, HEAD_DIM_B)
    o_b = jnp.transpose(o_b, (2, 0, 1, 3, 4)).reshape(steps * nseq, WIDTH_B)
    x = _ffn(x, o_a.reshape(steps * nseq, WIDTH_A), o_b, p["mod_s"], p["g_post_mix"], p["g_pre_ffn"],
             p["g_post_ffn"], p["w_o"], p["w_up"], p["w_down"], layer, nseq, steps)
    xp = jnp.concatenate([p["conv_tm"][layer], p3[:, :, OFF_XC:OFF_XC + CONV_CH]], axis=0)
    conv = jnp.transpose(xp[xp.shape[0] - (CONV_WIDTH - 1):], (1, 0, 2))
    return x, conv, ssm, k_new, v_new


def _pack_w_in(w_in):
    o1 = CONV_CH + WIDTH_A
    o2 = o1 + 2 * N_HEADS_A
    pad = jnp.zeros(w_in.shape[:-1] + (PROJ_COLS - OFF_BA - 2 * N_HEADS_A,), w_in.dtype)
    return jnp.concatenate([w_in[..., :o1], w_in[..., o2:], w_in[..., o1:o2], pad], axis=-1).astype(BF16)


def _head_params(a_log, dt_bias):
    depth = a_log.shape[0]
    hp = jnp.zeros((depth, 8, LANE), F32)
    hp = hp.at[:, 0, N_HEADS_A:2 * N_HEADS_A].set(a_log.astype(F32))
    hp = hp.at[:, 1, N_HEADS_A:2 * N_HEADS_A].set(dt_bias.astype(F32))
    return hp


def kernel(x_prompt, x_sample, state_conv, state_ssm, cache_swa_k, cache_swa_v, c_prompt, c_sample,
           g_pre_mix, g_post_mix, g_pre_ffn, g_post_ffn, w_ada, b_ada, w_in, conv_w, a_log, dt_bias,
           gdn_norm_w, sinks, w_o, w_up, w_down):
    depth = w_ada.shape[0]
    batch, seq, _ = x_prompt.shape
    nseq, steps, _ = x_sample.shape

    c_all = jnp.concatenate([c_prompt, c_sample], axis=0)
    rows = c_all.shape[0]
    rows_p = -(-rows // 8) * 8
    c_all = jnp.pad(c_all, ((0, rows_p - rows), (0, 0)))
    mod_all = _modulation(c_all, w_ada, b_ada).reshape(depth, rows_p, N_MOD, D_MODEL)
    p = dict(
        mod_p=mod_all[:, :batch].reshape(depth, batch, N_MOD, 1, D_MODEL),
        mod_s=jnp.transpose(mod_all[:, batch:batch + nseq], (0, 2, 1, 3)).reshape(depth, 1, N_MOD, nseq, D_MODEL),
        g_pre_mix=g_pre_mix.reshape(depth, 1, D_MODEL), g_post_mix=g_post_mix.reshape(depth, 1, D_MODEL),
        g_pre_ffn=g_pre_ffn.reshape(depth, 1, D_MODEL), g_post_ffn=g_post_ffn.reshape(depth, 1, D_MODEL),
        w_in=_pack_w_in(w_in), conv_w=conv_w, hp=_head_params(a_log, dt_bias),
        gdn_norm_w=gdn_norm_w.reshape(depth, 1, HEAD_DIM_A), sinks=sinks,
        w_o=w_o.astype(BF16), w_up=w_up.astype(BF16), w_down=w_down.astype(BF16),
        conv_tm=jnp.transpose(state_conv, (0, 2, 1, 3)), state_ssm=state_ssm,
        cache_k=cache_swa_k.reshape(depth, nseq, WINDOW, KV_WIDTH_B),
        cache_v=cache_swa_v.reshape(depth, nseq, WINDOW, KV_WIDTH_B),
    )

    yp = x_prompt.reshape(batch * seq, D_MODEL)
    ys = jnp.transpose(x_sample, (1, 0, 2)).reshape(steps * nseq, D_MODEL)
    outs_p, outs_s = [], []
    for l in range(depth):
        yp, c1, s1, k1, v1 = _prompt_layer(yp, p, l, batch, seq)
        ys, c2, s2, k2, v2 = _sample_layer(ys, p, l, nseq, steps)
        outs_p.append((c1, s1, k1, v1))
        outs_s.append((c2, s2, k2, v2))

    y_prompt = yp.reshape(batch, seq, D_MODEL)
    y_sample = jnp.transpose(ys.reshape(steps, nseq, D_MODEL), (1, 0, 2))
    stack = lambda outs, i: jnp.stack([o[i] for o in outs])
    kv_shape = (depth, nseq, steps, N_KV_HEADS_B, HEAD_DIM_B)
    k_s = jnp.concatenate([cache_swa_k, stack(outs_s, 2).reshape(kv_shape)], axis=2)[:, :, -WINDOW:]
    v_s = jnp.concatenate([cache_swa_v, stack(outs_s, 3).reshape(kv_shape)], axis=2)[:, :, -WINDOW:]
    return (y_prompt, y_sample,
            stack(outs_p, 0), stack(outs_p, 1), stack(outs_p, 2), stack(outs_p, 3),
            stack(outs_s, 0), stack(outs_s, 1), k_s, v_s)
```

```python
import functools

import jax
import jax.numpy as jnp
from jax import lax
from jax.experimental import pallas as pl
from jax.experimental.pallas import tpu as pltpu

F32 = jnp.float32
BF16 = jnp.bfloat16

D_MODEL = 1024
N_MOD = 6
N_HEADS_A = 4
HEAD_DIM_A = 128
WIDTH_A = N_HEADS_A * HEAD_DIM_A
CONV_WIDTH = 4
CONV_CH = 3 * WIDTH_A
N_Q_HEADS_B = 8
N_KV_HEADS_B = 2
GQA_GROUP = N_Q_HEADS_B // N_KV_HEADS_B
HEAD_DIM_B = 64
WIDTH_B = N_Q_HEADS_B * HEAD_DIM_B
KV_WIDTH_B = N_KV_HEADS_B * HEAD_DIM_B
WINDOW = 128
D_FF = 4 * D_MODEL
EPS = 1e-6
NEG_BIG = -1e30

OFF_XC = 0
OFF_Z = OFF_XC + CONV_CH
OFF_QB = OFF_Z + WIDTH_A
OFF_KB = OFF_QB + WIDTH_B
OFF_VB = OFF_KB + KV_WIDTH_B
OFF_BA = OFF_VB + KV_WIDTH_B
LANE = 128
PROJ_COLS = OFF_BA + LANE

GDN_CHUNK = 64
GDN_BLOCK = 512
GDN_UNROLL = 8
SWA_BLOCKS = 2
TOKEN_TILE = 512
SAMPLE_GROUP = 16
VMEM_LIMIT = 56 * 1024 * 1024


def _silu(x):
    return x * jax.nn.sigmoid(x)


def _softplus(x):
    return jnp.maximum(x, 0.0) + jnp.log(1.0 + jnp.exp(-jnp.abs(x)))


def _rms(x, w):
    return x * lax.rsqrt(jnp.mean(x * x, axis=-1, keepdims=True) + EPS) * w


def _bdot(a, b):
    return jnp.dot(a.astype(BF16), b.astype(BF16), preferred_element_type=F32)


def _bdot_nt(a, b):
    return lax.dot_general(a.astype(BF16), b.astype(BF16), (((1,), (1,)), ((), ())),
                           preferred_element_type=F32)


def _bdot_tn(a, b):
    return lax.dot_general(a.astype(BF16), b.astype(BF16), (((0,), (0,)), ((), ())),
                           preferred_element_type=F32)


def _split3(x):
    hi = x.astype(BF16)
    r1 = x - hi.astype(F32)
    mid = r1.astype(BF16)
    lo = (r1 - mid.astype(F32)).astype(BF16)
    return hi, mid, lo


def _cparams(sem, flags=None):
    return pltpu.CompilerParams(dimension_semantics=sem, vmem_limit_bytes=VMEM_LIMIT, flags=flags)


def _mod_kernel(c_ref, w_ref, b_ref, o_ref):
    s = _silu(c_ref[...]).astype(BF16)
    o_ref[0] = jnp.dot(s, w_ref[0].astype(BF16), preferred_element_type=F32) + b_ref[0]


def _modulation(c_all, w_ada, b_ada):
    depth = w_ada.shape[0]
    rows = c_all.shape[0]
    ncol = N_MOD * D_MODEL
    tn = 1536
    return pl.pallas_call(
        _mod_kernel,
        grid=(depth, ncol // tn),
        in_specs=[
            pl.BlockSpec((rows, D_MODEL), lambda l, j: (0, 0)),
            pl.BlockSpec((1, D_MODEL, tn), lambda l, j: (l, 0, j)),
            pl.BlockSpec((1, 1, tn), lambda l, j: (l, 0, j)),
        ],
        out_specs=pl.BlockSpec((1, rows, tn), lambda l, j: (l, 0, j)),
        out_shape=jax.ShapeDtypeStruct((depth, rows, ncol), F32),
        compiler_params=_cparams(("arbitrary", "arbitrary")),
        name="adaln_mod",
    )(c_all, w_ada, b_ada.reshape(depth, 1, ncol))


def _premix_kernel(x_ref, mod_ref, g_ref, w_ref, o_ref):
    x = x_ref[...]
    h = _rms(x, g_ref[...]) * (1.0 + mod_ref[0, 1]) + mod_ref[0, 0]
    o_ref[...] = jnp.dot(h.astype(BF16), w_ref[...], preferred_element_type=F32)


def _premix_conv_kernel(x_ref, mod_ref, g_ref, w_ref, cw_ref, o_ref, tail_ref, prev_scr, *, rep):
    tm = x_ref.shape[0]
    x = x_ref[...]
    h = (_rms(x, g_ref[...]) * (1.0 + mod_ref[0, 1]) + mod_ref[0, 0]).astype(BF16)

    @pl.when(pl.program_id(0) % rep == 0)
    def _():
        prev_scr[...] = jnp.zeros(prev_scr.shape, F32)

    def project(c0, c1):
        return jnp.dot(h, w_ref[:, c0:c1], preferred_element_type=F32)

    def conv_group(xc, grp):
        c0 = grp * WIDTH_A
        win = jnp.concatenate([prev_scr[:, c0:c0 + WIDTH_A], xc], axis=0)
        conv = xc * cw_ref[CONV_WIDTH - 1:CONV_WIDTH, c0:c0 + WIDTH_A]
        for s in range(1, CONV_WIDTH):
            conv = conv + pltpu.roll(win, s, 0)[8:] * cw_ref[CONV_WIDTH - 1 - s:CONV_WIDTH - s, c0:c0 + WIDTH_A]
        xs = _silu(conv)
        for hh in range(N_HEADS_A):
            blk = xs[:, hh * HEAD_DIM_A:(hh + 1) * HEAD_DIM_A]
            if grp < 2:
                scale = lax.rsqrt(jnp.sum(blk * blk, axis=-1, keepdims=True) + EPS)
                blk = blk * (scale * (HEAD_DIM_A ** -0.5) if grp == 0 else scale)
            o_ref[:, OFF_XC + c0 + hh * HEAD_DIM_A:OFF_XC + c0 + (hh + 1) * HEAD_DIM_A] = blk
        prev_scr[:, c0:c0 + WIDTH_A] = xc[tm - 8:, :]
        tail_ref[0, :, c0:c0 + WIDTH_A] = xc[tm - 8:, :]

    xq = project(OFF_XC, OFF_XC + WIDTH_A)
    xk = project(OFF_XC + WIDTH_A, OFF_XC + 2 * WIDTH_A)
    conv_group(xq, 0)
    xv = project(OFF_XC + 2 * WIDTH_A, OFF_XC + CONV_CH)
    conv_group(xk, 1)
    o_ref[:, OFF_Z:] = project(OFF_Z, PROJ_COLS)
    conv_group(xv, 2)


def _premix(x, mod, g, w, layer, tm, rep, conv_w=None):
    n = x.shape[0]
    r = mod.shape[3]
    in_specs = [
        pl.BlockSpec((tm, D_MODEL), lambda i: (i, 0)),
        pl.BlockSpec((None, 1, N_MOD, r, D_MODEL), lambda i: (layer, i // rep, 0, 0, 0)),
        pl.BlockSpec((None, 1, D_MODEL), lambda i: (layer, 0, 0)),
        pl.BlockSpec((None, D_MODEL, PROJ_COLS), lambda i: (layer, 0, 0)),
    ]
    proj_spec = pl.BlockSpec((tm, PROJ_COLS), lambda i: (i, 0))
    proj_shape = jax.ShapeDtypeStruct((n, PROJ_COLS), F32)
    if conv_w is None:
        return pl.pallas_call(
            _premix_kernel, grid=(n // tm,), in_specs=in_specs, out_specs=proj_spec, out_shape=proj_shape,
            compiler_params=_cparams(("arbitrary",)), name="premix_proj",
        )(x, mod, g, w)
    nseq = n // (tm * rep)
    return pl.pallas_call(
        functools.partial(_premix_conv_kernel, rep=rep),
        grid=(n // tm,),
        in_specs=in_specs + [pl.BlockSpec((None, CONV_WIDTH, CONV_CH), lambda i: (layer, 0, 0))],
        out_specs=[proj_spec, pl.BlockSpec((1, 8, CONV_CH), lambda i: (i // rep, 0, 0))],
        out_shape=[proj_shape, jax.ShapeDtypeStruct((nseq, 8, CONV_CH), F32)],
        scratch_shapes=[pltpu.VMEM((8, CONV_CH), F32)],
        compiler_params=_cparams(("arbitrary",)),
        name="premix_conv_proj",
    )(x, mod, g, w, conv_w)


def _ffn_kernel(x_ref, oa_ref, ob_ref, mod_ref, gpm_ref, gpf_ref, gqf_ref, wo_ref, wup_ref, wdn_ref, o_ref):
    x = x_ref[...]
    o_mix = jnp.concatenate([oa_ref[...].astype(BF16), ob_ref[...].astype(BF16)], axis=1)
    mix = jnp.dot(o_mix, wo_ref[...], preferred_element_type=F32)
    x1 = x + mod_ref[0, 2] * _rms(mix, gpm_ref[...])
    h = (_rms(x1, gpf_ref[...]) * (1.0 + mod_ref[0, 4]) + mod_ref[0, 3]).astype(BF16)
    ck = 1024
    acc = jnp.zeros(x.shape, F32)
    for c in range(D_FF // ck):
        u = jnp.dot(h, wup_ref[:, c * ck:(c + 1) * ck], preferred_element_type=F32)
        a = jnp.square(jnp.maximum(u, 0.0)).astype(BF16)
        acc = acc + jnp.dot(a, wdn_ref[c * ck:(c + 1) * ck, :], preferred_element_type=F32)
    o_ref[...] = x1 + mod_ref[0, 5] * _rms(acc, gqf_ref[...])


def _ffn(x, oa, ob, mod, g_post_mix, g_pre_ffn, g_post_ffn, wo, wup, wdn, layer, tm, rep):
    n = x.shape[0]
    r = mod.shape[3]
    const = lambda i: (layer, 0, 0)
    return pl.pallas_call(
        _ffn_kernel,
        grid=(n // tm,),
        in_specs=[
            pl.BlockSpec((tm, D_MODEL), lambda i: (i, 0)),
            pl.BlockSpec((tm, WIDTH_A), lambda i: (i, 0)),
            pl.BlockSpec((tm, WIDTH_B), lambda i: (i, 0)),
            pl.BlockSpec((None, 1, N_MOD, r, D_MODEL), lambda i: (layer, i // rep, 0, 0, 0)),
            pl.BlockSpec((None, 1, D_MODEL), const),
            pl.BlockSpec((None, 1, D_MODEL), const),
            pl.BlockSpec((None, 1, D_MODEL), const),
            pl.BlockSpec((None, D_MODEL, D_MODEL), const, pipeline_mode=pl.Buffered(1)),
            pl.BlockSpec((None, D_MODEL, D_FF), const, pipeline_mode=pl.Buffered(1)),
            pl.BlockSpec((None, D_FF, D_MODEL), const, pipeline_mode=pl.Buffered(1)),
        ],
        out_specs=pl.BlockSpec((tm, D_MODEL), lambda i: (i, 0)),
        out_shape=jax.ShapeDtypeStruct((n, D_MODEL), F32),
        compiler_params=_cparams(("arbitrary",)),
        name="outproj_ffn",
    )(x, oa, ob, mod, g_post_mix, g_pre_ffn, g_post_ffn, wo, wup, wdn)


def _lane_bcast(x, col, width=HEAD_DIM_A):
    return jnp.broadcast_to(x[:, col:col + 1], (x.shape[0], width))


def _gdn_prompt_kernel(xs_ref, z_ref, ba_ref, hp_ref, gnw_ref, o_ref, ssm_ref, s_scr, *, chunk, block, group):
    C = chunk
    G = group
    H = N_HEADS_A
    HD = HEAD_DIM_A
    PW = H * C
    j = pl.program_id(1)
    nj = pl.num_programs(1)

    @pl.when(j == 0)
    def _():
        s_scr[...] = jnp.zeros(s_scr.shape, F32)

    ri = lax.broadcasted_iota(jnp.int32, (C, PW), 0)
    li = lax.broadcasted_iota(jnp.int32, (C, PW), 1)
    lc = li % C
    lower = ri >= lc
    strict = ri > lc
    eye = ri == lc
    rb = lax.broadcasted_iota(jnp.int32, (PW, PW), 0) // C
    cb = lax.broadcasted_iota(jnp.int32, (PW, PW), 1) // C
    bd_mask = rb == cb
    kbd_mask = (lax.broadcasted_iota(jnp.int32, (PW, WIDTH_A), 0) // C
                == lax.broadcasted_iota(jnp.int32, (PW, WIDTH_A), 1) // HD)
    rhs_mask = (lax.broadcasted_iota(jnp.int32, (PW, 2 * WIDTH_A), 0) // C
                == (lax.broadcasted_iota(jnp.int32, (PW, 2 * WIDTH_A), 1) % WIDTH_A) // HD)
    tri = (lax.broadcasted_iota(jnp.int32, (C, C), 0)
           >= lax.broadcasted_iota(jnp.int32, (C, C), 1)).astype(BF16)
    lane128 = lax.broadcasted_iota(jnp.int32, (C, LANE), 1)

    neg_a = -jnp.exp(hp_ref[0:1, :])
    dt_b = hp_ref[1:2, :]
    gnw = gnw_ref[...]

    def bd(y):
        return jnp.where(bd_mask, jnp.concatenate([y] * H, axis=0), 0.0)

    def pmm(x, y):
        return _bdot(x, bd(y))

    def pack_cols(cols):
        per_tile = LANE // C
        tiles = []
        for t in range(H // per_tile):
            acc = cols[t * per_tile]
            for u in range(1, per_tile):
                acc = jnp.where(lane128 < u * C, acc, cols[t * per_tile + u])
            tiles.append(acc)
        return jnp.concatenate(tiles, axis=1)

    def prepare(r0):
        xs = xs_ref[pl.ds(r0, C), :]
        ba = ba_ref[pl.ds(r0, C), :]
        beta_all = jax.nn.sigmoid(ba)
        g_all = neg_a * _softplus(ba + dt_b)
        g_hi, g_mid, g_lo = _split3(g_all)
        g3 = jnp.dot(tri, jnp.concatenate([g_hi, g_mid, g_lo], axis=1), preferred_element_type=F32)
        gc_all = g3[:, 0:LANE] + g3[:, LANE:2 * LANE] + g3[:, 2 * LANE:3 * LANE]

        q, k, v, kb, beta_c, gc_c, eg_c = [], [], [], [], [], [], []
        for h in range(H):
            qh = xs[:, h * HD:(h + 1) * HD]
            kh = xs[:, WIDTH_A + h * HD:WIDTH_A + (h + 1) * HD]
            vh = xs[:, 2 * WIDTH_A + h * HD:2 * WIDTH_A + (h + 1) * HD]
            bh = _lane_bcast(beta_all, h)
            gh = _lane_bcast(gc_all, N_HEADS_A + h)
            q.append(qh)
            k.append(kh)
            v.append(vh)
            kb.append(kh * bh)
            beta_c.append(bh)
            gc_c.append(gh)
            eg_c.append(jnp.exp(gh))

        gcp = pack_cols(gc_c)
        gc_row = jnp.sum(jnp.where(eye, gcp, 0.0), axis=0, keepdims=True)
        decay = jnp.exp(jnp.where(lower, gcp - gc_row, NEG_BIG))

        k_p = jnp.concatenate(k, axis=1)
        kbd = jnp.where(kbd_mask, jnp.concatenate([k_p] * H, axis=0), 0.0)
        lhs = jnp.concatenate([jnp.concatenate(kb, axis=1), jnp.concatenate(q, axis=1)], axis=0)
        kq = _bdot_nt(lhs, kbd)
        lmat = jnp.where(strict, kq[0:C] * decay, 0.0)
        rhs = jnp.concatenate([v[h] * beta_c[h] for h in range(H)]
                              + [kb[h] * eg_c[h] for h in range(H)], axis=1)
        g_last = [gc_all[C - 1:C, N_HEADS_A + h:N_HEADS_A + h + 1] for h in range(H)]
        return dict(lmat=lmat, attn=kq[C:2 * C] * decay, rhs=rhs, g_last=g_last,
                    qg=[q[h] * eg_c[h] for h in range(H)],
                    kg=[k[h] * jnp.exp(g_last[h] - gc_c[h]) for h in range(H)])

    base_blk = min(16, C)
    same_base = (ri // base_blk) == (lc // base_blk)

    def step(i, carry):
        r0s = [pl.multiple_of((i * G + g) * C, C) for g in range(G)]
        ch = [prepare(r0) for r0 in r0s]
        pw = [jnp.where(same_base, c["lmat"], 0.0) for c in ch]
        p = [jnp.where(eye, 1.0, 0.0) - x for x in pw]
        span = 2
        while span < base_blk:
            pw = [pmm(x, x) for x in pw]
            p = [x + pmm(x, y) for x, y in zip(p, pw)]
            span *= 2
        blk = base_blk
        same = same_base
        while blk < C:
            nxt = 2 * blk
            same_n = (ri // nxt) == (lc // nxt)
            off = jnp.logical_and(same_n, jnp.logical_not(same))
            t = [pmm(jnp.where(off, c["lmat"], 0.0), x) for c, x in zip(ch, p)]
            p = [x - pmm(x, y) for x, y in zip(p, t)]
            same = same_n
            blk = nxt

        sol = [_bdot(x, jnp.where(rhs_mask, jnp.concatenate([c["rhs"]] * H, axis=0), 0.0))
               for c, x in zip(ch, p)]
        asol = [_bdot(c["attn"], jnp.where(rhs_mask, jnp.concatenate([x] * H, axis=0), 0.0))
                for c, x in zip(ch, sol)]
        ab = [[_bdot_tn(c["kg"][h], jnp.concatenate(
                   [x[:, WIDTH_A + h * HD:WIDTH_A + (h + 1) * HD], x[:, h * HD:(h + 1) * HD]], axis=1))
               for h in range(H)] for c, x in zip(ch, sol)]
        for g in range(G):
            for h in range(H):
                q_eff = ch[g]["qg"][h] - asol[g][:, WIDTH_A + h * HD:WIDTH_A + (h + 1) * HD]
                sh = s_scr[h]
                o_h = _bdot(q_eff, sh) + asol[g][:, h * HD:(h + 1) * HD]
                s_scr[h] = (sh * jnp.exp(ch[g]["g_last"][h]) - _bdot(ab[g][h][:, 0:HD], sh)
                            + ab[g][h][:, HD:2 * HD])
                z_h = z_ref[pl.ds(r0s[g], C), h * HD:(h + 1) * HD]
                o_ref[pl.ds(r0s[g], C), h * HD:(h + 1) * HD] = (_rms(o_h, gnw) * _silu(z_h)).astype(o_ref.dtype)
        return carry

    lax.fori_loop(0, block // (G * C), step, 0)

    @pl.when(j == nj - 1)
    def _():
        ssm_ref[0] = s_scr[...]


def _gdn_prompt(proj, hp, gnw, layer, batch, seq):
    blk = min(GDN_BLOCK, seq)
    nblk = seq // blk
    chunk = min(GDN_CHUNK, seq)
    kern = functools.partial(_gdn_prompt_kernel, chunk=chunk, block=blk, group=min(GDN_UNROLL, blk // chunk))
    return pl.pallas_call(
        kern,
        grid=(batch, nblk),
        in_specs=[
            pl.BlockSpec((blk, CONV_CH), lambda b, j: (b * nblk + j, OFF_XC // CONV_CH)),
            pl.BlockSpec((blk, WIDTH_A), lambda b, j: (b * nblk + j, OFF_Z // WIDTH_A)),
            pl.BlockSpec((blk, LANE), lambda b, j: (b * nblk + j, OFF_BA // LANE)),
            pl.BlockSpec((None, 8, LANE), lambda b, j: (layer, 0, 0)),
            pl.BlockSpec((None, 1, HEAD_DIM_A), lambda b, j: (layer, 0, 0)),
        ],
        out_specs=[
            pl.BlockSpec((blk, WIDTH_A), lambda b, j: (b * nblk + j, 0)),
            pl.BlockSpec((1, N_HEADS_A, HEAD_DIM_A, HEAD_DIM_A), lambda b, j: (b, 0, 0, 0)),
        ],
        out_shape=[
            jax.ShapeDtypeStruct((batch * seq, WIDTH_A), BF16),
            jax.ShapeDtypeStruct((batch, N_HEADS_A, HEAD_DIM_A, HEAD_DIM_A), F32),
        ],
        scratch_shapes=[pltpu.VMEM((N_HEADS_A, HEAD_DIM_A, HEAD_DIM_A), F32)],
        compiler_params=_cparams(("arbitrary", "arbitrary")),
        name="gdn_prompt",
    )(proj, proj, proj, hp, gnw)


def _sink_column(sink_ref, layer, h, shape, axis, period):
    g = (lax.broadcasted_iota(jnp.int32, shape, axis) // period) % GQA_GROUP
    col = jnp.full(shape, sink_ref[layer, h * GQA_GROUP], F32)
    for gg in range(1, GQA_GROUP):
        col = jnp.where(g == gg, sink_ref[layer, h * GQA_GROUP + gg], col)
    return col


def _swa_prompt_kernel(sink_ref, q_ref, kc_ref, kp_ref, vc_ref, vp_ref, o_ref, *, layer, nsub):
    W = WINDOW
    j = pl.program_id(1)
    rows = GQA_GROUP * W
    q = (q_ref[...] * (HEAD_DIM_B ** -0.5)).astype(BF16)
    kall = jnp.concatenate([kp_ref[...], kc_ref[...]], axis=0).astype(BF16)
    vall = jnp.concatenate([vp_ref[...], vc_ref[...]], axis=0).astype(BF16)
    qi = lax.broadcasted_iota(jnp.int32, (rows, 2 * W), 0) % W
    kj = lax.broadcasted_iota(jnp.int32, (rows, 2 * W), 1)
    band = (kj > qi) & (kj <= qi + W)
    first_key = jnp.where(j > 0, 0, W)
    masks = [band & (kj >= first_key)] + [band] * (nsub - 1)
    probs = [(sb, h) for sb in range(nsub) for h in range(N_KV_HEADS_B)]
    s = []
    for sb, h in probs:
        k_h = kall[sb * W:(sb + 2) * W, h * HEAD_DIM_B:(h + 1) * HEAD_DIM_B]
        qs = jnp.concatenate(
            [q[sb * W:(sb + 1) * W, (h * GQA_GROUP + g) * HEAD_DIM_B:(h * GQA_GROUP + g + 1) * HEAD_DIM_B]
             for g in range(GQA_GROUP)], axis=0)
        s.append(jnp.where(masks[sb], _bdot_nt(qs, k_h), NEG_BIG))
    sk = [_sink_column(sink_ref, layer, h, (rows, 1), 0, W) for h in range(N_KV_HEADS_B)]
    m = [jnp.maximum(jnp.max(x, axis=-1, keepdims=True), sk[h]) for x, (sb, h) in zip(s, probs)]
    p = [jnp.exp(x - y) for x, y in zip(s, m)]
    inv = [1.0 / (jnp.sum(x, axis=-1, keepdims=True) + jnp.exp(sk[h] - y))
           for x, y, (sb, h) in zip(p, m, probs)]
    for x, r, (sb, h) in zip(p, inv, probs):
        v_h = vall[sb * W:(sb + 2) * W, h * HEAD_DIM_B:(h + 1) * HEAD_DIM_B]
        o = jnp.dot(x.astype(BF16), v_h, preferred_element_type=F32) * r
        for g in range(GQA_GROUP):
            c0 = (h * GQA_GROUP + g) * HEAD_DIM_B
            o_ref[sb * W:(sb + 1) * W, c0:c0 + HEAD_DIM_B] = o[g * W:(g + 1) * W, :].astype(o_ref.dtype)


def _swa_prompt(proj, sinks, layer, batch, seq):
    nsub = min(SWA_BLOCKS, seq // WINDOW)
    tq = nsub * WINDOW
    nb = seq // tq
    kcol = OFF_KB // KV_WIDTH_B
    vcol = OFF_VB // KV_WIDTH_B
    prev = lambda col: (lambda b, j: ((b * nb + j) * nsub - jnp.where(j > 0, 1, 0), col))
    return pl.pallas_call(
        functools.partial(_swa_prompt_kernel, layer=layer, nsub=nsub),
        grid=(batch, nb),
        in_specs=[
            pl.BlockSpec(memory_space=pltpu.SMEM),
            pl.BlockSpec((tq, WIDTH_B), lambda b, j: (b * nb + j, OFF_QB // WIDTH_B)),
            pl.BlockSpec((tq, KV_WIDTH_B), lambda b, j: (b * nb + j, kcol)),
            pl.BlockSpec((WINDOW, KV_WIDTH_B), prev(kcol)),
            pl.BlockSpec((tq, KV_WIDTH_B), lambda b, j: (b * nb + j, vcol)),
            pl.BlockSpec((WINDOW, KV_WIDTH_B), prev(vcol)),
        ],
        out_specs=pl.BlockSpec((tq, WIDTH_B), lambda b, j: (b * nb + j, 0)),
        out_shape=jax.ShapeDtypeStruct((batch * seq, WIDTH_B), BF16),
        compiler_params=_cparams(("arbitrary", "arbitrary")),
        name="swa_prompt",
    )(sinks, proj, proj, proj, proj, proj)


def _gdn_sample_kernel(xc_ref, z_ref, ba_ref, cs_ref, s0_ref, cw_ref, hp_ref, gnw_ref,
                       o_ref, s1_ref, lhs_scr, u_scr, kg_scr, res_scr, gl_scr, *, steps, group):
    T = steps
    GB = group
    H = N_HEADS_A
    HD = HEAD_DIM_A
    R = 2 * T

    u_scr[...] = jnp.zeros(u_scr.shape, F32)
    kg_scr[...] = jnp.zeros(kg_scr.shape, F32)

    xp = [cs_ref[i] for i in range(CONV_WIDTH - 1)] + [xc_ref[t] for t in range(T)]
    xs = []
    for t in range(T):
        acc = xp[t] * cw_ref[0:1, :]
        for i in range(1, CONV_WIDTH):
            acc = acc + xp[t + i] * cw_ref[i:i + 1, :]
        xs.append(_silu(acc))

    neg_a = -jnp.exp(hp_ref[0:1, :])
    dt_b = hp_ref[1:2, :]
    beta_all = [jax.nn.sigmoid(ba_ref[t]) for t in range(T)]
    g_all = [neg_a * _softplus(ba_ref[t] + dt_b) for t in range(T)]
    gc_all = [g_all[0]]
    for t in range(1, T):
        gc_all.append(gc_all[t - 1] + g_all[t])

    u_keep, attn_keep = [], []
    for h in range(H):
        q, k, v, kb, beta, gc = [], [], [], [], [], []
        for t in range(T):
            qh = xs[t][:, h * HD:(h + 1) * HD]
            kh = xs[t][:, WIDTH_A + h * HD:WIDTH_A + (h + 1) * HD]
            vh = xs[t][:, 2 * WIDTH_A + h * HD:2 * WIDTH_A + (h + 1) * HD]
            qh = qh * lax.rsqrt(jnp.sum(qh * qh, axis=-1, keepdims=True) + EPS) * (HD ** -0.5)
            kh = kh * lax.rsqrt(jnp.sum(kh * kh, axis=-1, keepdims=True) + EPS)
            bt = beta_all[t][:, h:h + 1]
            q.append(qh)
            k.append(kh)
            v.append(vh)
            kb.append(kh * bt)
            beta.append(bt)
            gc.append(gc_all[t][:, H + h:H + h + 1])
        u, w, attn = [], [], []
        for t in range(T):
            ut = v[t] * beta[t]
            wt = kb[t] * jnp.exp(gc[t])
            arow = []
            for s in range(t + 1):
                dec = jnp.exp(gc[t] - gc[s])
                arow.append(jnp.sum(q[t] * k[s], axis=-1, keepdims=True) * dec)
                if s < t:
                    l_ts = jnp.sum(kb[t] * k[s], axis=-1, keepdims=True) * dec
                    ut = ut - l_ts * u[s]
                    wt = wt - l_ts * w[s]
            u.append(ut)
            w.append(wt)
            attn.append(arow)
        g_last = gc[T - 1]
        for t in range(T):
            row0 = h * GB * R
            lhs_scr[pl.ds(row0 + t, GB, stride=R), :] = w[t]
            lhs_scr[pl.ds(row0 + T + t, GB, stride=R), :] = q[t] * jnp.exp(gc[t])
            u_scr[pl.ds(row0 + t, GB, stride=R), :] = u[t]
            kg_scr[pl.ds(row0 + t, GB, stride=R), :] = k[t] * jnp.exp(g_last - gc[t])
        gl_scr[h] = jnp.broadcast_to(jnp.exp(g_last), (GB, HD))
        u_keep.append(u)
        attn_keep.append(attn)

    def body(b, carry):
        for h in range(H):
            row = pl.multiple_of((h * GB + b) * R, R)
            s = s0_ref[b, h]
            res = _bdot(lhs_scr[pl.ds(row, R), :], s)
            res_scr[pl.ds(row, R), :] = res
            vn = u_scr[pl.ds(row, R), :] - res
            upd = _bdot_tn(kg_scr[pl.ds(row, R), :], vn)
            s1_ref[b, h] = s * gl_scr[h, pl.ds(b, 1), :] + upd
        return carry

    lax.fori_loop(0, GB, body, 0)

    gnw = gnw_ref[...]
    for h in range(H):
        row0 = h * GB * R
        vn = []
        for t in range(T):
            ws_t = res_scr[pl.ds(row0 + t, GB, stride=R), :]
            vn.append(u_keep[h][t] - ws_t)
        for t in range(T):
            o_t = res_scr[pl.ds(row0 + T + t, GB, stride=R), :]
            for s in range(t + 1):
                o_t = o_t + attn_keep[h][t][s] * vn[s]
            z_t = z_ref[t][:, h * HD:(h + 1) * HD]
            o_ref[t, :, h * HD:(h + 1) * HD] = _rms(o_t, gnw) * _silu(z_t)


def _gdn_sample(proj3, cs_tm, s0, conv_w, hp, gnw, layer):
    steps, nseq, _ = proj3.shape
    gb = min(SAMPLE_GROUP, nseq)
    rows = N_HEADS_A * gb * 2 * steps
    kern = functools.partial(_gdn_sample_kernel, steps=steps, group=gb)
    return pl.pallas_call(
        kern,
        grid=(nseq // gb,),
        in_specs=[
            pl.BlockSpec((steps, gb, CONV_CH), lambda i: (0, i, OFF_XC // CONV_CH)),
            pl.BlockSpec((steps, gb, WIDTH_A), lambda i: (0, i, OFF_Z // WIDTH_A)),
            pl.BlockSpec((steps, gb, LANE), lambda i: (0, i, OFF_BA // LANE)),
            pl.BlockSpec((None, CONV_WIDTH - 1, gb, CONV_CH), lambda i: (layer, 0, i, 0)),
            pl.BlockSpec((None, gb, N_HEADS_A, HEAD_DIM_A, HEAD_DIM_A), lambda i: (layer, i, 0, 0, 0)),
            pl.BlockSpec((None, CONV_WIDTH, CONV_CH), lambda i: (layer, 0, 0)),
            pl.BlockSpec((None, 8, LANE), lambda i: (layer, 0, 0)),
            pl.BlockSpec((None, 1, HEAD_DIM_A), lambda i: (layer, 0, 0)),
        ],
        out_specs=[
            pl.BlockSpec((steps, gb, WIDTH_A), lambda i: (0, i, 0)),
            pl.BlockSpec((gb, N_HEADS_A, HEAD_DIM_A, HEAD_DIM_A), lambda i: (i, 0, 0, 0)),
        ],
        out_shape=[
            jax.ShapeDtypeStruct((steps, nseq, WIDTH_A), F32),
            jax.ShapeDtypeStruct(s0.shape[1:], F32),
        ],
        scratch_shapes=[
            pltpu.VMEM((rows, HEAD_DIM_A), F32),
            pltpu.VMEM((rows, HEAD_DIM_A), F32),
            pltpu.VMEM((rows, HEAD_DIM_A), F32),
            pltpu.VMEM((rows, HEAD_DIM_A), F32),
            pltpu.VMEM((N_HEADS_A, gb, HEAD_DIM_A), F32),
        ],
        compiler_params=_cparams(("arbitrary",)),
        name="gdn_sample",
    )(proj3, proj3, proj3, cs_tm, s0, conv_w, hp, gnw)


def _swa_sample_kernel(sink_ref, q_ref, kc_ref, vc_ref, kn_ref, vn_ref, o_ref, *, steps, layer):
    T = steps
    W = WINDOW
    rows = T * GQA_GROUP
    gb = q_ref.shape[0]
    t_of_row = lax.broadcasted_iota(jnp.int32, (1, rows, 1), 1) // GQA_GROUP
    kj = lax.broadcasted_iota(jnp.int32, (1, 1, W), 2)
    cmask = kj > t_of_row
    for h in range(N_KV_HEADS_B):
        lo, hi = h * HEAD_DIM_B, (h + 1) * HEAD_DIM_B
        qh = q_ref[:, h]
        qh_r = qh.astype(BF16)
        kch = kc_ref[:, :, lo:hi].astype(BF16)
        vch = vc_ref[:, :, lo:hi].astype(BF16)
        sc = jnp.einsum("bqd,bkd->bqk", qh_r, kch, preferred_element_type=F32) * (HEAD_DIM_B ** -0.5)
        sc = jnp.where(cmask, sc, NEG_BIG)
        sn = []
        for s in range(T):
            kn_s = kn_ref[:, s:s + 1, lo:hi]
            v = jnp.sum(qh * kn_s, axis=-1, keepdims=True) * (HEAD_DIM_B ** -0.5)
            sn.append(jnp.where(t_of_row >= s, v, NEG_BIG))
        sk = _sink_column(sink_ref, layer, h, (1, rows, 1), 1, 1)
        m = jnp.maximum(jnp.max(sc, axis=-1, keepdims=True), sk)
        for s in range(T):
            m = jnp.maximum(m, sn[s])
        pc = jnp.exp(sc - m)
        pn = [jnp.exp(sn[s] - m) for s in range(T)]
        den = jnp.sum(pc, axis=-1, keepdims=True) + jnp.exp(sk - m)
        for s in range(T):
            den = den + pn[s]
        inv = 1.0 / den
        o = jnp.einsum("bqk,bkd->bqd", (pc * inv).astype(BF16), vch, preferred_element_type=F32)
        for s in range(T):
            o = o + (pn[s] * inv) * vn_ref[:, s:s + 1, lo:hi]
        o_ref[:, h] = o


def _swa_sample(q_r, kc, vc, kn, vn, sinks, layer):
    nseq, _, rows, _ = q_r.shape
    steps = kn.shape[1]
    gb = min(SAMPLE_GROUP, nseq)
    kern = functools.partial(_swa_sample_kernel, steps=steps, layer=layer)
    return pl.pallas_call(
        kern,
        grid=(nseq // gb,),
        in_specs=[
            pl.BlockSpec(memory_space=pltpu.SMEM),
            pl.BlockSpec((gb, N_KV_HEADS_B, rows, HEAD_DIM_B), lambda i: (i, 0, 0, 0)),
            pl.BlockSpec((None, gb, WINDOW, KV_WIDTH_B), lambda i: (layer, i, 0, 0)),
            pl.BlockSpec((None, gb, WINDOW, KV_WIDTH_B), lambda i: (layer, i, 0, 0)),
            pl.BlockSpec((gb, steps, KV_WIDTH_B), lambda i: (i, 0, 0)),
            pl.BlockSpec((gb, steps, KV_WIDTH_B), lambda i: (i, 0, 0)),
        ],
        out_specs=pl.BlockSpec((gb, N_KV_HEADS_B, rows, HEAD_DIM_B), lambda i: (i, 0, 0, 0)),
        out_shape=jax.ShapeDtypeStruct(q_r.shape, F32),
        compiler_params=_cparams(("arbitrary",)),
        name="swa_sample",
    )(sinks, q_r, kc, vc, kn, vn)


def _prompt_layer(x, p, layer, batch, seq):
    tm = min(TOKEN_TILE, seq)
    rep = seq // tm
    proj, tail = _premix(x, p["mod_p"], p["g_pre_mix"], p["w_in"], layer, tm, rep, conv_w=p["conv_w"])
    o_a, ssm = _gdn_prompt(proj, p["hp"], p["gdn_norm_w"], layer, batch, seq)
    o_b = _swa_prompt(proj, p["sinks"], layer, batch, seq)
    x = _ffn(x, o_a, o_b, p["mod_p"], p["g_post_mix"], p["g_pre_ffn"], p["g_post_ffn"],
             p["w_o"], p["w_up"], p["w_down"], layer, tm, rep)
    p3 = proj.reshape(batch, seq, PROJ_COLS)
    conv = tail[:, 8 - (CONV_WIDTH - 1):, :]
    k_new = p3[:, seq - WINDOW:, OFF_KB:OFF_KB + KV_WIDTH_B].reshape(batch, WINDOW, N_KV_HEADS_B, HEAD_DIM_B)
    v_new = p3[:, seq - WINDOW:, OFF_VB:OFF_VB + KV_WIDTH_B].reshape(batch, WINDOW, N_KV_HEADS_B, HEAD_DIM_B)
    return x, conv, ssm, k_new, v_new


def _sample_layer(x, p, layer, nseq, steps):
    proj = _premix(x, p["mod_s"], p["g_pre_mix"], p["w_in"], layer, nseq, steps)
    p3 = proj.reshape(steps, nseq, PROJ_COLS)
    o_a, ssm = _gdn_sample(p3, p["conv_tm"], p["state_ssm"], p["conv_w"], p["hp"], p["gdn_norm_w"], layer)
    q_r = p3[:, :, OFF_QB:OFF_QB + WIDTH_B].reshape(steps, nseq, N_KV_HEADS_B, GQA_GROUP, HEAD_DIM_B)
    q_r = jnp.transpose(q_r, (1, 2, 0, 3, 4)).reshape(nseq, N_KV_HEADS_B, steps * GQA_GROUP, HEAD_DIM_B)
    k_new = jnp.transpose(p3[:, :, OFF_KB:OFF_KB + KV_WIDTH_B], (1, 0, 2))
    v_new = jnp.transpose(p3[:, :, OFF_VB:OFF_VB + KV_WIDTH_B], (1, 0, 2))
    o_b = _swa_sample(q_r, p["cache_k"], p["cache_v"], k_new, v_new, p["sinks"], layer)
    o_b = o_b.reshape(nseq, N_KV_HEADS_B, steps, GQA_GROUP, HEAD_DIM_B)
    o_b = jnp.transpose(o_b, (2, 0, 1, 3, 4)).reshape(steps * nseq, WIDTH_B)
    x = _ffn(x, o_a.reshape(steps * nseq, WIDTH_A), o_b, p["mod_s"], p["g_post_mix"], p["g_pre_ffn"],
             p["g_post_ffn"], p["w_o"], p["w_up"], p["w_down"], layer, nseq, steps)
    xp = jnp.concatenate([p["conv_tm"][layer], p3[:, :, OFF_XC:OFF_XC + CONV_CH]], axis=0)
    conv = jnp.transpose(xp[xp.shape[0] - (CONV_WIDTH - 1):], (1, 0, 2))
    return x, conv, ssm, k_new, v_new


def _pack_w_in(w_in):
    o1 = CONV_CH + WIDTH_A
    o2 = o1 + 2 * N_HEADS_A
    pad = jnp.zeros(w_in.shape[:-1] + (PROJ_COLS - OFF_BA - 2 * N_HEADS_A,), w_in.dtype)
    return jnp.concatenate([w_in[..., :o1], w_in[..., o2:], w_in[..., o1:o2], pad], axis=-1).astype(BF16)


def _head_params(a_log, dt_bias):
    depth = a_log.shape[0]
    hp = jnp.zeros((depth, 8, LANE), F32)
    hp = hp.at[:, 0, N_HEADS_A:2 * N_HEADS_A].set(a_log.astype(F32))
    hp = hp.at[:, 1, N_HEADS_A:2 * N_HEADS_A].set(dt_bias.astype(F32))
    return hp


def kernel(x_prompt, x_sample, state_conv, state_ssm, cache_swa_k, cache_swa_v, c_prompt, c_sample,
           g_pre_mix, g_post_mix, g_pre_ffn, g_post_ffn, w_ada, b_ada, w_in, conv_w, a_log, dt_bias,
           gdn_norm_w, sinks, w_o, w_up, w_down):
    depth = w_ada.shape[0]
    batch, seq, _ = x_prompt.shape
    nseq, steps, _ = x_sample.shape

    c_all = jnp.concatenate([c_prompt, c_sample], axis=0)
    rows = c_all.shape[0]
    rows_p = -(-rows // 8) * 8
    c_all = jnp.pad(c_all, ((0, rows_p - rows), (0, 0)))
    mod_all = _modulation(c_all, w_ada, b_ada).reshape(depth, rows_p, N_MOD, D_MODEL)
    p = dict(
        mod_p=mod_all[:, :batch].reshape(depth, batch, N_MOD, 1, D_MODEL),
        mod_s=jnp.transpose(mod_all[:, batch:batch + nseq], (0, 2, 1, 3)).reshape(depth, 1, N_MOD, nseq, D_MODEL),
        g_pre_mix=g_pre_mix.reshape(depth, 1, D_MODEL), g_post_mix=g_post_mix.reshape(depth, 1, D_MODEL),
        g_pre_ffn=g_pre_ffn.reshape(depth, 1, D_MODEL), g_post_ffn=g_post_ffn.reshape(depth, 1, D_MODEL),
        w_in=_pack_w_in(w_in), conv_w=conv_w, hp=_head_params(a_log, dt_bias),
        gdn_norm_w=gdn_norm_w.reshape(depth, 1, HEAD_DIM_A), sinks=sinks,
        w_o=w_o.astype(BF16), w_up=w_up.astype(BF16), w_down=w_down.astype(BF16),
        conv_tm=jnp.transpose(state_conv, (0, 2, 1, 3)), state_ssm=state_ssm,
        cache_k=cache_swa_k.reshape(depth, nseq, WINDOW, KV_WIDTH_B),
        cache_v=cache_swa_v.reshape(depth, nseq, WINDOW, KV_WIDTH_B),
    )

    yp = x_prompt.reshape(batch * seq, D_MODEL)
    ys = jnp.transpose(x_sample, (1, 0, 2)).reshape(steps * nseq, D_MODEL)
    outs_p, outs_s = [], []
    for l in range(depth):
        yp, c1, s1, k1, v1 = _prompt_layer(yp, p, l, batch, seq)
        ys, c2, s2, k2, v2 = _sample_layer(ys, p, l, nseq, steps)
        outs_p.append((c1, s1, k1, v1))
        outs_s.append((c2, s2, k2, v2))

    y_prompt = yp.reshape(batch, seq, D_MODEL)
    y_sample = jnp.transpose(ys.reshape(steps, nseq, D_MODEL), (1, 0, 2))
    stack = lambda outs, i: jnp.stack([o[i] for o in outs])
    kv_shape = (depth, nseq, steps, N_KV_HEADS_B, HEAD_DIM_B)
    k_s = jnp.concatenate([cache_swa_k, stack(outs_s, 2).reshape(kv_shape)], axis=2)[:, :, -WINDOW:]
    v_s = jnp.concatenate([cache_swa_v, stack(outs_s, 3).reshape(kv_shape)], axis=2)[:, :, -WINDOW:]
    return (y_prompt, y_sample,
            stack(outs_p, 0), stack(outs_p, 1), stack(outs_p, 2), stack(outs_p, 3),
            stack(outs_s, 0), stack(outs_s, 1), k_s, v_s)
```

```python
import functools

import jax
import jax.numpy as jnp
from jax import lax
from jax.experimental import pallas as pl
from jax.experimental.pallas import tpu as pltpu

F32 = jnp.float32
BF16 = jnp.bfloat16

D_MODEL = 1024
N_MOD = 6
N_HEADS_A = 4
HEAD_DIM_A = 128
WIDTH_A = N_HEADS_A * HEAD_DIM_A
CONV_WIDTH = 4
CONV_CH = 3 * WIDTH_A
N_Q_HEADS_B = 8
N_KV_HEADS_B = 2
GQA_GROUP = N_Q_HEADS_B // N_KV_HEADS_B
HEAD_DIM_B = 64
WIDTH_B = N_Q_HEADS_B * HEAD_DIM_B
KV_WIDTH_B = N_KV_HEADS_B * HEAD_DIM_B
WINDOW = 128
D_FF = 4 * D_MODEL
EPS = 1e-6
NEG_BIG = -1e30

OFF_XC = 0
OFF_Z = OFF_XC + CONV_CH
OFF_QB = OFF_Z + WIDTH_A
OFF_KB = OFF_QB + WIDTH_B
OFF_VB = OFF_KB + KV_WIDTH_B
OFF_BA = OFF_VB + KV_WIDTH_B
LANE = 128
PROJ_COLS = OFF_BA + LANE

GDN_CHUNK = 64
GDN_BLOCK = 512
GDN_UNROLL = 8
SWA_BLOCKS = 2
TOKEN_TILE = 512
SAMPLE_GROUP = 16
GDN_SAMPLE_GROUP = 32
SAMPLE_INNER = 4
VMEM_LIMIT = 56 * 1024 * 1024


def _silu(x):
    return x * jax.nn.sigmoid(x)


def _softplus(x):
    return jnp.maximum(x, 0.0) + jnp.log(1.0 + jnp.exp(-jnp.abs(x)))


def _rms(x, w):
    return x * lax.rsqrt(jnp.mean(x * x, axis=-1, keepdims=True) + EPS) * w


def _bdot(a, b):
    return jnp.dot(a.astype(BF16), b.astype(BF16), preferred_element_type=F32)


def _bdot_nt(a, b):
    return lax.dot_general(a.astype(BF16), b.astype(BF16), (((1,), (1,)), ((), ())),
                           preferred_element_type=F32)


def _bdot_tn(a, b):
    return lax.dot_general(a.astype(BF16), b.astype(BF16), (((0,), (0,)), ((), ())),
                           preferred_element_type=F32)


def _split3(x):
    hi = x.astype(BF16)
    r1 = x - hi.astype(F32)
    mid = r1.astype(BF16)
    lo = (r1 - mid.astype(F32)).astype(BF16)
    return hi, mid, lo


def _cparams(sem, flags=None):
    return pltpu.CompilerParams(dimension_semantics=sem, vmem_limit_bytes=VMEM_LIMIT, flags=flags)


def _mod_kernel(c_ref, w_ref, b_ref, o_ref):
    s = _silu(c_ref[...]).astype(BF16)
    o_ref[...] = jnp.dot(s, w_ref[...].astype(BF16), preferred_element_type=F32) + b_ref[...]


def _modulation(c_all, w_ada, b_ada):
    depth = w_ada.shape[0]
    rows = c_all.shape[0]
    return pl.pallas_call(
        _mod_kernel,
        grid=(depth, N_MOD),
        in_specs=[
            pl.BlockSpec((rows, D_MODEL), lambda l, j: (0, 0)),
            pl.BlockSpec((None, D_MODEL, D_MODEL), lambda l, j: (l, 0, j)),
            pl.BlockSpec((None, 1, D_MODEL), lambda l, j: (l, 0, j)),
        ],
        out_specs=pl.BlockSpec((None, None, rows, D_MODEL), lambda l, j: (l, j, 0, 0)),
        out_shape=jax.ShapeDtypeStruct((depth, N_MOD, rows, D_MODEL), F32),
        compiler_params=_cparams(("arbitrary", "arbitrary")),
        name="adaln_mod",
    )(c_all, w_ada, b_ada.reshape(depth, 1, N_MOD * D_MODEL))


def _mod_spec(mod, layer, tm, rep):
    if mod.ndim == 5:
        return pl.BlockSpec((None, None, N_MOD, 1, D_MODEL), lambda i: (layer, i // rep, 0, 0, 0))
    return pl.BlockSpec((None, N_MOD, tm, D_MODEL), lambda i: (layer, 0, 0, 0))


def _premix_kernel(x_ref, mod_ref, g_ref, w_ref, o_ref):
    x = x_ref[...]
    h = _rms(x, g_ref[...]) * (1.0 + mod_ref[1]) + mod_ref[0]
    o_ref[...] = _bdot_nt(h, w_ref[...])


def _premix_conv_kernel(x_ref, mod_ref, g_ref, w_ref, cw_ref, o_ref, tail_ref, prev_scr, *, rep):
    tm = x_ref.shape[0]
    x = x_ref[...]
    h = (_rms(x, g_ref[...]) * (1.0 + mod_ref[1]) + mod_ref[0]).astype(BF16)

    @pl.when(pl.program_id(0) % rep == 0)
    def _():
        prev_scr[...] = jnp.zeros(prev_scr.shape, F32)

    def project(c0, c1):
        return _bdot_nt(h, w_ref[c0:c1, :])

    def conv_group(xc, grp):
        c0 = grp * WIDTH_A
        win = jnp.concatenate([prev_scr[:, c0:c0 + WIDTH_A], xc], axis=0)
        conv = xc * cw_ref[CONV_WIDTH - 1:CONV_WIDTH, c0:c0 + WIDTH_A]
        for s in range(1, CONV_WIDTH):
            conv = conv + pltpu.roll(win, s, 0)[8:] * cw_ref[CONV_WIDTH - 1 - s:CONV_WIDTH - s, c0:c0 + WIDTH_A]
        xs = _silu(conv)
        for hh in range(N_HEADS_A):
            blk = xs[:, hh * HEAD_DIM_A:(hh + 1) * HEAD_DIM_A]
            if grp < 2:
                scale = lax.rsqrt(jnp.sum(blk * blk, axis=-1, keepdims=True) + EPS)
                blk = blk * (scale * (HEAD_DIM_A ** -0.5) if grp == 0 else scale)
            o_ref[:, OFF_XC + c0 + hh * HEAD_DIM_A:OFF_XC + c0 + (hh + 1) * HEAD_DIM_A] = blk
        prev_scr[:, c0:c0 + WIDTH_A] = xc[tm - 8:, :]
        tail_ref[0, :, c0:c0 + WIDTH_A] = xc[tm - 8:, :]

    xq = project(OFF_XC, OFF_XC + WIDTH_A)
    xk = project(OFF_XC + WIDTH_A, OFF_XC + 2 * WIDTH_A)
    conv_group(xq, 0)
    xv = project(OFF_XC + 2 * WIDTH_A, OFF_XC + CONV_CH)
    conv_group(xk, 1)
    o_ref[:, OFF_Z:] = project(OFF_Z, PROJ_COLS)
    conv_group(xv, 2)


def _premix(x, mod, g, w, layer, tm, rep, conv_w=None):
    n = x.shape[0]
    in_specs = [
        pl.BlockSpec((tm, D_MODEL), lambda i: (i, 0)),
        _mod_spec(mod, layer, tm, rep),
        pl.BlockSpec((None, 1, D_MODEL), lambda i: (layer, 0, 0)),
        pl.BlockSpec((None, PROJ_COLS, D_MODEL), lambda i: (layer, 0, 0)),
    ]
    proj_spec = pl.BlockSpec((tm, PROJ_COLS), lambda i: (i, 0))
    proj_shape = jax.ShapeDtypeStruct((n, PROJ_COLS), F32)
    if conv_w is None:
        return pl.pallas_call(
            _premix_kernel, grid=(n // tm,), in_specs=in_specs, out_specs=proj_spec, out_shape=proj_shape,
            compiler_params=_cparams(("arbitrary",)), name="premix_proj",
        )(x, mod, g, w)
    nseq = n // (tm * rep)
    return pl.pallas_call(
        functools.partial(_premix_conv_kernel, rep=rep),
        grid=(n // tm,),
        in_specs=in_specs + [pl.BlockSpec((None, CONV_WIDTH, CONV_CH), lambda i: (layer, 0, 0))],
        out_specs=[proj_spec, pl.BlockSpec((1, 8, CONV_CH), lambda i: (i // rep, 0, 0))],
        out_shape=[proj_shape, jax.ShapeDtypeStruct((nseq, 8, CONV_CH), F32)],
        scratch_shapes=[pltpu.VMEM((8, CONV_CH), F32)],
        compiler_params=_cparams(("arbitrary",)),
        name="premix_conv_proj",
    )(x, mod, g, w, conv_w)


def _ffn_kernel(x_ref, oa_ref, ob_ref, mod_ref, gpm_ref, gpf_ref, gqf_ref, wo_ref, wup_ref, wdn_ref, o_ref):
    x = x_ref[...]
    o_mix = jnp.concatenate([oa_ref[...].astype(BF16), ob_ref[...].astype(BF16)], axis=1)
    mix = jnp.dot(o_mix, wo_ref[...], preferred_element_type=F32)
    x1 = x + mod_ref[2] * _rms(mix, gpm_ref[...])
    h = (_rms(x1, gpf_ref[...]) * (1.0 + mod_ref[4]) + mod_ref[3]).astype(BF16)
    ck = 1024
    acc = jnp.zeros(x.shape, F32)
    for c in range(D_FF // ck):
        u = jnp.dot(h, wup_ref[:, c * ck:(c + 1) * ck], preferred_element_type=F32)
        a = jnp.square(jnp.maximum(u, 0.0)).astype(BF16)
        acc = acc + jnp.dot(a, wdn_ref[c * ck:(c + 1) * ck, :], preferred_element_type=F32)
    o_ref[...] = x1 + mod_ref[5] * _rms(acc, gqf_ref[...])


def _ffn(x, oa, ob, mod, g_post_mix, g_pre_ffn, g_post_ffn, wo, wup, wdn, layer, tm, rep):
    n = x.shape[0]
    const = lambda i: (layer, 0, 0)
    return pl.pallas_call(
        _ffn_kernel,
        grid=(n // tm,),
        in_specs=[
            pl.BlockSpec((tm, D_MODEL), lambda i: (i, 0)),
            pl.BlockSpec((tm, WIDTH_A), lambda i: (i, 0)),
            pl.BlockSpec((tm, WIDTH_B), lambda i: (i, 0)),
            _mod_spec(mod, layer, tm, rep),
            pl.BlockSpec((None, 1, D_MODEL), const),
            pl.BlockSpec((None, 1, D_MODEL), const),
            pl.BlockSpec((None, 1, D_MODEL), const),
            pl.BlockSpec((None, D_MODEL, D_MODEL), const, pipeline_mode=pl.Buffered(1)),
            pl.BlockSpec((None, D_MODEL, D_FF), const, pipeline_mode=pl.Buffered(1)),
            pl.BlockSpec((None, D_FF, D_MODEL), const, pipeline_mode=pl.Buffered(1)),
        ],
        out_specs=pl.BlockSpec((tm, D_MODEL), lambda i: (i, 0)),
        out_shape=jax.ShapeDtypeStruct((n, D_MODEL), F32),
        compiler_params=_cparams(("arbitrary",)),
        name="outproj_ffn",
    )(x, oa, ob, mod, g_post_mix, g_pre_ffn, g_post_ffn, wo, wup, wdn)


def _lane_bcast(x, col, width=HEAD_DIM_A):
    return jnp.broadcast_to(x[:, col:col + 1], (x.shape[0], width))


def _gdn_prompt_kernel(xs_ref, z_ref, ba_ref, hp_ref, gnw_ref, o_ref, ssm_ref, s_scr, *, chunk, block, group):
    C = chunk
    G = group
    H = N_HEADS_A
    HD = HEAD_DIM_A
    PW = H * C
    j = pl.program_id(1)
    nj = pl.num_programs(1)

    @pl.when(j == 0)
    def _():
        s_scr[...] = jnp.zeros(s_scr.shape, F32)

    ri = lax.broadcasted_iota(jnp.int32, (C, PW), 0)
    li = lax.broadcasted_iota(jnp.int32, (C, PW), 1)
    lc = li % C
    lower = ri >= lc
    strict = ri > lc
    eye = ri == lc
    rb = lax.broadcasted_iota(jnp.int32, (PW, PW), 0) // C
    cb = lax.broadcasted_iota(jnp.int32, (PW, PW), 1) // C
    bd_mask = rb == cb
    kbd_mask = (lax.broadcasted_iota(jnp.int32, (PW, WIDTH_A), 0) // C
                == lax.broadcasted_iota(jnp.int32, (PW, WIDTH_A), 1) // HD)
    rhs_mask = (lax.broadcasted_iota(jnp.int32, (PW, 2 * WIDTH_A), 0) // C
                == (lax.broadcasted_iota(jnp.int32, (PW, 2 * WIDTH_A), 1) % WIDTH_A) // HD)
    tri = (lax.broadcasted_iota(jnp.int32, (C, C), 0)
           >= lax.broadcasted_iota(jnp.int32, (C, C), 1)).astype(BF16)
    lane128 = lax.broadcasted_iota(jnp.int32, (C, LANE), 1)

    neg_a = -jnp.exp(hp_ref[0:1, :])
    dt_b = hp_ref[1:2, :]
    gnw = gnw_ref[...]

    def bd(y):
        return jnp.where(bd_mask, jnp.concatenate([y] * H, axis=0), 0.0)

    def pmm(x, y):
        return _bdot(x, bd(y))

    def pack_cols(cols):
        per_tile = LANE // C
        tiles = []
        for t in range(H // per_tile):
            acc = cols[t * per_tile]
            for u in range(1, per_tile):
                acc = jnp.where(lane128 < u * C, acc, cols[t * per_tile + u])
            tiles.append(acc)
        return jnp.concatenate(tiles, axis=1)

    def prepare(r0):
        xs = xs_ref[pl.ds(r0, C), :]
        ba = ba_ref[pl.ds(r0, C), :]
        beta_all = jax.nn.sigmoid(ba)
        g_all = neg_a * _softplus(ba + dt_b)
        g_hi, g_mid, g_lo = _split3(g_all)
        g3 = jnp.dot(tri, jnp.concatenate([g_hi, g_mid, g_lo], axis=1), preferred_element_type=F32)
        gc_all = g3[:, 0:LANE] + g3[:, LANE:2 * LANE] + g3[:, 2 * LANE:3 * LANE]

        q, k, v, kb, beta_c, gc_c, eg_c = [], [], [], [], [], [], []
        for h in range(H):
            qh = xs[:, h * HD:(h + 1) * HD]
            kh = xs[:, WIDTH_A + h * HD:WIDTH_A + (h + 1) * HD]
            vh = xs[:, 2 * WIDTH_A + h * HD:2 * WIDTH_A + (h + 1) * HD]
            bh = _lane_bcast(beta_all, h)
            gh = _lane_bcast(gc_all, N_HEADS_A + h)
            q.append(qh)
            k.append(kh)
            v.append(vh)
            kb.append(kh * bh)
            beta_c.append(bh)
            gc_c.append(gh)
            eg_c.append(jnp.exp(gh))

        gcp = pack_cols(gc_c)
        gc_row = jnp.sum(jnp.where(eye, gcp, 0.0), axis=0, keepdims=True)
        decay = jnp.exp(jnp.where(lower, gcp - gc_row, NEG_BIG))

        k_p = jnp.concatenate(k, axis=1)
        kbd = jnp.where(kbd_mask, jnp.concatenate([k_p] * H, axis=0), 0.0)
        lhs = jnp.concatenate([jnp.concatenate(kb, axis=1), jnp.concatenate(q, axis=1)], axis=0)
        kq = _bdot_nt(lhs, kbd)
        lmat = jnp.where(strict, kq[0:C] * decay, 0.0)
        rhs = jnp.concatenate([v[h] * beta_c[h] for h in range(H)]
                              + [kb[h] * eg_c[h] for h in range(H)], axis=1)
        g_last = [gc_all[C - 1:C, N_HEADS_A + h:N_HEADS_A + h + 1] for h in range(H)]
        return dict(lmat=lmat, attn=kq[C:2 * C] * decay, rhs=rhs, g_last=g_last,
                    qg=[q[h] * eg_c[h] for h in range(H)],
                    kg=[k[h] * jnp.exp(g_last[h] - gc_c[h]) for h in range(H)])

    base_blk = min(16, C)
    same_base = (ri // base_blk) == (lc // base_blk)

    def step(i, carry):
        r0s = [pl.multiple_of((i * G + g) * C, C) for g in range(G)]
        ch = [prepare(r0) for r0 in r0s]
        pw = [jnp.where(same_base, c["lmat"], 0.0) for c in ch]
        p = [jnp.where(eye, 1.0, 0.0) - x for x in pw]
        span = 2
        while span < base_blk:
            pw = [pmm(x, x) for x in pw]
            p = [x + pmm(x, y) for x, y in zip(p, pw)]
            span *= 2
        blk = base_blk
        same = same_base
        while blk < C:
            nxt = 2 * blk
            same_n = (ri // nxt) == (lc // nxt)
            off = jnp.logical_and(same_n, jnp.logical_not(same))
            t = [pmm(jnp.where(off, c["lmat"], 0.0), x) for c, x in zip(ch, p)]
            p = [x - pmm(x, y) for x, y in zip(p, t)]
            same = same_n
            blk = nxt

        sol = [_bdot(x, jnp.where(rhs_mask, jnp.concatenate([c["rhs"]] * H, axis=0), 0.0))
               for c, x in zip(ch, p)]
        asol = [_bdot(c["attn"], jnp.where(rhs_mask, jnp.concatenate([x] * H, axis=0), 0.0))
                for c, x in zip(ch, sol)]
        ab = [[_bdot_tn(c["kg"][h], jnp.concatenate(
                   [x[:, WIDTH_A + h * HD:WIDTH_A + (h + 1) * HD], x[:, h * HD:(h + 1) * HD]], axis=1))
               for h in range(H)] for c, x in zip(ch, sol)]
        for g in range(G):
            for h in range(H):
                q_eff = ch[g]["qg"][h] - asol[g][:, WIDTH_A + h * HD:WIDTH_A + (h + 1) * HD]
                sh = s_scr[h]
                o_h = _bdot(q_eff, sh) + asol[g][:, h * HD:(h + 1) * HD]
                s_scr[h] = (sh * jnp.exp(ch[g]["g_last"][h]) - _bdot(ab[g][h][:, 0:HD], sh)
                            + ab[g][h][:, HD:2 * HD])
                z_h = z_ref[pl.ds(r0s[g], C), h * HD:(h + 1) * HD]
                o_ref[pl.ds(r0s[g], C), h * HD:(h + 1) * HD] = (_rms(o_h, gnw) * _silu(z_h)).astype(o_ref.dtype)
        return carry

    lax.fori_loop(0, block // (G * C), step, 0)

    @pl.when(j == nj - 1)
    def _():
        ssm_ref[0] = s_scr[...]


def _gdn_prompt(proj, hp, gnw, layer, batch, seq):
    blk = min(GDN_BLOCK, seq)
    nblk = seq // blk
    chunk = min(GDN_CHUNK, seq)
    kern = functools.partial(_gdn_prompt_kernel, chunk=chunk, block=blk, group=min(GDN_UNROLL, blk // chunk))
    return pl.pallas_call(
        kern,
        grid=(batch, nblk),
        in_specs=[
            pl.BlockSpec((blk, CONV_CH), lambda b, j: (b * nblk + j, OFF_XC // CONV_CH)),
            pl.BlockSpec((blk, WIDTH_A), lambda b, j: (b * nblk + j, OFF_Z // WIDTH_A)),
            pl.BlockSpec((blk, LANE), lambda b, j: (b * nblk + j, OFF_BA // LANE)),
            pl.BlockSpec((None, 8, LANE), lambda b, j: (layer, 0, 0)),
            pl.BlockSpec((None, 1, HEAD_DIM_A), lambda b, j: (layer, 0, 0)),
        ],
        out_specs=[
            pl.BlockSpec((blk, WIDTH_A), lambda b, j: (b * nblk + j, 0)),
            pl.BlockSpec((1, N_HEADS_A, HEAD_DIM_A, HEAD_DIM_A), lambda b, j: (b, 0, 0, 0)),
        ],
        out_shape=[
            jax.ShapeDtypeStruct((batch * seq, WIDTH_A), BF16),
            jax.ShapeDtypeStruct((batch, N_HEADS_A, HEAD_DIM_A, HEAD_DIM_A), F32),
        ],
        scratch_shapes=[pltpu.VMEM((N_HEADS_A, HEAD_DIM_A, HEAD_DIM_A), F32)],
        compiler_params=_cparams(("arbitrary", "arbitrary")),
        name="gdn_prompt",
    )(proj, proj, proj, hp, gnw)


def _sink_column(sink_ref, layer, h, shape, axis, period):
    g = (lax.broadcasted_iota(jnp.int32, shape, axis) // period) % GQA_GROUP
    col = jnp.full(shape, sink_ref[layer, h * GQA_GROUP], F32)
    for gg in range(1, GQA_GROUP):
        col = jnp.where(g == gg, sink_ref[layer, h * GQA_GROUP + gg], col)
    return col


def _swa_prompt_kernel(sink_ref, q_ref, kc_ref, kp_ref, vc_ref, vp_ref, o_ref, *, layer, nsub):
    W = WINDOW
    j = pl.program_id(1)
    rows = GQA_GROUP * W
    q = (q_ref[...] * (HEAD_DIM_B ** -0.5)).astype(BF16)
    kall = jnp.concatenate([kp_ref[...], kc_ref[...]], axis=0).astype(BF16)
    vall = jnp.concatenate([vp_ref[...], vc_ref[...]], axis=0).astype(BF16)
    qi = lax.broadcasted_iota(jnp.int32, (rows, 2 * W), 0) % W
    kj = lax.broadcasted_iota(jnp.int32, (rows, 2 * W), 1)
    band = (kj > qi) & (kj <= qi + W)
    first_key = jnp.where(j > 0, 0, W)
    masks = [band & (kj >= first_key)] + [band] * (nsub - 1)
    probs = [(sb, h) for sb in range(nsub) for h in range(N_KV_HEADS_B)]
    s = []
    for sb, h in probs:
        k_h = kall[sb * W:(sb + 2) * W, h * HEAD_DIM_B:(h + 1) * HEAD_DIM_B]
        qs = jnp.concatenate(
            [q[sb * W:(sb + 1) * W, (h * GQA_GROUP + g) * HEAD_DIM_B:(h * GQA_GROUP + g + 1) * HEAD_DIM_B]
             for g in range(GQA_GROUP)], axis=0)
        s.append(jnp.where(masks[sb], _bdot_nt(qs, k_h), NEG_BIG))
    sk = [_sink_column(sink_ref, layer, h, (rows, 1), 0, W) for h in range(N_KV_HEADS_B)]
    m = [jnp.maximum(jnp.max(x, axis=-1, keepdims=True), sk[h]) for x, (sb, h) in zip(s, probs)]
    p = [jnp.exp(x - y) for x, y in zip(s, m)]
    inv = [1.0 / (jnp.sum(x, axis=-1, keepdims=True) + jnp.exp(sk[h] - y))
           for x, y, (sb, h) in zip(p, m, probs)]
    for x, r, (sb, h) in zip(p, inv, probs):
        v_h = vall[sb * W:(sb + 2) * W, h * HEAD_DIM_B:(h + 1) * HEAD_DIM_B]
        o = jnp.dot(x.astype(BF16), v_h, preferred_element_type=F32) * r
        for g in range(GQA_GROUP):
            c0 = (h * GQA_GROUP + g) * HEAD_DIM_B
            o_ref[sb * W:(sb + 1) * W, c0:c0 + HEAD_DIM_B] = o[g * W:(g + 1) * W, :].astype(o_ref.dtype)


def _swa_prompt(proj, sinks, layer, batch, seq):
    nsub = min(SWA_BLOCKS, seq // WINDOW)
    tq = nsub * WINDOW
    nb = seq // tq
    kcol = OFF_KB // KV_WIDTH_B
    vcol = OFF_VB // KV_WIDTH_B
    prev = lambda col: (lambda b, j: ((b * nb + j) * nsub - jnp.where(j > 0, 1, 0), col))
    return pl.pallas_call(
        functools.partial(_swa_prompt_kernel, layer=layer, nsub=nsub),
        grid=(batch, nb),
        in_specs=[
            pl.BlockSpec(memory_space=pltpu.SMEM),
            pl.BlockSpec((tq, WIDTH_B), lambda b, j: (b * nb + j, OFF_QB // WIDTH_B)),
            pl.BlockSpec((tq, KV_WIDTH_B), lambda b, j: (b * nb + j, kcol)),
            pl.BlockSpec((WINDOW, KV_WIDTH_B), prev(kcol)),
            pl.BlockSpec((tq, KV_WIDTH_B), lambda b, j: (b * nb + j, vcol)),
            pl.BlockSpec((WINDOW, KV_WIDTH_B), prev(vcol)),
        ],
        out_specs=pl.BlockSpec((tq, WIDTH_B), lambda b, j: (b * nb + j, 0)),
        out_shape=jax.ShapeDtypeStruct((batch * seq, WIDTH_B), BF16),
        compiler_params=_cparams(("arbitrary", "arbitrary")),
        name="swa_prompt",
    )(sinks, proj, proj, proj, proj, proj)


def _gdn_sample_kernel(xc_ref, z_ref, ba_ref, cs_ref, s0_ref, cw_ref, hp_ref, gnw_ref, acc_ref,
                       o_ref, s1_ref, lhs_scr, u_scr, kg_scr, res_scr, gl_scr, *, steps, group):
    del acc_ref
    T = steps
    GB = group
    H = N_HEADS_A
    HD = HEAD_DIM_A
    R = 2 * T

    u_scr[...] = jnp.zeros(u_scr.shape, F32)
    kg_scr[...] = jnp.zeros(kg_scr.shape, F32)

    xp = [cs_ref[i] for i in range(CONV_WIDTH - 1)] + [xc_ref[t] for t in range(T)]
    xs = []
    for t in range(T):
        acc = xp[t] * cw_ref[0:1, :]
        for i in range(1, CONV_WIDTH):
            acc = acc + xp[t + i] * cw_ref[i:i + 1, :]
        xs.append(_silu(acc))

    neg_a = -jnp.exp(hp_ref[0:1, :])
    dt_b = hp_ref[1:2, :]
    beta_all = [jax.nn.sigmoid(ba_ref[t]) for t in range(T)]
    g_all = [neg_a * _softplus(ba_ref[t] + dt_b) for t in range(T)]
    gc_all = [g_all[0]]
    for t in range(1, T):
        gc_all.append(gc_all[t - 1] + g_all[t])

    u_keep, attn_keep = [], []
    for h in range(H):
        q, k, v, kb, beta, gc = [], [], [], [], [], []
        for t in range(T):
            qh = xs[t][:, h * HD:(h + 1) * HD]
            kh = xs[t][:, WIDTH_A + h * HD:WIDTH_A + (h + 1) * HD]
            vh = xs[t][:, 2 * WIDTH_A + h * HD:2 * WIDTH_A + (h + 1) * HD]
            qh = qh * lax.rsqrt(jnp.sum(qh * qh, axis=-1, keepdims=True) + EPS) * (HD ** -0.5)
            kh = kh * lax.rsqrt(jnp.sum(kh * kh, axis=-1, keepdims=True) + EPS)
            bt = beta_all[t][:, h:h + 1]
            q.append(qh)
            k.append(kh)
            v.append(vh)
            kb.append(kh * bt)
            beta.append(bt)
            gc.append(gc_all[t][:, H + h:H + h + 1])
        u, w, attn = [], [], []
        for t in range(T):
            ut = v[t] * beta[t]
            wt = kb[t] * jnp.exp(gc[t])
            arow = []
            for s in range(t + 1):
                dec = jnp.exp(gc[t] - gc[s])
                arow.append(jnp.sum(q[t] * k[s], axis=-1, keepdims=True) * dec)
                if s < t:
                    l_ts = jnp.sum(kb[t] * k[s], axis=-1, keepdims=True) * dec
                    ut = ut - l_ts * u[s]
                    wt = wt - l_ts * w[s]
            u.append(ut)
            w.append(wt)
            attn.append(arow)
        g_last = gc[T - 1]
        for t in range(T):
            row0 = h * GB * R
            lhs_scr[pl.ds(row0 + t, GB, stride=R), :] = w[t]
            lhs_scr[pl.ds(row0 + T + t, GB, stride=R), :] = q[t] * jnp.exp(gc[t])
            u_scr[pl.ds(row0 + t, GB, stride=R), :] = u[t]
            kg_scr[pl.ds(row0 + t, GB, stride=R), :] = k[t] * jnp.exp(g_last - gc[t])
        gl_scr[h] = jnp.broadcast_to(jnp.exp(g_last), (GB, HD))
        u_keep.append(u)
        attn_keep.append(attn)

    inner = min(SAMPLE_INNER, GB)

    def body(i, carry):
        pairs = [(i * inner + g, h) for g in range(inner) for h in range(H)]
        rows = [pl.multiple_of((h * GB + b) * R, R) for b, h in pairs]
        res = [_bdot(lhs_scr[pl.ds(row, R), :], s0_ref[b, h]) for (b, h), row in zip(pairs, rows)]
        vn = []
        for row, r in zip(rows, res):
            res_scr[pl.ds(row, R), :] = r
            vn.append(u_scr[pl.ds(row, R), :] - r)
        upd = [_bdot_tn(kg_scr[pl.ds(row, R), :], x) for row, x in zip(rows, vn)]
        for (b, h), x in zip(pairs, upd):
            s1_ref[b, h] = s0_ref[b, h] * gl_scr[h, pl.ds(b, 1), :] + x
        return carry

    lax.fori_loop(0, GB // inner, body, 0)

    gnw = gnw_ref[...]
    for h in range(H):
        row0 = h * GB * R
        vn = []
        for t in range(T):
            ws_t = res_scr[pl.ds(row0 + t, GB, stride=R), :]
            vn.append(u_keep[h][t] - ws_t)
        for t in range(T):
            o_t = res_scr[pl.ds(row0 + T + t, GB, stride=R), :]
            for s in range(t + 1):
                o_t = o_t + attn_keep[h][t][s] * vn[s]
            z_t = z_ref[t][:, h * HD:(h + 1) * HD]
            o_ref[t, :, h * HD:(h + 1) * HD] = _rms(o_t, gnw) * _silu(z_t)


def _gdn_sample(proj3, cs_tm, s0, conv_w, hp, gnw, layer, acc):
    steps, nseq, _ = proj3.shape
    gb = min(GDN_SAMPLE_GROUP, nseq)
    rows = N_HEADS_A * gb * 2 * steps
    kern = functools.partial(_gdn_sample_kernel, steps=steps, group=gb)
    return pl.pallas_call(
        kern,
        grid=(nseq // gb,),
        input_output_aliases={8: 1},
        in_specs=[
            pl.BlockSpec((steps, gb, CONV_CH), lambda i: (0, i, OFF_XC // CONV_CH)),
            pl.BlockSpec((steps, gb, WIDTH_A), lambda i: (0, i, OFF_Z // WIDTH_A)),
            pl.BlockSpec((steps, gb, LANE), lambda i: (0, i, OFF_BA // LANE)),
            pl.BlockSpec((None, CONV_WIDTH - 1, gb, CONV_CH), lambda i: (layer, 0, i, 0)),
            pl.BlockSpec((None, gb, N_HEADS_A, HEAD_DIM_A, HEAD_DIM_A), lambda i: (layer, i, 0, 0, 0)),
            pl.BlockSpec((None, CONV_WIDTH, CONV_CH), lambda i: (layer, 0, 0)),
            pl.BlockSpec((None, 8, LANE), lambda i: (layer, 0, 0)),
            pl.BlockSpec((None, 1, HEAD_DIM_A), lambda i: (layer, 0, 0)),
            pl.BlockSpec(memory_space=pl.ANY),
        ],
        out_specs=[
            pl.BlockSpec((steps, gb, WIDTH_A), lambda i: (0, i, 0)),
            pl.BlockSpec((None, gb, N_HEADS_A, HEAD_DIM_A, HEAD_DIM_A), lambda i: (layer, i, 0, 0, 0)),
        ],
        out_shape=[
            jax.ShapeDtypeStruct((steps, nseq, WIDTH_A), F32),
            jax.ShapeDtypeStruct(s0.shape, F32),
        ],
        scratch_shapes=[
            pltpu.VMEM((rows, HEAD_DIM_A), F32),
            pltpu.VMEM((rows, HEAD_DIM_A), F32),
            pltpu.VMEM((rows, HEAD_DIM_A), F32),
            pltpu.VMEM((rows, HEAD_DIM_A), F32),
            pltpu.VMEM((N_HEADS_A, gb, HEAD_DIM_A), F32),
        ],
        compiler_params=_cparams(("arbitrary",)),
        name="gdn_sample",
    )(proj3, proj3, proj3, cs_tm, s0, conv_w, hp, gnw, acc)


def _swa_sample_kernel(sink_ref, q_ref, kc_ref, vc_ref, kn_ref, vn_ref, o_ref, *, steps, layer):
    T = steps
    W = WINDOW
    rows = T * GQA_GROUP
    gb = q_ref.shape[0]
    t_of_row = lax.broadcasted_iota(jnp.int32, (1, rows, 1), 1) // GQA_GROUP
    kj = lax.broadcasted_iota(jnp.int32, (1, 1, W), 2)
    cmask = kj > t_of_row
    for h in range(N_KV_HEADS_B):
        lo, hi = h * HEAD_DIM_B, (h + 1) * HEAD_DIM_B
        qh = q_ref[:, h]
        qh_r = qh.astype(BF16)
        kch = kc_ref[:, :, lo:hi].astype(BF16)
        vch = vc_ref[:, :, lo:hi].astype(BF16)
        sc = jnp.einsum("bqd,bkd->bqk", qh_r, kch, preferred_element_type=F32) * (HEAD_DIM_B ** -0.5)
        sc = jnp.where(cmask, sc, NEG_BIG)
        sn = []
        for s in range(T):
            kn_s = kn_ref[:, s:s + 1, lo:hi]
            v = jnp.sum(qh * kn_s, axis=-1, keepdims=True) * (HEAD_DIM_B ** -0.5)
            sn.append(jnp.where(t_of_row >= s, v, NEG_BIG))
        sk = _sink_column(sink_ref, layer, h, (1, rows, 1), 1, 1)
        m = jnp.maximum(jnp.max(sc, axis=-1, keepdims=True), sk)
        for s in range(T):
            m = jnp.maximum(m, sn[s])
        pc = jnp.exp(sc - m)
        pn = [jnp.exp(sn[s] - m) for s in range(T)]
        den = jnp.sum(pc, axis=-1, keepdims=True) + jnp.exp(sk - m)
        for s in range(T):
            den = den + pn[s]
        inv = 1.0 / den
        o = jnp.einsum("bqk,bkd->bqd", (pc * inv).astype(BF16), vch, preferred_element_type=F32)
        for s in range(T):
            o = o + (pn[s] * inv) * vn_ref[:, s:s + 1, lo:hi]
        o_ref[:, h] = o


def _swa_sample(q_r, kc, vc, kn, vn, sinks, layer):
    nseq, _, rows, _ = q_r.shape
    steps = kn.shape[1]
    gb = min(SAMPLE_GROUP, nseq)
    kern = functools.partial(_swa_sample_kernel, steps=steps, layer=layer)
    return pl.pallas_call(
        kern,
        grid=(nseq // gb,),
        in_specs=[
            pl.BlockSpec(memory_space=pltpu.SMEM),
            pl.BlockSpec((gb, N_KV_HEADS_B, rows, HEAD_DIM_B), lambda i: (i, 0, 0, 0)),
            pl.BlockSpec((None, gb, WINDOW, KV_WIDTH_B), lambda i: (layer, i, 0, 0)),
            pl.BlockSpec((None, gb, WINDOW, KV_WIDTH_B), lambda i: (layer, i, 0, 0)),
            pl.BlockSpec((gb, steps, KV_WIDTH_B), lambda i: (i, 0, 0)),
            pl.BlockSpec((gb, steps, KV_WIDTH_B), lambda i: (i, 0, 0)),
        ],
        out_specs=pl.BlockSpec((gb, N_KV_HEADS_B, rows, HEAD_DIM_B), lambda i: (i, 0, 0, 0)),
        out_shape=jax.ShapeDtypeStruct(q_r.shape, F32),
        compiler_params=_cparams(("arbitrary",)),
        name="swa_sample",
    )(sinks, q_r, kc, vc, kn, vn)


def _prompt_layer(x, p, layer, batch, seq):
    tm = min(TOKEN_TILE, seq)
    rep = seq // tm
    proj, tail = _premix(x, p["mod_p"], p["g_pre_mix"], p["w_in"], layer, tm, rep, conv_w=p["conv_w"])
    o_a, ssm = _gdn_prompt(proj, p["hp"], p["gdn_norm_w"], layer, batch, seq)
    o_b = _swa_prompt(proj, p["sinks"], layer, batch, seq)
    x = _ffn(x, o_a, o_b, p["mod_p"], p["g_post_mix"], p["g_pre_ffn"], p["g_post_ffn"],
             p["w_o"], p["w_up"], p["w_down"], layer, tm, rep)
    p3 = proj.reshape(batch, seq, PROJ_COLS)
    conv = tail[:, 8 - (CONV_WIDTH - 1):, :]
    k_new = p3[:, seq - WINDOW:, OFF_KB:OFF_KB + KV_WIDTH_B].reshape(batch, WINDOW, N_KV_HEADS_B, HEAD_DIM_B)
    v_new = p3[:, seq - WINDOW:, OFF_VB:OFF_VB + KV_WIDTH_B].reshape(batch, WINDOW, N_KV_HEADS_B, HEAD_DIM_B)
    return x, conv, ssm, k_new, v_new


def _sample_layer(x, p, layer, nseq, steps, ssm_acc):
    proj = _premix(x, p["mod_s"], p["g_pre_mix"], p["w_in"], layer, nseq, steps)
    p3 = proj.reshape(steps, nseq, PROJ_COLS)
    o_a, ssm = _gdn_sample(p3, p["conv_tm"], p["state_ssm"], p["conv_w"], p["hp"], p["gdn_norm_w"], layer,
                           acc=ssm_acc)
    q_r = p3[:, :, OFF_QB:OFF_QB + WIDTH_B].reshape(steps, nseq, N_KV_HEADS_B, GQA_GROUP, HEAD_DIM_B)
    q_r = jnp.transpose(q_r, (1, 2, 0, 3, 4)).reshape(nseq, N_KV_HEADS_B, steps * GQA_GROUP, HEAD_DIM_B)
    k_new = jnp.transpose(p3[:, :, OFF_KB:OFF_KB + KV_WIDTH_B], (1, 0, 2))
    v_new = jnp.transpose(p3[:, :, OFF_VB:OFF_VB + KV_WIDTH_B], (1, 0, 2))
    o_b = _swa_sample(q_r, p["cache_k"], p["cache_v"], k_new, v_new, p["sinks"], layer)
    o_b = o_b.reshape(nseq, N_KV_HEADS_B, steps, GQA_GROUP, HEAD_DIM_B)
    o_b = jnp.transpose(o_b, (2, 0, 1, 3, 4)).reshape(steps * nseq, WIDTH_B)
    x = _ffn(x, o_a.reshape(steps * nseq, WIDTH_A), o_b, p["mod_s"], p["g_post_mix"], p["g_pre_ffn"],
             p["g_post_ffn"], p["w_o"], p["w_up"], p["w_down"], layer, nseq, steps)
    xp = jnp.concatenate([p["conv_tm"][layer], p3[:, :, OFF_XC:OFF_XC + CONV_CH]], axis=0)
    conv = jnp.transpose(xp[xp.shape[0] - (CONV_WIDTH - 1):], (1, 0, 2))
    return x, conv, ssm, k_new, v_new


def _pack_w_in(w_in):
    o1 = CONV_CH + WIDTH_A
    o2 = o1 + 2 * N_HEADS_A
    wt = jnp.transpose(w_in, (0, 2, 1))
    pad = jnp.zeros((wt.shape[0], PROJ_COLS - OFF_BA - 2 * N_HEADS_A, wt.shape[2]), wt.dtype)
    return jnp.concatenate([wt[:, :o1], wt[:, o2:], wt[:, o1:o2], pad], axis=1).astype(BF16)


def _head_params(a_log, dt_bias):
    depth = a_log.shape[0]
    hp = jnp.zeros((depth, 8, LANE), F32)
    hp = hp.at[:, 0, N_HEADS_A:2 * N_HEADS_A].set(a_log.astype(F32))
    hp = hp.at[:, 1, N_HEADS_A:2 * N_HEADS_A].set(dt_bias.astype(F32))
    return hp


def kernel(x_prompt, x_sample, state_conv, state_ssm, cache_swa_k, cache_swa_v, c_prompt, c_sample,
           g_pre_mix, g_post_mix, g_pre_ffn, g_post_ffn, w_ada, b_ada, w_in, conv_w, a_log, dt_bias,
           gdn_norm_w, sinks, w_o, w_up, w_down):
    depth = w_ada.shape[0]
    batch, seq, _ = x_prompt.shape
    nseq, steps, _ = x_sample.shape

    c_all = jnp.concatenate([c_sample, c_prompt], axis=0)
    rows = c_all.shape[0]
    rows_p = -(-rows // 8) * 8
    c_all = jnp.pad(c_all, ((0, rows_p - rows), (0, 0)))
    mod_all = _modulation(c_all, w_ada, b_ada)
    p = dict(
        mod_p=jnp.transpose(mod_all[:, :, nseq:nseq + batch], (0, 2, 1, 3)).reshape(depth, batch, N_MOD, 1, D_MODEL),
        mod_s=mod_all,
        g_pre_mix=g_pre_mix.reshape(depth, 1, D_MODEL), g_post_mix=g_post_mix.reshape(depth, 1, D_MODEL),
        g_pre_ffn=g_pre_ffn.reshape(depth, 1, D_MODEL), g_post_ffn=g_post_ffn.reshape(depth, 1, D_MODEL),
        w_in=_pack_w_in(w_in), conv_w=conv_w, hp=_head_params(a_log, dt_bias),
        gdn_norm_w=gdn_norm_w.reshape(depth, 1, HEAD_DIM_A), sinks=sinks,
        w_o=w_o.astype(BF16), w_up=w_up.astype(BF16), w_down=w_down.astype(BF16),
        conv_tm=jnp.transpose(state_conv, (0, 2, 1, 3)), state_ssm=state_ssm,
        cache_k=cache_swa_k.reshape(depth, nseq, WINDOW, KV_WIDTH_B),
        cache_v=cache_swa_v.reshape(depth, nseq, WINDOW, KV_WIDTH_B),
    )

    yp = x_prompt.reshape(batch * seq, D_MODEL)
    ys = jnp.transpose(x_sample, (1, 0, 2)).reshape(steps * nseq, D_MODEL)
    outs_p, outs_s = [], []
    ssm_s = jnp.zeros(state_ssm.shape, F32)
    for l in range(depth):
        yp, c1, s1, k1, v1 = _prompt_layer(yp, p, l, batch, seq)
        ys, c2, ssm_s, k2, v2 = _sample_layer(ys, p, l, nseq, steps, ssm_s)
        outs_p.append((c1, s1, k1, v1))
        outs_s.append((c2, None, k2, v2))

    y_prompt = yp.reshape(batch, seq, D_MODEL)
    y_sample = jnp.transpose(ys.reshape(steps, nseq, D_MODEL), (1, 0, 2))
    stack = lambda outs, i: jnp.stack([o[i] for o in outs])
    kv_shape = (depth, nseq, steps, N_KV_HEADS_B, HEAD_DIM_B)
    k_s = jnp.concatenate([cache_swa_k, stack(outs_s, 2).reshape(kv_shape)], axis=2)[:, :, -WINDOW:]
    v_s = jnp.concatenate([cache_swa_v, stack(outs_s, 3).reshape(kv_shape)], axis=2)[:, :, -WINDOW:]
    return (y_prompt, y_sample,
            stack(outs_p, 0), stack(outs_p, 1), stack(outs_p, 2), stack(outs_p, 3),
            stack(outs_s, 0), ssm_s, k_s, v_s)
```

```python
import functools

import jax
import jax.numpy as jnp
from jax import lax
from jax.experimental import pallas as pl
from jax.experimental.pallas import tpu as pltpu

F32 = jnp.float32
BF16 = jnp.bfloat16

D_MODEL = 1024
N_MOD = 6
N_HEADS_A = 4
HEAD_DIM_A = 128
WIDTH_A = N_HEADS_A * HEAD_DIM_A
CONV_WIDTH = 4
CONV_CH = 3 * WIDTH_A
N_Q_HEADS_B = 8
N_KV_HEADS_B = 2
GQA_GROUP = N_Q_HEADS_B // N_KV_HEADS_B
HEAD_DIM_B = 64
WIDTH_B = N_Q_HEADS_B * HEAD_DIM_B
KV_WIDTH_B = N_KV_HEADS_B * HEAD_DIM_B
WINDOW = 128
D_FF = 4 * D_MODEL
EPS = 1e-6
NEG_BIG = -1e30

OFF_XC = 0
OFF_Z = OFF_XC + CONV_CH
OFF_QB = OFF_Z + WIDTH_A
OFF_KB = OFF_QB + WIDTH_B
OFF_VB = OFF_KB + KV_WIDTH_B
OFF_BA = OFF_VB + KV_WIDTH_B
LANE = 128
PROJ_COLS = OFF_BA + LANE

GDN_CHUNK = 64
GDN_BLOCK = 512
GDN_UNROLL = 8
SWA_BLOCKS = 2
TOKEN_TILE = 512
SAMPLE_GROUP = 16
GDN_SAMPLE_GROUP = 32
SAMPLE_INNER = 4
VMEM_LIMIT = 56 * 1024 * 1024


def _silu(x):
    return x * jax.nn.sigmoid(x)


def _softplus(x):
    return jnp.maximum(x, 0.0) + jnp.log(1.0 + jnp.exp(-jnp.abs(x)))


def _rms(x, w):
    return x * lax.rsqrt(jnp.mean(x * x, axis=-1, keepdims=True) + EPS) * w


def _bdot(a, b):
    return jnp.dot(a.astype(BF16), b.astype(BF16), preferred_element_type=F32)


def _bdot_nt(a, b):
    return lax.dot_general(a.astype(BF16), b.astype(BF16), (((1,), (1,)), ((), ())),
                           preferred_element_type=F32)


def _bdot_tn(a, b):
    return lax.dot_general(a.astype(BF16), b.astype(BF16), (((0,), (0,)), ((), ())),
                           preferred_element_type=F32)


def _split3(x):
    hi = x.astype(BF16)
    r1 = x - hi.astype(F32)
    mid = r1.astype(BF16)
    lo = (r1 - mid.astype(F32)).astype(BF16)
    return hi, mid, lo


def _cparams(sem, flags=None):
    return pltpu.CompilerParams(dimension_semantics=sem, vmem_limit_bytes=VMEM_LIMIT, flags=flags)


def _mod_kernel(c_ref, w_ref, b_ref, o_ref):
    s = _silu(c_ref[...]).astype(BF16)
    o_ref[...] = jnp.dot(s, w_ref[...].astype(BF16), preferred_element_type=F32) + b_ref[...]


def _modulation(c_all, w_ada, b_ada):
    depth = w_ada.shape[0]
    rows = c_all.shape[0]
    return pl.pallas_call(
        _mod_kernel,
        grid=(depth, N_MOD),
        in_specs=[
            pl.BlockSpec((rows, D_MODEL), lambda l, j: (0, 0)),
            pl.BlockSpec((None, D_MODEL, D_MODEL), lambda l, j: (l, 0, j)),
            pl.BlockSpec((None, 1, D_MODEL), lambda l, j: (l, 0, j)),
        ],
        out_specs=pl.BlockSpec((None, None, rows, D_MODEL), lambda l, j: (l, j, 0, 0)),
        out_shape=jax.ShapeDtypeStruct((depth, N_MOD, rows, D_MODEL), F32),
        compiler_params=_cparams(("arbitrary", "arbitrary")),
        name="adaln_mod",
    )(c_all, w_ada, b_ada.reshape(depth, 1, N_MOD * D_MODEL))


def _mod_spec(mod, layer, tm, rep):
    if mod.ndim == 5:
        return pl.BlockSpec((None, None, N_MOD, 1, D_MODEL), lambda i: (layer, i // rep, 0, 0, 0))
    return pl.BlockSpec((None, N_MOD, tm, D_MODEL), lambda i: (layer, 0, 0, 0))


def _premix_kernel(x_ref, mod_ref, g_ref, w_ref, o_ref):
    x = x_ref[...]
    h = _rms(x, g_ref[...]) * (1.0 + mod_ref[1]) + mod_ref[0]
    o_ref[...] = jnp.dot(h.astype(BF16), w_ref[...], preferred_element_type=F32)


def _premix_conv_kernel(x_ref, mod_ref, g_ref, w_ref, cw_ref, o_ref, tail_ref, prev_scr, *, rep):
    tm = x_ref.shape[0]
    x = x_ref[...]
    h = (_rms(x, g_ref[...]) * (1.0 + mod_ref[1]) + mod_ref[0]).astype(BF16)

    @pl.when(pl.program_id(0) % rep == 0)
    def _():
        prev_scr[...] = jnp.zeros(prev_scr.shape, F32)

    def project(c0, c1):
        return jnp.dot(h, w_ref[:, c0:c1], preferred_element_type=F32)

    def conv_group(xc, grp):
        c0 = grp * WIDTH_A
        win = jnp.concatenate([prev_scr[:, c0:c0 + WIDTH_A], xc], axis=0)
        conv = xc * cw_ref[CONV_WIDTH - 1:CONV_WIDTH, c0:c0 + WIDTH_A]
        for s in range(1, CONV_WIDTH):
            conv = conv + pltpu.roll(win, s, 0)[8:] * cw_ref[CONV_WIDTH - 1 - s:CONV_WIDTH - s, c0:c0 + WIDTH_A]
        xs = _silu(conv)
        for hh in range(N_HEADS_A):
            blk = xs[:, hh * HEAD_DIM_A:(hh + 1) * HEAD_DIM_A]
            if grp < 2:
                scale = lax.rsqrt(jnp.sum(blk * blk, axis=-1, keepdims=True) + EPS)
                blk = blk * (scale * (HEAD_DIM_A ** -0.5) if grp == 0 else scale)
            o_ref[:, OFF_XC + c0 + hh * HEAD_DIM_A:OFF_XC + c0 + (hh + 1) * HEAD_DIM_A] = blk
        prev_scr[:, c0:c0 + WIDTH_A] = xc[tm - 8:, :]
        tail_ref[0, :, c0:c0 + WIDTH_A] = xc[tm - 8:, :]

    xq = project(OFF_XC, OFF_XC + WIDTH_A)
    xk = project(OFF_XC + WIDTH_A, OFF_XC + 2 * WIDTH_A)
    conv_group(xq, 0)
    xv = project(OFF_XC + 2 * WIDTH_A, OFF_XC + CONV_CH)
    conv_group(xk, 1)
    o_ref[:, OFF_Z:] = project(OFF_Z, PROJ_COLS)
    conv_group(xv, 2)


def _premix(x, mod, g, w, layer, tm, rep, conv_w=None):
    n = x.shape[0]
    in_specs = [
        pl.BlockSpec((tm, D_MODEL), lambda i: (i, 0)),
        _mod_spec(mod, layer, tm, rep),
        pl.BlockSpec((None, 1, D_MODEL), lambda i: (layer, 0, 0)),
        pl.BlockSpec((None, D_MODEL, PROJ_COLS), lambda i: (layer, 0, 0)),
    ]
    proj_spec = pl.BlockSpec((tm, PROJ_COLS), lambda i: (i, 0))
    proj_shape = jax.ShapeDtypeStruct((n, PROJ_COLS), F32)
    if conv_w is None:
        return pl.pallas_call(
            _premix_kernel, grid=(n // tm,), in_specs=in_specs, out_specs=proj_spec, out_shape=proj_shape,
            compiler_params=_cparams(("arbitrary",)), name="premix_proj",
        )(x, mod, g, w)
    nseq = n // (tm * rep)
    return pl.pallas_call(
        functools.partial(_premix_conv_kernel, rep=rep),
        grid=(n // tm,),
        in_specs=in_specs + [pl.BlockSpec((None, CONV_WIDTH, CONV_CH), lambda i: (layer, 0, 0))],
        out_specs=[proj_spec, pl.BlockSpec((1, 8, CONV_CH), lambda i: (i // rep, 0, 0))],
        out_shape=[proj_shape, jax.ShapeDtypeStruct((nseq, 8, CONV_CH), F32)],
        scratch_shapes=[pltpu.VMEM((8, CONV_CH), F32)],
        compiler_params=_cparams(("arbitrary",)),
        name="premix_conv_proj",
    )(x, mod, g, w, conv_w)


def _ffn_kernel(x_ref, oa_ref, ob_ref, mod_ref, gpm_ref, gpf_ref, gqf_ref, wo_ref, wup_ref, wdn_ref, o_ref):
    x = x_ref[...]
    o_mix = jnp.concatenate([oa_ref[...].astype(BF16), ob_ref[...].astype(BF16)], axis=1)
    mix = jnp.dot(o_mix, wo_ref[...], preferred_element_type=F32)
    x1 = x + mod_ref[2] * _rms(mix, gpm_ref[...])
    h = (_rms(x1, gpf_ref[...]) * (1.0 + mod_ref[4]) + mod_ref[3]).astype(BF16)
    ck = 1024
    acc = jnp.zeros(x.shape, F32)
    for c in range(D_FF // ck):
        u = jnp.dot(h, wup_ref[:, c * ck:(c + 1) * ck], preferred_element_type=F32)
        a = jnp.square(jnp.maximum(u, 0.0)).astype(BF16)
        acc = acc + jnp.dot(a, wdn_ref[c * ck:(c + 1) * ck, :], preferred_element_type=F32)
    o_ref[...] = x1 + mod_ref[5] * _rms(acc, gqf_ref[...])


def _ffn(x, oa, ob, mod, g_post_mix, g_pre_ffn, g_post_ffn, wo, wup, wdn, layer, tm, rep):
    n = x.shape[0]
    const = lambda i: (layer, 0, 0)
    return pl.pallas_call(
        _ffn_kernel,
        grid=(n // tm,),
        in_specs=[
            pl.BlockSpec((tm, D_MODEL), lambda i: (i, 0)),
            pl.BlockSpec((tm, WIDTH_A), lambda i: (i, 0)),
            pl.BlockSpec((tm, WIDTH_B), lambda i: (i, 0)),
            _mod_spec(mod, layer, tm, rep),
            pl.BlockSpec((None, 1, D_MODEL), const),
            pl.BlockSpec((None, 1, D_MODEL), const),
            pl.BlockSpec((None, 1, D_MODEL), const),
            pl.BlockSpec((None, D_MODEL, D_MODEL), const, pipeline_mode=pl.Buffered(1)),
            pl.BlockSpec((None, D_MODEL, D_FF), const, pipeline_mode=pl.Buffered(1)),
            pl.BlockSpec((None, D_FF, D_MODEL), const, pipeline_mode=pl.Buffered(1)),
        ],
        out_specs=pl.BlockSpec((tm, D_MODEL), lambda i: (i, 0)),
        out_shape=jax.ShapeDtypeStruct((n, D_MODEL), F32),
        compiler_params=_cparams(("arbitrary",)),
        name="outproj_ffn",
    )(x, oa, ob, mod, g_post_mix, g_pre_ffn, g_post_ffn, wo, wup, wdn)


def _lane_bcast(x, col, width=HEAD_DIM_A):
    return jnp.broadcast_to(x[:, col:col + 1], (x.shape[0], width))


def _gdn_prompt_kernel(xs_ref, z_ref, ba_ref, hp_ref, gnw_ref, o_ref, ssm_ref, s_scr, *, chunk, block, group):
    C = chunk
    G = group
    H = N_HEADS_A
    HD = HEAD_DIM_A
    PW = H * C
    j = pl.program_id(1)
    nj = pl.num_programs(1)

    @pl.when(j == 0)
    def _():
        s_scr[...] = jnp.zeros(s_scr.shape, F32)

    ri = lax.broadcasted_iota(jnp.int32, (C, PW), 0)
    li = lax.broadcasted_iota(jnp.int32, (C, PW), 1)
    lc = li % C
    lower = ri >= lc
    strict = ri > lc
    eye = ri == lc
    rb = lax.broadcasted_iota(jnp.int32, (PW, PW), 0) // C
    cb = lax.broadcasted_iota(jnp.int32, (PW, PW), 1) // C
    bd_mask = rb == cb
    kbd_mask = (lax.broadcasted_iota(jnp.int32, (PW, WIDTH_A), 0) // C
                == lax.broadcasted_iota(jnp.int32, (PW, WIDTH_A), 1) // HD)
    rhs_mask = (lax.broadcasted_iota(jnp.int32, (PW, 2 * WIDTH_A), 0) // C
                == (lax.broadcasted_iota(jnp.int32, (PW, 2 * WIDTH_A), 1) % WIDTH_A) // HD)
    tri = (lax.broadcasted_iota(jnp.int32, (C, C), 0)
           >= lax.broadcasted_iota(jnp.int32, (C, C), 1)).astype(BF16)
    lane128 = lax.broadcasted_iota(jnp.int32, (C, LANE), 1)

    neg_a = -jnp.exp(hp_ref[0:1, :])
    dt_b = hp_ref[1:2, :]
    gnw = gnw_ref[...]

    def bd(y):
        return jnp.where(bd_mask, jnp.concatenate([y] * H, axis=0), 0.0)

    def pmm(x, y):
        return _bdot(x, bd(y))

    def pack_cols(cols):
        per_tile = LANE // C
        tiles = []
        for t in range(H // per_tile):
            acc = cols[t * per_tile]
            for u in range(1, per_tile):
                acc = jnp.where(lane128 < u * C, acc, cols[t * per_tile + u])
            tiles.append(acc)
        return jnp.concatenate(tiles, axis=1)

    def prepare(r0):
        xs = xs_ref[pl.ds(r0, C), :]
        ba = ba_ref[pl.ds(r0, C), :]
        beta_all = jax.nn.sigmoid(ba)
        g_all = neg_a * _softplus(ba + dt_b)
        g_hi, g_mid, g_lo = _split3(g_all)
        g3 = jnp.dot(tri, jnp.concatenate([g_hi, g_mid, g_lo], axis=1), preferred_element_type=F32)
        gc_all = g3[:, 0:LANE] + g3[:, LANE:2 * LANE] + g3[:, 2 * LANE:3 * LANE]

        q, k, v, kb, beta_c, gc_c, eg_c = [], [], [], [], [], [], []
        for h in range(H):
            qh = xs[:, h * HD:(h + 1) * HD]
            kh = xs[:, WIDTH_A + h * HD:WIDTH_A + (h + 1) * HD]
            vh = xs[:, 2 * WIDTH_A + h * HD:2 * WIDTH_A + (h + 1) * HD]
            bh = _lane_bcast(beta_all, h)
            gh = _lane_bcast(gc_all, N_HEADS_A + h)
            q.append(qh)
            k.append(kh)
            v.append(vh)
            kb.append(kh * bh)
            beta_c.append(bh)
            gc_c.append(gh)
            eg_c.append(jnp.exp(gh))

        gcp = pack_cols(gc_c)
        gc_row = jnp.sum(jnp.where(eye, gcp, 0.0), axis=0, keepdims=True)
        decay = jnp.exp(jnp.where(lower, gcp - gc_row, NEG_BIG))

        k_p = jnp.concatenate(k, axis=1)
        kbd = jnp.where(kbd_mask, jnp.concatenate([k_p] * H, axis=0), 0.0)
        lhs = jnp.concatenate([jnp.concatenate(kb, axis=1), jnp.concatenate(q, axis=1)], axis=0)
        kq = _bdot_nt(lhs, kbd)
        lmat = jnp.where(strict, kq[0:C] * decay, 0.0)
        rhs = jnp.concatenate([v[h] * beta_c[h] for h in range(H)]
                              + [kb[h] * eg_c[h] for h in range(H)], axis=1)
        g_last = [gc_all[C - 1:C, N_HEADS_A + h:N_HEADS_A + h + 1] for h in range(H)]
        return dict(lmat=lmat, attn=kq[C:2 * C] * decay, rhs=rhs, g_last=g_last,
                    qg=[q[h] * eg_c[h] for h in range(H)],
                    kg=[k[h] * jnp.exp(g_last[h] - gc_c[h]) for h in range(H)])

    base_blk = min(16, C)
    same_base = (ri // base_blk) == (lc // base_blk)

    def step(i, carry):
        r0s = [pl.multiple_of((i * G + g) * C, C) for g in range(G)]
        ch = [prepare(r0) for r0 in r0s]
        pw = [jnp.where(same_base, c["lmat"], 0.0) for c in ch]
        p = [jnp.where(eye, 1.0, 0.0) - x for x in pw]
        span = 2
        while span < base_blk:
            pw = [pmm(x, x) for x in pw]
            p = [x + pmm(x, y) for x, y in zip(p, pw)]
            span *= 2
        blk = base_blk
        same = same_base
        while blk < C:
            nxt = 2 * blk
            same_n = (ri // nxt) == (lc // nxt)
            off = jnp.logical_and(same_n, jnp.logical_not(same))
            t = [pmm(jnp.where(off, c["lmat"], 0.0), x) for c, x in zip(ch, p)]
            p = [x - pmm(x, y) for x, y in zip(p, t)]
            same = same_n
            blk = nxt

        sol = [_bdot(x, jnp.where(rhs_mask, jnp.concatenate([c["rhs"]] * H, axis=0), 0.0))
               for c, x in zip(ch, p)]
        asol = [_bdot(c["attn"], jnp.where(rhs_mask, jnp.concatenate([x] * H, axis=0), 0.0))
                for c, x in zip(ch, sol)]
        ab = [[_bdot_tn(c["kg"][h], jnp.concatenate(
                   [x[:, WIDTH_A + h * HD:WIDTH_A + (h + 1) * HD], x[:, h * HD:(h + 1) * HD]], axis=1))
               for h in range(H)] for c, x in zip(ch, sol)]
        for g in range(G):
            for h in range(H):
                q_eff = ch[g]["qg"][h] - asol[g][:, WIDTH_A + h * HD:WIDTH_A + (h + 1) * HD]
                sh = s_scr[h]
                o_h = _bdot(q_eff, sh) + asol[g][:, h * HD:(h + 1) * HD]
                s_scr[h] = (sh * jnp.exp(ch[g]["g_last"][h]) - _bdot(ab[g][h][:, 0:HD], sh)
                            + ab[g][h][:, HD:2 * HD])
                z_h = z_ref[pl.ds(r0s[g], C), h * HD:(h + 1) * HD]
                o_ref[pl.ds(r0s[g], C), h * HD:(h + 1) * HD] = (_rms(o_h, gnw) * _silu(z_h)).astype(o_ref.dtype)
        return carry

    lax.fori_loop(0, block // (G * C), step, 0)

    @pl.when(j == nj - 1)
    def _():
        ssm_ref[0] = s_scr[...]


def _gdn_prompt(proj, hp, gnw, layer, batch, seq):
    blk = min(GDN_BLOCK, seq)
    nblk = seq // blk
    chunk = min(GDN_CHUNK, seq)
    kern = functools.partial(_gdn_prompt_kernel, chunk=chunk, block=blk, group=min(GDN_UNROLL, blk // chunk))
    return pl.pallas_call(
        kern,
        grid=(batch, nblk),
        in_specs=[
            pl.BlockSpec((blk, CONV_CH), lambda b, j: (b * nblk + j, OFF_XC // CONV_CH)),
            pl.BlockSpec((blk, WIDTH_A), lambda b, j: (b * nblk + j, OFF_Z // WIDTH_A)),
            pl.BlockSpec((blk, LANE), lambda b, j: (b * nblk + j, OFF_BA // LANE)),
            pl.BlockSpec((None, 8, LANE), lambda b, j: (layer, 0, 0)),
            pl.BlockSpec((None, 1, HEAD_DIM_A), lambda b, j: (layer, 0, 0)),
        ],
        out_specs=[
            pl.BlockSpec((blk, WIDTH_A), lambda b, j: (b * nblk + j, 0)),
            pl.BlockSpec((1, N_HEADS_A, HEAD_DIM_A, HEAD_DIM_A), lambda b, j: (b, 0, 0, 0)),
        ],
        out_shape=[
            jax.ShapeDtypeStruct((batch * seq, WIDTH_A), BF16),
            jax.ShapeDtypeStruct((batch, N_HEADS_A, HEAD_DIM_A, HEAD_DIM_A), F32),
        ],
        scratch_shapes=[pltpu.VMEM((N_HEADS_A, HEAD_DIM_A, HEAD_DIM_A), F32)],
        compiler_params=_cparams(("arbitrary", "arbitrary")),
        name="gdn_prompt",
    )(proj, proj, proj, hp, gnw)


def _sink_column(sink_ref, layer, h, shape, axis, period):
    g = (lax.broadcasted_iota(jnp.int32, shape, axis) // period) % GQA_GROUP
    col = jnp.full(shape, sink_ref[layer, h * GQA_GROUP], F32)
    for gg in range(1, GQA_GROUP):
        col = jnp.where(g == gg, sink_ref[layer, h * GQA_GROUP + gg], col)
    return col


def _swa_prompt_kernel(sink_ref, q_ref, kc_ref, kp_ref, vc_ref, vp_ref, o_ref, *, layer, nsub):
    W = WINDOW
    j = pl.program_id(1)
    rows = GQA_GROUP * W
    q = (q_ref[...] * (HEAD_DIM_B ** -0.5)).astype(BF16)
    kall = jnp.concatenate([kp_ref[...], kc_ref[...]], axis=0).astype(BF16)
    vall = jnp.concatenate([vp_ref[...], vc_ref[...]], axis=0).astype(BF16)
    qi = lax.broadcasted_iota(jnp.int32, (rows, 2 * W), 0) % W
    kj = lax.broadcasted_iota(jnp.int32, (rows, 2 * W), 1)
    band = (kj > qi) & (kj <= qi + W)
    first_key = jnp.where(j > 0, 0, W)
    masks = [band & (kj >= first_key)] + [band] * (nsub - 1)
    probs = [(sb, h) for sb in range(nsub) for h in range(N_KV_HEADS_B)]
    s = []
    for sb, h in probs:
        k_h = kall[sb * W:(sb + 2) * W, h * HEAD_DIM_B:(h + 1) * HEAD_DIM_B]
        qs = jnp.concatenate(
            [q[sb * W:(sb + 1) * W, (h * GQA_GROUP + g) * HEAD_DIM_B:(h * GQA_GROUP + g + 1) * HEAD_DIM_B]
             for g in range(GQA_GROUP)], axis=0)
        s.append(jnp.where(masks[sb], _bdot_nt(qs, k_h), NEG_BIG))
    sk = [_sink_column(sink_ref, layer, h, (rows, 1), 0, W) for h in range(N_KV_HEADS_B)]
    m = [jnp.maximum(jnp.max(x, axis=-1, keepdims=True), sk[h]) for x, (sb, h) in zip(s, probs)]
    p = [jnp.exp(x - y) for x, y in zip(s, m)]
    inv = [1.0 / (jnp.sum(x, axis=-1, keepdims=True) + jnp.exp(sk[h] - y))
           for x, y, (sb, h) in zip(p, m, probs)]
    for x, r, (sb, h) in zip(p, inv, probs):
        v_h = vall[sb * W:(sb + 2) * W, h * HEAD_DIM_B:(h + 1) * HEAD_DIM_B]
        o = jnp.dot(x.astype(BF16), v_h, preferred_element_type=F32) * r
        for g in range(GQA_GROUP):
            c0 = (h * GQA_GROUP + g) * HEAD_DIM_B
            o_ref[sb * W:(sb + 1) * W, c0:c0 + HEAD_DIM_B] = o[g * W:(g + 1) * W, :].astype(o_ref.dtype)


def _swa_prompt(proj, sinks, layer, batch, seq):
    nsub = min(SWA_BLOCKS, seq // WINDOW)
    tq = nsub * WINDOW
    nb = seq // tq
    kcol = OFF_KB // KV_WIDTH_B
    vcol = OFF_VB // KV_WIDTH_B
    prev = lambda col: (lambda b, j: ((b * nb + j) * nsub - jnp.where(j > 0, 1, 0), col))
    return pl.pallas_call(
        functools.partial(_swa_prompt_kernel, layer=layer, nsub=nsub),
        grid=(batch, nb),
        in_specs=[
            pl.BlockSpec(memory_space=pltpu.SMEM),
            pl.BlockSpec((tq, WIDTH_B), lambda b, j: (b * nb + j, OFF_QB // WIDTH_B)),
            pl.BlockSpec((tq, KV_WIDTH_B), lambda b, j: (b * nb + j, kcol)),
            pl.BlockSpec((WINDOW, KV_WIDTH_B), prev(kcol)),
            pl.BlockSpec((tq, KV_WIDTH_B), lambda b, j: (b * nb + j, vcol)),
            pl.BlockSpec((WINDOW, KV_WIDTH_B), prev(vcol)),
        ],
        out_specs=pl.BlockSpec((tq, WIDTH_B), lambda b, j: (b * nb + j, 0)),
        out_shape=jax.ShapeDtypeStruct((batch * seq, WIDTH_B), BF16),
        compiler_params=_cparams(("arbitrary", "arbitrary")),
        name="swa_prompt",
    )(sinks, proj, proj, proj, proj, proj)


def _gdn_sample_kernel(xc_ref, z_ref, ba_ref, cs_ref, s0_ref, cw_ref, hp_ref, gnw_ref, acc_ref,
                       o_ref, s1_ref, lhs_scr, u_scr, kg_scr, res_scr, gl_scr, *, steps, group):
    del acc_ref
    T = steps
    GB = group
    H = N_HEADS_A
    HD = HEAD_DIM_A
    R = 2 * T

    u_scr[...] = jnp.zeros(u_scr.shape, F32)
    kg_scr[...] = jnp.zeros(kg_scr.shape, F32)

    xp = [cs_ref[i] for i in range(CONV_WIDTH - 1)] + [xc_ref[t] for t in range(T)]
    xs = []
    for t in range(T):
        acc = xp[t] * cw_ref[0:1, :]
        for i in range(1, CONV_WIDTH):
            acc = acc + xp[t + i] * cw_ref[i:i + 1, :]
        xs.append(_silu(acc))

    neg_a = -jnp.exp(hp_ref[0:1, :])
    dt_b = hp_ref[1:2, :]
    beta_all = [jax.nn.sigmoid(ba_ref[t]) for t in range(T)]
    g_all = [neg_a * _softplus(ba_ref[t] + dt_b) for t in range(T)]
    gc_all = [g_all[0]]
    for t in range(1, T):
        gc_all.append(gc_all[t - 1] + g_all[t])

    u_keep, attn_keep = [], []
    for h in range(H):
        q, k, v, kb, beta, gc = [], [], [], [], [], []
        for t in range(T):
            qh = xs[t][:, h * HD:(h + 1) * HD]
            kh = xs[t][:, WIDTH_A + h * HD:WIDTH_A + (h + 1) * HD]
            vh = xs[t][:, 2 * WIDTH_A + h * HD:2 * WIDTH_A + (h + 1) * HD]
            qh = qh * lax.rsqrt(jnp.sum(qh * qh, axis=-1, keepdims=True) + EPS) * (HD ** -0.5)
            kh = kh * lax.rsqrt(jnp.sum(kh * kh, axis=-1, keepdims=True) + EPS)
            bt = beta_all[t][:, h:h + 1]
            q.append(qh)
            k.append(kh)
            v.append(vh)
            kb.append(kh * bt)
            beta.append(bt)
            gc.append(gc_all[t][:, H + h:H + h + 1])
        u, w, attn = [], [], []
        for t in range(T):
            ut = v[t] * beta[t]
            wt = kb[t] * jnp.exp(gc[t])
            arow = []
            for s in range(t + 1):
                dec = jnp.exp(gc[t] - gc[s])
                arow.append(jnp.sum(q[t] * k[s], axis=-1, keepdims=True) * dec)
                if s < t:
                    l_ts = jnp.sum(kb[t] * k[s], axis=-1, keepdims=True) * dec
                    ut = ut - l_ts * u[s]
                    wt = wt - l_ts * w[s]
            u.append(ut)
            w.append(wt)
            attn.append(arow)
        g_last = gc[T - 1]
        for t in range(T):
            row0 = h * GB * R
            lhs_scr[pl.ds(row0 + t, GB, stride=R), :] = w[t]
            lhs_scr[pl.ds(row0 + T + t, GB, stride=R), :] = q[t] * jnp.exp(gc[t])
            u_scr[pl.ds(row0 + t, GB, stride=R), :] = u[t]
            kg_scr[pl.ds(row0 + t, GB, stride=R), :] = k[t] * jnp.exp(g_last - gc[t])
        gl_scr[h] = jnp.broadcast_to(jnp.exp(g_last), (GB, HD))
        u_keep.append(u)
        attn_keep.append(attn)

    inner = min(SAMPLE_INNER, GB)

    def body(i, carry):
        pairs = [(i * inner + g, h) for g in range(inner) for h in range(H)]
        rows = [pl.multiple_of((h * GB + b) * R, R) for b, h in pairs]
        res = [_bdot(lhs_scr[pl.ds(row, R), :], s0_ref[b, h]) for (b, h), row in zip(pairs, rows)]
        vn = []
        for row, r in zip(rows, res):
            res_scr[pl.ds(row, R), :] = r
            vn.append(u_scr[pl.ds(row, R), :] - r)
        upd = [_bdot_tn(kg_scr[pl.ds(row, R), :], x) for row, x in zip(rows, vn)]
        for (b, h), x in zip(pairs, upd):
            s1_ref[b, h] = s0_ref[b, h] * gl_scr[h, pl.ds(b, 1), :] + x
        return carry

    lax.fori_loop(0, GB // inner, body, 0)

    gnw = gnw_ref[...]
    for h in range(H):
        row0 = h * GB * R
        vn = []
        for t in range(T):
            ws_t = res_scr[pl.ds(row0 + t, GB, stride=R), :]
            vn.append(u_keep[h][t] - ws_t)
        for t in range(T):
            o_t = res_scr[pl.ds(row0 + T + t, GB, stride=R), :]
            for s in range(t + 1):
                o_t = o_t + attn_keep[h][t][s] * vn[s]
            z_t = z_ref[t][:, h * HD:(h + 1) * HD]
            o_ref[t, :, h * HD:(h + 1) * HD] = _rms(o_t, gnw) * _silu(z_t)


def _gdn_sample(proj3, cs_tm, s0, conv_w, hp, gnw, layer, acc):
    steps, nseq, _ = proj3.shape
    gb = min(GDN_SAMPLE_GROUP, nseq)
    rows = N_HEADS_A * gb * 2 * steps
    kern = functools.partial(_gdn_sample_kernel, steps=steps, group=gb)
    return pl.pallas_call(
        kern,
        grid=(nseq // gb,),
        input_output_aliases={8: 1},
        in_specs=[
            pl.BlockSpec((steps, gb, CONV_CH), lambda i: (0, i, OFF_XC // CONV_CH)),
            pl.BlockSpec((steps, gb, WIDTH_A), lambda i: (0, i, OFF_Z // WIDTH_A)),
            pl.BlockSpec((steps, gb, LANE), lambda i: (0, i, OFF_BA // LANE)),
            pl.BlockSpec((None, CONV_WIDTH - 1, gb, CONV_CH), lambda i: (layer, 0, i, 0)),
            pl.BlockSpec((None, gb, N_HEADS_A, HEAD_DIM_A, HEAD_DIM_A), lambda i: (layer, i, 0, 0, 0)),
            pl.BlockSpec((None, CONV_WIDTH, CONV_CH), lambda i: (layer, 0, 0)),
            pl.BlockSpec((None, 8, LANE), lambda i: (layer, 0, 0)),
            pl.BlockSpec((None, 1, HEAD_DIM_A), lambda i: (layer, 0, 0)),
            pl.BlockSpec(memory_space=pl.ANY),
        ],
        out_specs=[
            pl.BlockSpec((steps, gb, WIDTH_A), lambda i: (0, i, 0)),
            pl.BlockSpec((None, gb, N_HEADS_A, HEAD_DIM_A, HEAD_DIM_A), lambda i: (layer, i, 0, 0, 0)),
        ],
        out_shape=[
            jax.ShapeDtypeStruct((steps, nseq, WIDTH_A), F32),
            jax.ShapeDtypeStruct(s0.shape, F32),
        ],
        scratch_shapes=[
            pltpu.VMEM((rows, HEAD_DIM_A), F32),
            pltpu.VMEM((rows, HEAD_DIM_A), F32),
            pltpu.VMEM((rows, HEAD_DIM_A), F32),
            pltpu.VMEM((rows, HEAD_DIM_A), F32),
            pltpu.VMEM((N_HEADS_A, gb, HEAD_DIM_A), F32),
        ],
        compiler_params=_cparams(("arbitrary",)),
        name="gdn_sample",
    )(proj3, proj3, proj3, cs_tm, s0, conv_w, hp, gnw, acc)


def _swa_sample_kernel(sink_ref, q_ref, kc_ref, vc_ref, kn_ref, vn_ref, knt_ref, vnt_ref,
                       o_ref, ko_ref, vo_ref, *, steps, layer):
    T = steps
    W = WINDOW
    rows = T * GQA_GROUP
    t_of_row = lax.broadcasted_iota(jnp.int32, (1, rows, 1), 1) // GQA_GROUP
    kj = lax.broadcasted_iota(jnp.int32, (1, 1, W), 2)
    cmask = kj > t_of_row
    keep = lax.broadcasted_iota(jnp.int32, (1, 1, W), 2) < W - T
    for h in range(N_KV_HEADS_B):
        lo, hi = h * HEAD_DIM_B, (h + 1) * HEAD_DIM_B
        qh = q_ref[:, h]
        qh_r = qh.astype(BF16)
        kt = kc_ref[:, h]
        vt = vc_ref[:, h]
        ko_ref[:, h] = jnp.where(keep, pltpu.roll(kt, W - T, 2), knt_ref[:, h])
        vo_ref[:, h] = jnp.where(keep, pltpu.roll(vt, W - T, 2), vnt_ref[:, h])
        sc = jnp.einsum("bqd,bdk->bqk", qh_r, kt.astype(BF16),
                        preferred_element_type=F32) * (HEAD_DIM_B ** -0.5)
        sc = jnp.where(cmask, sc, NEG_BIG)
        sn = []
        for s in range(T):
            kn_s = kn_ref[:, s:s + 1, lo:hi]
            v = jnp.sum(qh * kn_s, axis=-1, keepdims=True) * (HEAD_DIM_B ** -0.5)
            sn.append(jnp.where(t_of_row >= s, v, NEG_BIG))
        sk = _sink_column(sink_ref, layer, h, (1, rows, 1), 1, 1)
        m = jnp.maximum(jnp.max(sc, axis=-1, keepdims=True), sk)
        for s in range(T):
            m = jnp.maximum(m, sn[s])
        pc = jnp.exp(sc - m)
        pn = [jnp.exp(sn[s] - m) for s in range(T)]
        den = jnp.sum(pc, axis=-1, keepdims=True) + jnp.exp(sk - m)
        for s in range(T):
            den = den + pn[s]
        inv = 1.0 / den
        o = jnp.einsum("bqk,bdk->bqd", (pc * inv).astype(BF16), vt.astype(BF16), preferred_element_type=F32)
        for s in range(T):
            o = o + (pn[s] * inv) * vn_ref[:, s:s + 1, lo:hi]
        o_ref[:, h] = o


def _swa_sample(q_r, kc_t, vc_t, kn, vn, knt, vnt, sinks, layer):
    nseq, _, rows, _ = q_r.shape
    steps = kn.shape[1]
    gb = min(SAMPLE_GROUP, nseq)
    kern = functools.partial(_swa_sample_kernel, steps=steps, layer=layer)
    cache_spec = pl.BlockSpec((None, gb, N_KV_HEADS_B, HEAD_DIM_B, WINDOW), lambda i: (layer, i, 0, 0, 0))
    win_spec = pl.BlockSpec((gb, N_KV_HEADS_B, HEAD_DIM_B, WINDOW), lambda i: (i, 0, 0, 0))
    new_spec = pl.BlockSpec((gb, steps, KV_WIDTH_B), lambda i: (i, 0, 0))
    q_spec = pl.BlockSpec((gb, N_KV_HEADS_B, rows, HEAD_DIM_B), lambda i: (i, 0, 0, 0))
    win_shape = jax.ShapeDtypeStruct((nseq, N_KV_HEADS_B, HEAD_DIM_B, WINDOW), F32)
    return pl.pallas_call(
        kern,
        grid=(nseq // gb,),
        in_specs=[pl.BlockSpec(memory_space=pltpu.SMEM), q_spec, cache_spec, cache_spec,
                  new_spec, new_spec, win_spec, win_spec],
        out_specs=[q_spec, win_spec, win_spec],
        out_shape=[jax.ShapeDtypeStruct(q_r.shape, F32), win_shape, win_shape],
        compiler_params=_cparams(("arbitrary",)),
        name="swa_sample",
    )(sinks, q_r, kc_t, vc_t, kn, vn, knt, vnt)


def _prompt_layer(x, p, layer, batch, seq):
    tm = min(TOKEN_TILE, seq)
    rep = seq // tm
    proj, tail = _premix(x, p["mod_p"], p["g_pre_mix"], p["w_in"], layer, tm, rep, conv_w=p["conv_w"])
    o_a, ssm = _gdn_prompt(proj, p["hp"], p["gdn_norm_w"], layer, batch, seq)
    o_b = _swa_prompt(proj, p["sinks"], layer, batch, seq)
    x = _ffn(x, o_a, o_b, p["mod_p"], p["g_post_mix"], p["g_pre_ffn"], p["g_post_ffn"],
             p["w_o"], p["w_up"], p["w_down"], layer, tm, rep)
    p3 = proj.reshape(batch, seq, PROJ_COLS)
    conv = tail[:, 8 - (CONV_WIDTH - 1):, :]
    k_new = p3[:, seq - WINDOW:, OFF_KB:OFF_KB + KV_WIDTH_B].reshape(batch, WINDOW, N_KV_HEADS_B, HEAD_DIM_B)
    v_new = p3[:, seq - WINDOW:, OFF_VB:OFF_VB + KV_WIDTH_B].reshape(batch, WINDOW, N_KV_HEADS_B, HEAD_DIM_B)
    return x, conv, ssm, k_new, v_new


def _sample_layer(x, p, layer, nseq, steps, ssm_acc):
    proj = _premix(x, p["mod_s"], p["g_pre_mix"], p["w_in"], layer, nseq, steps)
    p3 = proj.reshape(steps, nseq, PROJ_COLS)
    o_a, ssm = _gdn_sample(p3, p["conv_tm"], p["state_ssm"], p["conv_w"], p["hp"], p["gdn_norm_w"], layer,
                           acc=ssm_acc)
    q_r = p3[:, :, OFF_QB:OFF_QB + WIDTH_B].reshape(steps, nseq, N_KV_HEADS_B, GQA_GROUP, HEAD_DIM_B)
    q_r = jnp.transpose(q_r, (1, 2, 0, 3, 4)).reshape(nseq, N_KV_HEADS_B, steps * GQA_GROUP, HEAD_DIM_B)
    k_tm = p3[:, :, OFF_KB:OFF_KB + KV_WIDTH_B]
    v_tm = p3[:, :, OFF_VB:OFF_VB + KV_WIDTH_B]

    def tail_lanes(a):
        a = jnp.transpose(a.reshape(steps, nseq, N_KV_HEADS_B, HEAD_DIM_B), (1, 2, 3, 0))
        return jnp.pad(a, ((0, 0), (0, 0), (0, 0), (WINDOW - steps, 0)))

    o_b, k_win, v_win = _swa_sample(q_r, p["cache_kt"], p["cache_vt"], jnp.transpose(k_tm, (1, 0, 2)),
                                    jnp.transpose(v_tm, (1, 0, 2)), tail_lanes(k_tm), tail_lanes(v_tm),
                                    p["sinks"], layer)
    o_b = o_b.reshape(nseq, N_KV_HEADS_B, steps, GQA_GROUP, HEAD_DIM_B)
    o_b = jnp.transpose(o_b, (2, 0, 1, 3, 4)).reshape(steps * nseq, WIDTH_B)
    x = _ffn(x, o_a.reshape(steps * nseq, WIDTH_A), o_b, p["mod_s"], p["g_post_mix"], p["g_pre_ffn"],
             p["g_post_ffn"], p["w_o"], p["w_up"], p["w_down"], layer, nseq, steps)
    xp = jnp.concatenate([p["conv_tm"][layer], p3[:, :, OFF_XC:OFF_XC + CONV_CH]], axis=0)
    conv = jnp.transpose(xp[xp.shape[0] - (CONV_WIDTH - 1):], (1, 0, 2))
    return x, conv, ssm, k_win, v_win


def _pack_w_in(w_in):
    o1 = CONV_CH + WIDTH_A
    o2 = o1 + 2 * N_HEADS_A
    pad = jnp.zeros(w_in.shape[:-1] + (PROJ_COLS - OFF_BA - 2 * N_HEADS_A,), w_in.dtype)
    return jnp.concatenate([w_in[..., :o1], w_in[..., o2:], w_in[..., o1:o2], pad], axis=-1).astype(BF16)


def _head_params(a_log, dt_bias):
    depth = a_log.shape[0]
    hp = jnp.zeros((depth, 8, LANE), F32)
    hp = hp.at[:, 0, N_HEADS_A:2 * N_HEADS_A].set(a_log.astype(F32))
    hp = hp.at[:, 1, N_HEADS_A:2 * N_HEADS_A].set(dt_bias.astype(F32))
    return hp


def kernel(x_prompt, x_sample, state_conv, state_ssm, cache_swa_k, cache_swa_v, c_prompt, c_sample,
           g_pre_mix, g_post_mix, g_pre_ffn, g_post_ffn, w_ada, b_ada, w_in, conv_w, a_log, dt_bias,
           gdn_norm_w, sinks, w_o, w_up, w_down):
    depth = w_ada.shape[0]
    batch, seq, _ = x_prompt.shape
    nseq, steps, _ = x_sample.shape

    c_all = jnp.concatenate([c_sample, c_prompt], axis=0)
    rows = c_all.shape[0]
    rows_p = -(-rows // 8) * 8
    c_all = jnp.pad(c_all, ((0, rows_p - rows), (0, 0)))
    mod_all = _modulation(c_all, w_ada, b_ada)
    p = dict(
        mod_p=jnp.transpose(mod_all[:, :, nseq:nseq + batch], (0, 2, 1, 3)).reshape(depth, batch, N_MOD, 1, D_MODEL),
        mod_s=mod_all,
        g_pre_mix=g_pre_mix.reshape(depth, 1, D_MODEL), g_post_mix=g_post_mix.reshape(depth, 1, D_MODEL),
        g_pre_ffn=g_pre_ffn.reshape(depth, 1, D_MODEL), g_post_ffn=g_post_ffn.reshape(depth, 1, D_MODEL),
        w_in=_pack_w_in(w_in), conv_w=conv_w, hp=_head_params(a_log, dt_bias),
        gdn_norm_w=gdn_norm_w.reshape(depth, 1, HEAD_DIM_A), sinks=sinks,
        w_o=w_o.astype(BF16), w_up=w_up.astype(BF16), w_down=w_down.astype(BF16),
        conv_tm=jnp.transpose(state_conv, (0, 2, 1, 3)), state_ssm=state_ssm,
        cache_kt=jnp.transpose(cache_swa_k, (0, 1, 3, 4, 2)),
        cache_vt=jnp.transpose(cache_swa_v, (0, 1, 3, 4, 2)),
    )

    yp = x_prompt.reshape(batch * seq, D_MODEL)
    ys = jnp.transpose(x_sample, (1, 0, 2)).reshape(steps * nseq, D_MODEL)
    outs_p, outs_s = [], []
    ssm_s = jnp.zeros(state_ssm.shape, F32)
    for l in range(depth):
        yp, c1, s1, k1, v1 = _prompt_layer(yp, p, l, batch, seq)
        ys, c2, ssm_s, k2, v2 = _sample_layer(ys, p, l, nseq, steps, ssm_s)
        outs_p.append((c1, s1, k1, v1))
        outs_s.append((c2, None, k2, v2))

    y_prompt = yp.reshape(batch, seq, D_MODEL)
    y_sample = jnp.transpose(ys.reshape(steps, nseq, D_MODEL), (1, 0, 2))
    stack = lambda outs, i: jnp.stack([o[i] for o in outs])
    k_s = jnp.transpose(stack(outs_s, 2), (0, 1, 4, 2, 3))
    v_s = jnp.transpose(stack(outs_s, 3), (0, 1, 4, 2, 3))
    return (y_prompt, y_sample,
            stack(outs_p, 0), stack(outs_p, 1), stack(outs_p, 2), stack(outs_p, 3),
            stack(outs_s, 0), ssm_s, k_s, v_s)
```

```python
import functools

import jax
import jax.numpy as jnp
from jax import lax
from jax.experimental import pallas as pl
from jax.experimental.pallas import tpu as pltpu

F32 = jnp.float32
BF16 = jnp.bfloat16

D_MODEL = 1024
N_MOD = 6
N_HEADS_A = 4
HEAD_DIM_A = 128
WIDTH_A = N_HEADS_A * HEAD_DIM_A
CONV_WIDTH = 4
CONV_CH = 3 * WIDTH_A
N_Q_HEADS_B = 8
N_KV_HEADS_B = 2
GQA_GROUP = N_Q_HEADS_B // N_KV_HEADS_B
HEAD_DIM_B = 64
WIDTH_B = N_Q_HEADS_B * HEAD_DIM_B
KV_WIDTH_B = N_KV_HEADS_B * HEAD_DIM_B
WINDOW = 128
D_FF = 4 * D_MODEL
EPS = 1e-6
NEG_BIG = -1e30

OFF_XC = 0
OFF_Z = OFF_XC + CONV_CH
OFF_QB = OFF_Z + WIDTH_A
OFF_KB = OFF_QB + WIDTH_B
OFF_VB = OFF_KB + KV_WIDTH_B
OFF_BA = OFF_VB + KV_WIDTH_B
LANE = 128
PROJ_COLS = OFF_BA + LANE

GDN_CHUNK = 64
GDN_BLOCK = 1024
GDN_UNROLL = 16
SWA_BLOCKS = 4
TOKEN_TILE = 512
SAMPLE_GROUP = 16
GDN_SAMPLE_GROUP = 32
SAMPLE_INNER = 4
VMEM_LIMIT = 56 * 1024 * 1024


def _silu(x):
    return x * jax.nn.sigmoid(x)


def _softplus(x):
    return jnp.maximum(x, 0.0) + jnp.log(1.0 + jnp.exp(-jnp.abs(x)))


def _rms(x, w):
    return x * lax.rsqrt(jnp.mean(x * x, axis=-1, keepdims=True) + EPS) * w


def _bdot(a, b):
    return jnp.dot(a.astype(BF16), b.astype(BF16), preferred_element_type=F32)


def _bdot_nt(a, b):
    return lax.dot_general(a.astype(BF16), b.astype(BF16), (((1,), (1,)), ((), ())),
                           preferred_element_type=F32)


def _bdot_tn(a, b):
    return lax.dot_general(a.astype(BF16), b.astype(BF16), (((0,), (0,)), ((), ())),
                           preferred_element_type=F32)


def _split3(x):
    hi = x.astype(BF16)
    r1 = x - hi.astype(F32)
    mid = r1.astype(BF16)
    lo = (r1 - mid.astype(F32)).astype(BF16)
    return hi, mid, lo


def _cparams(sem, flags=None):
    return pltpu.CompilerParams(dimension_semantics=sem, vmem_limit_bytes=VMEM_LIMIT, flags=flags)


def _mod_kernel(c_ref, w_ref, b_ref, o_ref):
    s = _silu(c_ref[...]).astype(BF16)
    o_ref[...] = jnp.dot(s, w_ref[...].astype(BF16), preferred_element_type=F32) + b_ref[...]


def _modulation(c_all, w_ada, b_ada):
    depth = w_ada.shape[0]
    rows = c_all.shape[0]
    return pl.pallas_call(
        _mod_kernel,
        grid=(depth, N_MOD),
        in_specs=[
            pl.BlockSpec((rows, D_MODEL), lambda l, j: (0, 0)),
            pl.BlockSpec((None, D_MODEL, D_MODEL), lambda l, j: (l, 0, j)),
            pl.BlockSpec((None, 1, D_MODEL), lambda l, j: (l, 0, j)),
        ],
        out_specs=pl.BlockSpec((None, None, rows, D_MODEL), lambda l, j: (l, j, 0, 0)),
        out_shape=jax.ShapeDtypeStruct((depth, N_MOD, rows, D_MODEL), F32),
        compiler_params=_cparams(("arbitrary", "arbitrary")),
        name="adaln_mod",
    )(c_all, w_ada, b_ada.reshape(depth, 1, N_MOD * D_MODEL))


def _mod_spec(mod, layer, tm, rep):
    if mod.ndim == 5:
        return pl.BlockSpec((None, None, N_MOD, 1, D_MODEL), lambda i: (layer, i // rep, 0, 0, 0))
    return pl.BlockSpec((None, N_MOD, tm // rep, D_MODEL), lambda i: (layer, 0, 0, 0))


def _mod_rows(mod_ref, k, tm):
    m = mod_ref[k]
    if m.shape[0] in (1, tm):
        return m
    return jnp.concatenate([m] * (tm // m.shape[0]), axis=0)


def _premix_kernel(x_ref, mod_ref, g_ref, wa_ref, wb_ref, o_ref):
    x = x_ref[...]
    tm = x.shape[0]
    h = (_rms(x, g_ref[...]) * (1.0 + _mod_rows(mod_ref, 1, tm)) + _mod_rows(mod_ref, 0, tm)).astype(BF16)
    o_ref[:, :OFF_QB] = jnp.dot(h, wa_ref[...], preferred_element_type=F32)
    o_ref[:, OFF_QB:] = jnp.dot(h, wb_ref[...], preferred_element_type=F32)


def _premix_conv_kernel(x_ref, mod_ref, g_ref, wa_ref, wb_ref, cw_ref, o_ref, tail_ref, prev_scr, *, rep):
    tm = x_ref.shape[0]
    x = x_ref[...]
    h = (_rms(x, g_ref[...]) * (1.0 + _mod_rows(mod_ref, 1, tm)) + _mod_rows(mod_ref, 0, tm)).astype(BF16)

    @pl.when(pl.program_id(0) % rep == 0)
    def _():
        prev_scr[...] = jnp.zeros(prev_scr.shape, F32)

    def project(c0, c1):
        return jnp.dot(h, wa_ref[:, c0:c1], preferred_element_type=F32)

    def conv_group(xc, grp):
        c0 = grp * WIDTH_A
        win = jnp.concatenate([prev_scr[:, c0:c0 + WIDTH_A], xc], axis=0)
        conv = xc * cw_ref[CONV_WIDTH - 1:CONV_WIDTH, c0:c0 + WIDTH_A]
        for s in range(1, CONV_WIDTH):
            conv = conv + pltpu.roll(win, s, 0)[8:] * cw_ref[CONV_WIDTH - 1 - s:CONV_WIDTH - s, c0:c0 + WIDTH_A]
        xs = _silu(conv)
        for hh in range(N_HEADS_A):
            blk = xs[:, hh * HEAD_DIM_A:(hh + 1) * HEAD_DIM_A]
            if grp < 2:
                scale = lax.rsqrt(jnp.sum(blk * blk, axis=-1, keepdims=True) + EPS)
                blk = blk * (scale * (HEAD_DIM_A ** -0.5) if grp == 0 else scale)
            o_ref[:, OFF_XC + c0 + hh * HEAD_DIM_A:OFF_XC + c0 + (hh + 1) * HEAD_DIM_A] = blk
        prev_scr[:, c0:c0 + WIDTH_A] = xc[tm - 8:, :]
        tail_ref[0, :, c0:c0 + WIDTH_A] = xc[tm - 8:, :]

    xq = project(OFF_XC, OFF_XC + WIDTH_A)
    xk = project(OFF_XC + WIDTH_A, OFF_XC + 2 * WIDTH_A)
    conv_group(xq, 0)
    xv = project(OFF_XC + 2 * WIDTH_A, OFF_XC + CONV_CH)
    conv_group(xk, 1)
    o_ref[:, OFF_Z:OFF_QB] = project(OFF_Z, OFF_QB)
    o_ref[:, OFF_QB:] = jnp.dot(h, wb_ref[...], preferred_element_type=F32)
    conv_group(xv, 2)


def _premix(x, mod, g, w, layer, tm, rep, conv_w=None):
    n = x.shape[0]
    in_specs = [
        pl.BlockSpec((tm, D_MODEL), lambda i: (i, 0)),
        _mod_spec(mod, layer, tm, rep),
        pl.BlockSpec((None, 1, D_MODEL), lambda i: (layer, 0, 0)),
        pl.BlockSpec((None, D_MODEL, OFF_QB), lambda i: (layer, 0, 0)),
        pl.BlockSpec((None, D_MODEL, PROJ_COLS - OFF_QB), lambda i: (layer, 0, 0)),
    ]
    proj_spec = pl.BlockSpec((tm, PROJ_COLS), lambda i: (i, 0))
    proj_shape = jax.ShapeDtypeStruct((n, PROJ_COLS), F32)
    if conv_w is None:
        return pl.pallas_call(
            _premix_kernel, grid=(n // tm,), in_specs=in_specs, out_specs=proj_spec, out_shape=proj_shape,
            compiler_params=_cparams(("arbitrary",)), name="premix_proj",
        )(x, mod, g, w[0], w[1])
    nseq = n // (tm * rep)
    return pl.pallas_call(
        functools.partial(_premix_conv_kernel, rep=rep),
        grid=(n // tm,),
        in_specs=in_specs + [pl.BlockSpec((None, CONV_WIDTH, CONV_CH), lambda i: (layer, 0, 0))],
        out_specs=[proj_spec, pl.BlockSpec((1, 8, CONV_CH), lambda i: (i // rep, 0, 0))],
        out_shape=[proj_shape, jax.ShapeDtypeStruct((nseq, 8, CONV_CH), F32)],
        scratch_shapes=[pltpu.VMEM((8, CONV_CH), F32)],
        compiler_params=_cparams(("arbitrary",)),
        name="premix_conv_proj",
    )(x, mod, g, w[0], w[1], conv_w)


def _ffn_kernel(x_ref, oa_ref, ob_ref, mod_ref, gpm_ref, gpf_ref, gqf_ref, wo_ref, wup_ref, wdn_ref, o_ref):
    x = x_ref[...]
    o_mix = jnp.concatenate([oa_ref[...].astype(BF16), ob_ref[...].astype(BF16)], axis=1)
    mix = jnp.dot(o_mix, wo_ref[...], preferred_element_type=F32)
    tm = x.shape[0]
    x1 = x + _mod_rows(mod_ref, 2, tm) * _rms(mix, gpm_ref[...])
    h = (_rms(x1, gpf_ref[...]) * (1.0 + _mod_rows(mod_ref, 4, tm)) + _mod_rows(mod_ref, 3, tm)).astype(BF16)
    ck = 1024
    acc = jnp.zeros(x.shape, F32)
    for c in range(D_FF // ck):
        u = jnp.dot(h, wup_ref[:, c * ck:(c + 1) * ck], preferred_element_type=F32)
        a = jnp.square(jnp.maximum(u, 0.0)).astype(BF16)
        acc = acc + jnp.dot(a, wdn_ref[c * ck:(c + 1) * ck, :], preferred_element_type=F32)
    o_ref[...] = x1 + _mod_rows(mod_ref, 5, tm) * _rms(acc, gqf_ref[...])


def _ffn(x, oa, ob, mod, g_post_mix, g_pre_ffn, g_post_ffn, wo, wup, wdn, layer, tm, rep):
    n = x.shape[0]
    const = lambda i: (layer, 0, 0)
    return pl.pallas_call(
        _ffn_kernel,
        grid=(n // tm,),
        in_specs=[
            pl.BlockSpec((tm, D_MODEL), lambda i: (i, 0)),
            pl.BlockSpec((tm, WIDTH_A), lambda i: (i, 0)),
            pl.BlockSpec((tm, WIDTH_B), lambda i: (i, 0)),
            _mod_spec(mod, layer, tm, rep),
            pl.BlockSpec((None, 1, D_MODEL), const),
            pl.BlockSpec((None, 1, D_MODEL), const),
            pl.BlockSpec((None, 1, D_MODEL), const),
            pl.BlockSpec((None, D_MODEL, D_MODEL), const, pipeline_mode=pl.Buffered(1)),
            pl.BlockSpec((None, D_MODEL, D_FF), const, pipeline_mode=pl.Buffered(1)),
            pl.BlockSpec((None, D_FF, D_MODEL), const, pipeline_mode=pl.Buffered(1)),
        ],
        out_specs=pl.BlockSpec((tm, D_MODEL), lambda i: (i, 0)),
        out_shape=jax.ShapeDtypeStruct((n, D_MODEL), F32),
        compiler_params=_cparams(("arbitrary",)),
        name="outproj_ffn",
    )(x, oa, ob, mod, g_post_mix, g_pre_ffn, g_post_ffn, wo, wup, wdn)


def _lane_bcast(x, col, width=HEAD_DIM_A):
    return jnp.broadcast_to(x[:, col:col + 1], (x.shape[0], width))


def _gdn_prompt_kernel(xs_ref, z_ref, ba_ref, hp_ref, gnw_ref, o_ref, ssm_ref, s_scr, *, chunk, block, group):
    C = chunk
    G = group
    H = N_HEADS_A
    HD = HEAD_DIM_A
    PW = H * C
    j = pl.program_id(1)
    nj = pl.num_programs(1)

    @pl.when(j == 0)
    def _():
        s_scr[...] = jnp.zeros(s_scr.shape, F32)

    ri = lax.broadcasted_iota(jnp.int32, (C, PW), 0)
    li = lax.broadcasted_iota(jnp.int32, (C, PW), 1)
    lc = li % C
    lower = ri >= lc
    strict = ri > lc
    eye = ri == lc
    rb = lax.broadcasted_iota(jnp.int32, (PW, PW), 0) // C
    cb = lax.broadcasted_iota(jnp.int32, (PW, PW), 1) // C
    bd_mask = rb == cb
    kbd_mask = (lax.broadcasted_iota(jnp.int32, (PW, WIDTH_A), 0) // C
                == lax.broadcasted_iota(jnp.int32, (PW, WIDTH_A), 1) // HD)
    rhs_mask = (lax.broadcasted_iota(jnp.int32, (PW, 2 * WIDTH_A), 0) // C
                == (lax.broadcasted_iota(jnp.int32, (PW, 2 * WIDTH_A), 1) % WIDTH_A) // HD)
    tri = (lax.broadcasted_iota(jnp.int32, (C, C), 0)
           >= lax.broadcasted_iota(jnp.int32, (C, C), 1)).astype(BF16)
    lane128 = lax.broadcasted_iota(jnp.int32, (C, LANE), 1)

    neg_a = -jnp.exp(hp_ref[0:1, :])
    dt_b = hp_ref[1:2, :]
    gnw = gnw_ref[...]

    def bd(y):
        return jnp.where(bd_mask, jnp.concatenate([y] * H, axis=0), 0.0)

    def pmm(x, y):
        return _bdot(x, bd(y))

    def pack_cols(cols):
        per_tile = LANE // C
        tiles = []
        for t in range(H // per_tile):
            acc = cols[t * per_tile]
            for u in range(1, per_tile):
                acc = jnp.where(lane128 < u * C, acc, cols[t * per_tile + u])
            tiles.append(acc)
        return jnp.concatenate(tiles, axis=1)

    def prepare(r0):
        xs = xs_ref[pl.ds(r0, C), :]
        ba = ba_ref[pl.ds(r0, C), :]
        beta_all = jax.nn.sigmoid(ba)
        g_all = neg_a * _softplus(ba + dt_b)
        g_hi, g_mid, g_lo = _split3(g_all)
        g3 = jnp.dot(tri, jnp.concatenate([g_hi, g_mid, g_lo], axis=1), preferred_element_type=F32)
        gc_all = g3[:, 0:LANE] + g3[:, LANE:2 * LANE] + g3[:, 2 * LANE:3 * LANE]

        q, k, v, kb, beta_c, gc_c, eg_c = [], [], [], [], [], [], []
        for h in range(H):
            qh = xs[:, h * HD:(h + 1) * HD]
            kh = xs[:, WIDTH_A + h * HD:WIDTH_A + (h + 1) * HD]
            vh = xs[:, 2 * WIDTH_A + h * HD:2 * WIDTH_A + (h + 1) * HD]
            bh = _lane_bcast(beta_all, h)
            gh = _lane_bcast(gc_all, N_HEADS_A + h)
            q.append(qh)
            k.append(kh)
            v.append(vh)
            kb.append(kh * bh)
            beta_c.append(bh)
            gc_c.append(gh)
            eg_c.append(jnp.exp(gh))

        gcp = pack_cols(gc_c)
        gc_row = jnp.sum(jnp.where(eye, gcp, 0.0), axis=0, keepdims=True)
        decay = jnp.exp(jnp.where(lower, gcp - gc_row, NEG_BIG))

        k_p = jnp.concatenate(k, axis=1)
        kbd = jnp.where(kbd_mask, jnp.concatenate([k_p] * H, axis=0), 0.0)
        lhs = jnp.concatenate([jnp.concatenate(kb, axis=1), jnp.concatenate(q, axis=1)], axis=0)
        kq = _bdot_nt(lhs, kbd)
        lmat = jnp.where(strict, kq[0:C] * decay, 0.0)
        rhs = jnp.concatenate([v[h] * beta_c[h] for h in range(H)]
                              + [kb[h] * eg_c[h] for h in range(H)], axis=1)
        g_last = [gc_all[C - 1:C, N_HEADS_A + h:N_HEADS_A + h + 1] for h in range(H)]
        return dict(lmat=lmat, attn=kq[C:2 * C] * decay, rhs=rhs, g_last=g_last,
                    qg=[q[h] * eg_c[h] for h in range(H)],
                    kg=[k[h] * jnp.exp(g_last[h] - gc_c[h]) for h in range(H)])

    base_blk = min(16, C)
    same_base = (ri // base_blk) == (lc // base_blk)

    def prepare_phase(r0s, ch):
        return [lambda r0=r0: ch.append(prepare(r0)) for r0 in r0s]

    def solve_phase(ch, res):
        st = {}
        thunks = []

        def init():
            st["pw"] = [jnp.where(same_base, c["lmat"], 0.0) for c in ch]
            st["p"] = [jnp.where(eye, 1.0, 0.0) - x for x in st["pw"]]
        thunks.append(init)

        def square():
            st["pw"] = [pmm(x, x) for x in st["pw"]]

        def accumulate():
            st["p"] = [x + pmm(x, y) for x, y in zip(st["p"], st["pw"])]
        span = 2
        while span < base_blk:
            thunks += [square, accumulate]
            span *= 2
        blk = base_blk
        same = same_base
        while blk < C:
            nxt = 2 * blk
            same_n = (ri // nxt) == (lc // nxt)
            off = jnp.logical_and(same_n, jnp.logical_not(same))

            def left(off=off):
                st["t"] = [pmm(jnp.where(off, c["lmat"], 0.0), x) for c, x in zip(ch, st["p"])]

            def merge_blocks():
                st["p"] = [x - pmm(x, y) for x, y in zip(st["p"], st["t"])]
            thunks += [left, merge_blocks]
            same = same_n
            blk = nxt

        def solve():
            res["sol"] = [_bdot(x, jnp.where(rhs_mask, jnp.concatenate([c["rhs"]] * H, axis=0), 0.0))
                          for c, x in zip(ch, st["p"])]

        def attn_products():
            res["asol"] = [_bdot(c["attn"], jnp.where(rhs_mask, jnp.concatenate([x] * H, axis=0), 0.0))
                           for c, x in zip(ch, res["sol"])]

        def state_operands():
            res["ab"] = [[_bdot_tn(c["kg"][h], jnp.concatenate(
                              [x[:, WIDTH_A + h * HD:WIDTH_A + (h + 1) * HD], x[:, h * HD:(h + 1) * HD]], axis=1))
                          for h in range(H)] for c, x in zip(ch, res["sol"])]
        return thunks + [solve, attn_products, state_operands]

    def finish_phase(r0s, ch, res):
        def one(g):
            for h in range(H):
                asol = res["asol"][g]
                ab = res["ab"][g][h]
                q_eff = ch[g]["qg"][h] - asol[:, WIDTH_A + h * HD:WIDTH_A + (h + 1) * HD]
                sh = s_scr[h]
                o_h = _bdot(q_eff, sh) + asol[:, h * HD:(h + 1) * HD]
                s_scr[h] = sh * jnp.exp(ch[g]["g_last"][h]) - _bdot(ab[:, 0:HD], sh) + ab[:, HD:2 * HD]
                z_h = z_ref[pl.ds(r0s[g], C), h * HD:(h + 1) * HD]
                o_ref[pl.ds(r0s[g], C), h * HD:(h + 1) * HD] = (_rms(o_h, gnw) * _silu(z_h)).astype(o_ref.dtype)
        return [lambda g=g: one(g) for g in range(len(r0s))]

    def run_merged(a, b=()):
        order = sorted([((k + 0.5) / len(a), 0, k) for k in range(len(a))]
                       + [((k + 0.5) / len(b), 1, k) for k in range(len(b))])
        for _, which, k in order:
            (a, b)[which][k]()

    def step(i, carry):
        r0s = [pl.multiple_of((i * G + g) * C, C) for g in range(G)]
        halves = [r0s[:max(G // 2, 1)], r0s[max(G // 2, 1):]]
        ch = [[], []]
        res = [{}, {}]
        run_merged(prepare_phase(halves[0], ch[0]))
        run_merged(solve_phase(ch[0], res[0]), prepare_phase(halves[1], ch[1]))
        if halves[1]:
            run_merged(solve_phase(ch[1], res[1]), finish_phase(halves[0], ch[0], res[0]))
            run_merged(finish_phase(halves[1], ch[1], res[1]))
        else:
            run_merged(finish_phase(halves[0], ch[0], res[0]))
        return carry

    lax.fori_loop(0, block // (G * C), step, 0)

    @pl.when(j == nj - 1)
    def _():
        ssm_ref[0] = s_scr[...]


def _gdn_prompt(proj, hp, gnw, layer, batch, seq):
    blk = min(GDN_BLOCK, seq)
    nblk = seq // blk
    chunk = min(GDN_CHUNK, seq)
    kern = functools.partial(_gdn_prompt_kernel, chunk=chunk, block=blk, group=min(GDN_UNROLL, blk // chunk))
    return pl.pallas_call(
        kern,
        grid=(batch, nblk),
        in_specs=[
            pl.BlockSpec((blk, CONV_CH), lambda b, j: (b * nblk + j, OFF_XC // CONV_CH)),
            pl.BlockSpec((blk, WIDTH_A), lambda b, j: (b * nblk + j, OFF_Z // WIDTH_A)),
            pl.BlockSpec((blk, LANE), lambda b, j: (b * nblk + j, OFF_BA // LANE)),
            pl.BlockSpec((None, 8, LANE), lambda b, j: (layer, 0, 0)),
            pl.BlockSpec((None, 1, HEAD_DIM_A), lambda b, j: (layer, 0, 0)),
        ],
        out_specs=[
            pl.BlockSpec((blk, WIDTH_A), lambda b, j: (b * nblk + j, 0)),
            pl.BlockSpec((1, N_HEADS_A, HEAD_DIM_A, HEAD_DIM_A), lambda b, j: (b, 0, 0, 0)),
        ],
        out_shape=[
            jax.ShapeDtypeStruct((batch * seq, WIDTH_A), BF16),
            jax.ShapeDtypeStruct((batch, N_HEADS_A, HEAD_DIM_A, HEAD_DIM_A), F32),
        ],
        scratch_shapes=[pltpu.VMEM((N_HEADS_A, HEAD_DIM_A, HEAD_DIM_A), F32)],
        compiler_params=_cparams(("arbitrary", "arbitrary")),
        name="gdn_prompt",
    )(proj, proj, proj, hp, gnw)


def _sink_column(sink_ref, layer, h, shape, axis, period):
    g = (lax.broadcasted_iota(jnp.int32, shape, axis) // period) % GQA_GROUP
    col = jnp.full(shape, sink_ref[layer, h * GQA_GROUP], F32)
    for gg in range(1, GQA_GROUP):
        col = jnp.where(g == gg, sink_ref[layer, h * GQA_GROUP + gg], col)
    return col


def _swa_prompt_kernel(sink_ref, q_ref, kc_ref, kp_ref, vc_ref, vp_ref, o_ref, *, layer, nsub):
    W = WINDOW
    j = pl.program_id(1)
    rows = GQA_GROUP * W
    q = (q_ref[...] * (HEAD_DIM_B ** -0.5)).astype(BF16)
    kall = jnp.concatenate([kp_ref[...], kc_ref[...]], axis=0).astype(BF16)
    vall = jnp.concatenate([vp_ref[...], vc_ref[...]], axis=0).astype(BF16)
    qi = lax.broadcasted_iota(jnp.int32, (rows, 2 * W), 0) % W
    kj = lax.broadcasted_iota(jnp.int32, (rows, 2 * W), 1)
    band = (kj > qi) & (kj <= qi + W)
    first_key = jnp.where(j > 0, 0, W)
    masks = [band & (kj >= first_key)] + [band] * (nsub - 1)
    probs = [(sb, h) for sb in range(nsub) for h in range(N_KV_HEADS_B)]
    s = []
    for sb, h in probs:
        k_h = kall[sb * W:(sb + 2) * W, h * HEAD_DIM_B:(h + 1) * HEAD_DIM_B]
        qs = jnp.concatenate(
            [q[sb * W:(sb + 1) * W, (h * GQA_GROUP + g) * HEAD_DIM_B:(h * GQA_GROUP + g + 1) * HEAD_DIM_B]
             for g in range(GQA_GROUP)], axis=0)
        s.append(jnp.where(masks[sb], _bdot_nt(qs, k_h), NEG_BIG))
    sk = [_sink_column(sink_ref, layer, h, (rows, 1), 0, W) for h in range(N_KV_HEADS_B)]
    m = [jnp.maximum(jnp.max(x, axis=-1, keepdims=True), sk[h]) for x, (sb, h) in zip(s, probs)]
    p = [jnp.exp(x - y) for x, y in zip(s, m)]
    inv = [1.0 / (jnp.sum(x, axis=-1, keepdims=True) + jnp.exp(sk[h] - y))
           for x, y, (sb, h) in zip(p, m, probs)]
    for x, r, (sb, h) in zip(p, inv, probs):
        v_h = vall[sb * W:(sb + 2) * W, h * HEAD_DIM_B:(h + 1) * HEAD_DIM_B]
        o = jnp.dot(x.astype(BF16), v_h, preferred_element_type=F32) * r
        for g in range(GQA_GROUP):
            c0 = (h * GQA_GROUP + g) * HEAD_DIM_B
            o_ref[sb * W:(sb + 1) * W, c0:c0 + HEAD_DIM_B] = o[g * W:(g + 1) * W, :].astype(o_ref.dtype)


def _swa_prompt(proj, sinks, layer, batch, seq):
    nsub = min(SWA_BLOCKS, seq // WINDOW)
    tq = nsub * WINDOW
    nb = seq // tq
    kcol = OFF_KB // KV_WIDTH_B
    vcol = OFF_VB // KV_WIDTH_B
    prev = lambda col: (lambda b, j: ((b * nb + j) * nsub - jnp.where(j > 0, 1, 0), col))
    return pl.pallas_call(
        functools.partial(_swa_prompt_kernel, layer=layer, nsub=nsub),
        grid=(batch, nb),
        in_specs=[
            pl.BlockSpec(memory_space=pltpu.SMEM),
            pl.BlockSpec((tq, WIDTH_B), lambda b, j: (b * nb + j, OFF_QB // WIDTH_B)),
            pl.BlockSpec((tq, KV_WIDTH_B), lambda b, j: (b * nb + j, kcol)),
            pl.BlockSpec((WINDOW, KV_WIDTH_B), prev(kcol)),
            pl.BlockSpec((tq, KV_WIDTH_B), lambda b, j: (b * nb + j, vcol)),
            pl.BlockSpec((WINDOW, KV_WIDTH_B), prev(vcol)),
        ],
        out_specs=pl.BlockSpec((tq, WIDTH_B), lambda b, j: (b * nb + j, 0)),
        out_shape=jax.ShapeDtypeStruct((batch * seq, WIDTH_B), BF16),
        compiler_params=_cparams(("arbitrary", "arbitrary")),
        name="swa_prompt",
    )(sinks, proj, proj, proj, proj, proj)


def _gdn_sample_kernel(xc_ref, z_ref, ba_ref, cs_ref, s0_ref, cw_ref, hp_ref, gnw_ref, acc_ref,
                       o_ref, s1_ref, lhs_scr, u_scr, kg_scr, res_scr, gl_scr, *, steps, group):
    del acc_ref
    T = steps
    GB = group
    H = N_HEADS_A
    HD = HEAD_DIM_A
    R = 2 * T

    u_scr[...] = jnp.zeros(u_scr.shape, F32)
    kg_scr[...] = jnp.zeros(kg_scr.shape, F32)

    xp = [cs_ref[i] for i in range(CONV_WIDTH - 1)] + [xc_ref[t] for t in range(T)]
    xs = []
    for t in range(T):
        acc = xp[t] * cw_ref[0:1, :]
        for i in range(1, CONV_WIDTH):
            acc = acc + xp[t + i] * cw_ref[i:i + 1, :]
        xs.append(_silu(acc))

    neg_a = -jnp.exp(hp_ref[0:1, :])
    dt_b = hp_ref[1:2, :]
    beta_all = [jax.nn.sigmoid(ba_ref[t]) for t in range(T)]
    g_all = [neg_a * _softplus(ba_ref[t] + dt_b) for t in range(T)]
    gc_all = [g_all[0]]
    for t in range(1, T):
        gc_all.append(gc_all[t - 1] + g_all[t])

    u_keep, attn_keep = [], []
    for h in range(H):
        q, k, v, kb, beta, gc = [], [], [], [], [], []
        for t in range(T):
            qh = xs[t][:, h * HD:(h + 1) * HD]
            kh = xs[t][:, WIDTH_A + h * HD:WIDTH_A + (h + 1) * HD]
            vh = xs[t][:, 2 * WIDTH_A + h * HD:2 * WIDTH_A + (h + 1) * HD]
            qh = qh * lax.rsqrt(jnp.sum(qh * qh, axis=-1, keepdims=True) + EPS) * (HD ** -0.5)
            kh = kh * lax.rsqrt(jnp.sum(kh * kh, axis=-1, keepdims=True) + EPS)
            bt = beta_all[t][:, h:h + 1]
            q.append(qh)
            k.append(kh)
            v.append(vh)
            kb.append(kh * bt)
            beta.append(bt)
            gc.append(gc_all[t][:, H + h:H + h + 1])
        u, w, attn = [], [], []
        for t in range(T):
            ut = v[t] * beta[t]
            wt = kb[t] * jnp.exp(gc[t])
            arow = []
            for s in range(t + 1):
                dec = jnp.exp(gc[t] - gc[s])
                arow.append(jnp.sum(q[t] * k[s], axis=-1, keepdims=True) * dec)
                if s < t:
                    l_ts = jnp.sum(kb[t] * k[s], axis=-1, keepdims=True) * dec
                    ut = ut - l_ts * u[s]
                    wt = wt - l_ts * w[s]
            u.append(ut)
            w.append(wt)
            attn.append(arow)
        g_last = gc[T - 1]
        for t in range(T):
            row0 = h * GB * R
            lhs_scr[pl.ds(row0 + t, GB, stride=R), :] = w[t]
            lhs_scr[pl.ds(row0 + T + t, GB, stride=R), :] = q[t] * jnp.exp(gc[t])
            u_scr[pl.ds(row0 + t, GB, stride=R), :] = u[t]
            kg_scr[pl.ds(row0 + t, GB, stride=R), :] = k[t] * jnp.exp(g_last - gc[t])
        gl_scr[h] = jnp.broadcast_to(jnp.exp(g_last), (GB, HD))
        u_keep.append(u)
        attn_keep.append(attn)

    inner = min(SAMPLE_INNER, GB)

    def body(i, carry):
        pairs = [(i * inner + g, h) for g in range(inner) for h in range(H)]
        rows = [pl.multiple_of((h * GB + b) * R, R) for b, h in pairs]
        res = [_bdot(lhs_scr[pl.ds(row, R), :], s0_ref[b, h]) for (b, h), row in zip(pairs, rows)]
        vn = []
        for row, r in zip(rows, res):
            res_scr[pl.ds(row, R), :] = r
            vn.append(u_scr[pl.ds(row, R), :] - r)
        upd = [_bdot_tn(kg_scr[pl.ds(row, R), :], x) for row, x in zip(rows, vn)]
        for (b, h), x in zip(pairs, upd):
            s1_ref[b, h] = s0_ref[b, h] * gl_scr[h, pl.ds(b, 1), :] + x
        return carry

    lax.fori_loop(0, GB // inner, body, 0)

    gnw = gnw_ref[...]
    for h in range(H):
        row0 = h * GB * R
        vn = []
        for t in range(T):
            ws_t = res_scr[pl.ds(row0 + t, GB, stride=R), :]
            vn.append(u_keep[h][t] - ws_t)
        for t in range(T):
            o_t = res_scr[pl.ds(row0 + T + t, GB, stride=R), :]
            for s in range(t + 1):
                o_t = o_t + attn_keep[h][t][s] * vn[s]
            z_t = z_ref[t][:, h * HD:(h + 1) * HD]
            o_ref[t, :, h * HD:(h + 1) * HD] = _rms(o_t, gnw) * _silu(z_t)


def _gdn_sample(proj3, cs_tm, s0, conv_w, hp, gnw, layer, acc):
    steps, nseq, _ = proj3.shape
    gb = min(GDN_SAMPLE_GROUP, nseq)
    rows = N_HEADS_A * gb * 2 * steps
    kern = functools.partial(_gdn_sample_kernel, steps=steps, group=gb)
    return pl.pallas_call(
        kern,
        grid=(nseq // gb,),
        input_output_aliases={8: 1},
        in_specs=[
            pl.BlockSpec((steps, gb, CONV_CH), lambda i: (0, i, OFF_XC // CONV_CH)),
            pl.BlockSpec((steps, gb, WIDTH_A), lambda i: (0, i, OFF_Z // WIDTH_A)),
            pl.BlockSpec((steps, gb, LANE), lambda i: (0, i, OFF_BA // LANE)),
            pl.BlockSpec((None, CONV_WIDTH - 1, gb, CONV_CH), lambda i: (layer, 0, i, 0)),
            pl.BlockSpec((None, gb, N_HEADS_A, HEAD_DIM_A, HEAD_DIM_A), lambda i: (layer, i, 0, 0, 0)),
            pl.BlockSpec((None, CONV_WIDTH, CONV_CH), lambda i: (layer, 0, 0)),
            pl.BlockSpec((None, 8, LANE), lambda i: (layer, 0, 0)),
            pl.BlockSpec((None, 1, HEAD_DIM_A), lambda i: (layer, 0, 0)),
            pl.BlockSpec(memory_space=pl.ANY),
        ],
        out_specs=[
            pl.BlockSpec((steps, gb, WIDTH_A), lambda i: (0, i, 0)),
            pl.BlockSpec((None, gb, N_HEADS_A, HEAD_DIM_A, HEAD_DIM_A), lambda i: (layer, i, 0, 0, 0)),
        ],
        out_shape=[
            jax.ShapeDtypeStruct((steps, nseq, WIDTH_A), F32),
            jax.ShapeDtypeStruct(s0.shape, F32),
        ],
        scratch_shapes=[
            pltpu.VMEM((rows, HEAD_DIM_A), F32),
            pltpu.VMEM((rows, HEAD_DIM_A), F32),
            pltpu.VMEM((rows, HEAD_DIM_A), F32),
            pltpu.VMEM((rows, HEAD_DIM_A), F32),
            pltpu.VMEM((N_HEADS_A, gb, HEAD_DIM_A), F32),
        ],
        compiler_params=_cparams(("arbitrary",)),
        name="gdn_sample",
    )(proj3, proj3, proj3, cs_tm, s0, conv_w, hp, gnw, acc)


def _swa_sample_kernel(sink_ref, q_ref, kc_ref, vc_ref, kn_ref, vn_ref, knt_ref, vnt_ref,
                       o_ref, ko_ref, vo_ref, *, steps, layer):
    T = steps
    W = WINDOW
    rows = T * GQA_GROUP
    t_of_row = lax.broadcasted_iota(jnp.int32, (1, rows, 1), 1) // GQA_GROUP
    kj = lax.broadcasted_iota(jnp.int32, (1, 1, W), 2)
    cmask = kj > t_of_row
    keep = lax.broadcasted_iota(jnp.int32, (1, 1, W), 2) < W - T
    for h in range(N_KV_HEADS_B):
        lo, hi = h * HEAD_DIM_B, (h + 1) * HEAD_DIM_B
        qh = q_ref[:, h]
        qh_r = qh.astype(BF16)
        kt = kc_ref[:, h]
        vt = vc_ref[:, h]
        ko_ref[:, h] = jnp.where(keep, pltpu.roll(kt, W - T, 2), knt_ref[:, h])
        vo_ref[:, h] = jnp.where(keep, pltpu.roll(vt, W - T, 2), vnt_ref[:, h])
        sc = jnp.einsum("bqd,bdk->bqk", qh_r, kt.astype(BF16),
                        preferred_element_type=F32) * (HEAD_DIM_B ** -0.5)
        sc = jnp.where(cmask, sc, NEG_BIG)
        sn = []
        for s in range(T):
            kn_s = kn_ref[:, s:s + 1, lo:hi]
            v = jnp.sum(qh * kn_s, axis=-1, keepdims=True) * (HEAD_DIM_B ** -0.5)
            sn.append(jnp.where(t_of_row >= s, v, NEG_BIG))
        sk = _sink_column(sink_ref, layer, h, (1, rows, 1), 1, 1)
        m = jnp.maximum(jnp.max(sc, axis=-1, keepdims=True), sk)
        for s in range(T):
            m = jnp.maximum(m, sn[s])
        pc = jnp.exp(sc - m)
        pn = [jnp.exp(sn[s] - m) for s in range(T)]
        den = jnp.sum(pc, axis=-1, keepdims=True) + jnp.exp(sk - m)
        for s in range(T):
            den = den + pn[s]
        inv = 1.0 / den
        o = jnp.einsum("bqk,bdk->bqd", (pc * inv).astype(BF16), vt.astype(BF16), preferred_element_type=F32)
        for s in range(T):
            o = o + (pn[s] * inv) * vn_ref[:, s:s + 1, lo:hi]
        o_ref[:, h] = o


def _swa_sample(q_r, kc_t, vc_t, kn, vn, knt, vnt, sinks, layer):
    nseq, _, rows, _ = q_r.shape
    steps = kn.shape[1]
    gb = min(SAMPLE_GROUP, nseq)
    kern = functools.partial(_swa_sample_kernel, steps=steps, layer=layer)
    cache_spec = pl.BlockSpec((None, gb, N_KV_HEADS_B, HEAD_DIM_B, WINDOW), lambda i: (layer, i, 0, 0, 0))
    win_spec = pl.BlockSpec((gb, N_KV_HEADS_B, HEAD_DIM_B, WINDOW), lambda i: (i, 0, 0, 0))
    new_spec = pl.BlockSpec((gb, steps, KV_WIDTH_B), lambda i: (i, 0, 0))
    q_spec = pl.BlockSpec((gb, N_KV_HEADS_B, rows, HEAD_DIM_B), lambda i: (i, 0, 0, 0))
    win_shape = jax.ShapeDtypeStruct((nseq, N_KV_HEADS_B, HEAD_DIM_B, WINDOW), F32)
    return pl.pallas_call(
        kern,
        grid=(nseq // gb,),
        in_specs=[pl.BlockSpec(memory_space=pltpu.SMEM), q_spec, cache_spec, cache_spec,
                  new_spec, new_spec, win_spec, win_spec],
        out_specs=[q_spec, win_spec, win_spec],
        out_shape=[jax.ShapeDtypeStruct(q_r.shape, F32), win_shape, win_shape],
        compiler_params=_cparams(("arbitrary",)),
        name="swa_sample",
    )(sinks, q_r, kc_t, vc_t, kn, vn, knt, vnt)


def _prompt_layer(x, p, layer, batch, seq):
    tm = min(TOKEN_TILE, seq)
    rep = seq // tm
    proj, tail = _premix(x, p["mod_p"], p["g_pre_mix"], p["w_in"], layer, tm, rep, conv_w=p["conv_w"])
    o_a, ssm = _gdn_prompt(proj, p["hp"], p["gdn_norm_w"], layer, batch, seq)
    o_b = _swa_prompt(proj, p["sinks"], layer, batch, seq)
    x = _ffn(x, o_a, o_b, p["mod_p"], p["g_post_mix"], p["g_pre_ffn"], p["g_post_ffn"],
             p["w_o"], p["w_up"], p["w_down"], layer, tm, rep)
    p3 = proj.reshape(batch, seq, PROJ_COLS)
    conv = tail[:, 8 - (CONV_WIDTH - 1):, :]
    k_new = p3[:, seq - WINDOW:, OFF_KB:OFF_KB + KV_WIDTH_B].reshape(batch, WINDOW, N_KV_HEADS_B, HEAD_DIM_B)
    v_new = p3[:, seq - WINDOW:, OFF_VB:OFF_VB + KV_WIDTH_B].reshape(batch, WINDOW, N_KV_HEADS_B, HEAD_DIM_B)
    return x, conv, ssm, k_new, v_new


def _sample_layer(x, p, layer, nseq, steps, ssm_acc):
    proj = _premix(x, p["mod_s"], p["g_pre_mix"], p["w_in"], layer, steps * nseq, steps)
    p3 = proj.reshape(steps, nseq, PROJ_COLS)
    o_a, ssm = _gdn_sample(p3, p["conv_tm"], p["state_ssm"], p["conv_w"], p["hp"], p["gdn_norm_w"], layer,
                           acc=ssm_acc)
    q_r = p3[:, :, OFF_QB:OFF_QB + WIDTH_B].reshape(steps, nseq, N_KV_HEADS_B, GQA_GROUP, HEAD_DIM_B)
    q_r = jnp.transpose(q_r, (1, 2, 0, 3, 4)).reshape(nseq, N_KV_HEADS_B, steps * GQA_GROUP, HEAD_DIM_B)
    k_tm = p3[:, :, OFF_KB:OFF_KB + KV_WIDTH_B]
    v_tm = p3[:, :, OFF_VB:OFF_VB + KV_WIDTH_B]

    def tail_lanes(a):
        a = jnp.transpose(a.reshape(steps, nseq, N_KV_HEADS_B, HEAD_DIM_B), (1, 2, 3, 0))
        return jnp.pad(a, ((0, 0), (0, 0), (0, 0), (WINDOW - steps, 0)))

    o_b, k_win, v_win = _swa_sample(q_r, p["cache_kt"], p["cache_vt"], jnp.transpose(k_tm, (1, 0, 2)),
                                    jnp.transpose(v_tm, (1, 0, 2)), tail_lanes(k_tm), tail_lanes(v_tm),
                                    p["sinks"], layer)
    o_b = o_b.reshape(nseq, N_KV_HEADS_B, steps, GQA_GROUP, HEAD_DIM_B)
    o_b = jnp.transpose(o_b, (2, 0, 1, 3, 4)).reshape(steps * nseq, WIDTH_B)
    x = _ffn(x, o_a.reshape(steps * nseq, WIDTH_A), o_b, p["mod_s"], p["g_post_mix"], p["g_pre_ffn"],
             p["g_post_ffn"], p["w_o"], p["w_up"], p["w_down"], layer, steps * nseq, steps)
    xp = jnp.concatenate([p["conv_tm"][layer], p3[:, :, OFF_XC:OFF_XC + CONV_CH]], axis=0)
    conv = jnp.transpose(xp[xp.shape[0] - (CONV_WIDTH - 1):], (1, 0, 2))
    return x, conv, ssm, k_win, v_win


def _pack_w_in(w_in):
    o1 = CONV_CH + WIDTH_A
    o2 = o1 + 2 * N_HEADS_A
    pad = jnp.zeros(w_in.shape[:-1] + (PROJ_COLS - OFF_BA - 2 * N_HEADS_A,), w_in.dtype)
    tail = jnp.concatenate([w_in[..., o2:], w_in[..., o1:o2], pad], axis=-1)
    return w_in.astype(BF16), tail.astype(BF16)


def _head_params(a_log, dt_bias):
    depth = a_log.shape[0]
    hp = jnp.zeros((depth, 8, LANE), F32)
    hp = hp.at[:, 0, N_HEADS_A:2 * N_HEADS_A].set(a_log.astype(F32))
    hp = hp.at[:, 1, N_HEADS_A:2 * N_HEADS_A].set(dt_bias.astype(F32))
    return hp


def kernel(x_prompt, x_sample, state_conv, state_ssm, cache_swa_k, cache_swa_v, c_prompt, c_sample,
           g_pre_mix, g_post_mix, g_pre_ffn, g_post_ffn, w_ada, b_ada, w_in, conv_w, a_log, dt_bias,
           gdn_norm_w, sinks, w_o, w_up, w_down):
    depth = w_ada.shape[0]
    batch, seq, _ = x_prompt.shape
    nseq, steps, _ = x_sample.shape

    c_all = jnp.concatenate([c_sample, c_prompt], axis=0)
    rows = c_all.shape[0]
    rows_p = -(-rows // 8) * 8
    c_all = jnp.pad(c_all, ((0, rows_p - rows), (0, 0)))
    mod_all = _modulation(c_all, w_ada, b_ada)
    p = dict(
        mod_p=jnp.transpose(mod_all[:, :, nseq:nseq + batch], (0, 2, 1, 3)).reshape(depth, batch, N_MOD, 1, D_MODEL),
        mod_s=mod_all,
        g_pre_mix=g_pre_mix.reshape(depth, 1, D_MODEL), g_post_mix=g_post_mix.reshape(depth, 1, D_MODEL),
        g_pre_ffn=g_pre_ffn.reshape(depth, 1, D_MODEL), g_post_ffn=g_post_ffn.reshape(depth, 1, D_MODEL),
        w_in=_pack_w_in(w_in), conv_w=conv_w, hp=_head_params(a_log, dt_bias),
        gdn_norm_w=gdn_norm_w.reshape(depth, 1, HEAD_DIM_A), sinks=sinks,
        w_o=w_o.astype(BF16), w_up=w_up.astype(BF16), w_down=w_down.astype(BF16),
        conv_tm=jnp.transpose(state_conv, (0, 2, 1, 3)), state_ssm=state_ssm,
        cache_kt=jnp.transpose(cache_swa_k, (0, 1, 3, 4, 2)),
        cache_vt=jnp.transpose(cache_swa_v, (0, 1, 3, 4, 2)),
    )

    yp = x_prompt.reshape(batch * seq, D_MODEL)
    ys = jnp.transpose(x_sample, (1, 0, 2)).reshape(steps * nseq, D_MODEL)
    outs_p, outs_s = [], []
    ssm_s = jnp.zeros(state_ssm.shape, F32)
    for l in range(depth):
        yp, c1, s1, k1, v1 = _prompt_layer(yp, p, l, batch, seq)
        ys, c2, ssm_s, k2, v2 = _sample_layer(ys, p, l, nseq, steps, ssm_s)
        outs_p.append((c1, s1, k1, v1))
        outs_s.append((c2, None, k2, v2))

    y_prompt = yp.reshape(batch, seq, D_MODEL)
    y_sample = jnp.transpose(ys.reshape(steps, nseq, D_MODEL), (1, 0, 2))
    stack = lambda outs, i: jnp.stack([o[i] for o in outs])
    k_s = jnp.transpose(stack(outs_s, 2), (0, 1, 4, 2, 3))
    v_s = jnp.transpose(stack(outs_s, 3), (0, 1, 4, 2, 3))
    return (y_prompt, y_sample,
            stack(outs_p, 0), stack(outs_p, 1), stack(outs_p, 2), stack(outs_p, 3),
            stack(outs_s, 0), ssm_s, k_s, v_s)
```

```python
import functools

import jax
import jax.numpy as jnp
from jax import lax
from jax.experimental import pallas as pl
from jax.experimental.pallas import tpu as pltpu

F32 = jnp.float32
BF16 = jnp.bfloat16

D_MODEL = 1024
N_MOD = 6
N_HEADS_A = 4
HEAD_DIM_A = 128
WIDTH_A = N_HEADS_A * HEAD_DIM_A
CONV_WIDTH = 4
CONV_CH = 3 * WIDTH_A
N_Q_HEADS_B = 8
N_KV_HEADS_B = 2
GQA_GROUP = N_Q_HEADS_B // N_KV_HEADS_B
HEAD_DIM_B = 64
WIDTH_B = N_Q_HEADS_B * HEAD_DIM_B
KV_WIDTH_B = N_KV_HEADS_B * HEAD_DIM_B
WINDOW = 128
D_FF = 4 * D_MODEL
EPS = 1e-6
NEG_BIG = -1e30

OFF_XC = 0
OFF_Z = OFF_XC + CONV_CH
OFF_QB = OFF_Z + WIDTH_A
OFF_KB = OFF_QB + WIDTH_B
OFF_VB = OFF_KB + KV_WIDTH_B
OFF_BA = OFF_VB + KV_WIDTH_B
LANE = 128
PROJ_COLS = OFF_BA + LANE

GDN_CHUNK = 64
GDN_BLOCK = 512
GDN_UNROLL = 8
SWA_BLOCKS = 4
TOKEN_TILE = 512
SAMPLE_GROUP = 16
GDN_SAMPLE_GROUP = 32
SAMPLE_INNER = 4
VMEM_LIMIT = 56 * 1024 * 1024


def _silu(x):
    return x * jax.nn.sigmoid(x)


def _softplus(x):
    return jnp.maximum(x, 0.0) + jnp.log(1.0 + jnp.exp(-jnp.abs(x)))


def _rms(x, w):
    return x * lax.rsqrt(jnp.mean(x * x, axis=-1, keepdims=True) + EPS) * w


def _bdot(a, b):
    return jnp.dot(a.astype(BF16), b.astype(BF16), preferred_element_type=F32)


def _bdot_nt(a, b):
    return lax.dot_general(a.astype(BF16), b.astype(BF16), (((1,), (1,)), ((), ())),
                           preferred_element_type=F32)


def _bdot_tn(a, b):
    return lax.dot_general(a.astype(BF16), b.astype(BF16), (((0,), (0,)), ((), ())),
                           preferred_element_type=F32)


def _split3(x):
    hi = x.astype(BF16)
    r1 = x - hi.astype(F32)
    mid = r1.astype(BF16)
    lo = (r1 - mid.astype(F32)).astype(BF16)
    return hi, mid, lo


def _cparams(sem, flags=None):
    return pltpu.CompilerParams(dimension_semantics=sem, vmem_limit_bytes=VMEM_LIMIT, flags=flags)


def _mod_kernel(c_ref, w_ref, b_ref, o_ref):
    s = _silu(c_ref[...]).astype(BF16)
    o_ref[...] = jnp.dot(s, w_ref[...].astype(BF16), preferred_element_type=F32) + b_ref[...]


def _modulation(c_all, w_ada, b_ada):
    depth = w_ada.shape[0]
    rows = c_all.shape[0]
    return pl.pallas_call(
        _mod_kernel,
        grid=(depth, N_MOD),
        in_specs=[
            pl.BlockSpec((rows, D_MODEL), lambda l, j: (0, 0)),
            pl.BlockSpec((None, D_MODEL, D_MODEL), lambda l, j: (l, 0, j)),
            pl.BlockSpec((None, 1, D_MODEL), lambda l, j: (l, 0, j)),
        ],
        out_specs=pl.BlockSpec((None, None, rows, D_MODEL), lambda l, j: (l, j, 0, 0)),
        out_shape=jax.ShapeDtypeStruct((depth, N_MOD, rows, D_MODEL), F32),
        compiler_params=_cparams(("arbitrary", "arbitrary")),
        name="adaln_mod",
    )(c_all, w_ada, b_ada.reshape(depth, 1, N_MOD * D_MODEL))


def _mod_spec(mod, layer, tm, rep):
    if mod.ndim == 5:
        return pl.BlockSpec((None, None, N_MOD, 1, D_MODEL), lambda i: (layer, i // rep, 0, 0, 0))
    return pl.BlockSpec((None, N_MOD, tm // rep, D_MODEL), lambda i: (layer, 0, 0, 0))


def _mod_rows(mod_ref, k, tm):
    m = mod_ref[k]
    if m.shape[0] in (1, tm):
        return m
    return jnp.concatenate([m] * (tm // m.shape[0]), axis=0)


def _premix_kernel(x_ref, mod_ref, g_ref, w_ref, o_ref):
    x = x_ref[...]
    tm = x.shape[0]
    h = (_rms(x, g_ref[...]) * (1.0 + _mod_rows(mod_ref, 1, tm)) + _mod_rows(mod_ref, 0, tm)).astype(BF16)
    o_ref[...] = jnp.dot(h, w_ref[...], preferred_element_type=F32)


def _premix_conv_kernel(x_ref, mod_ref, g_ref, w_ref, cw_ref, o_ref, tail_ref, prev_scr, *, rep):
    tm = x_ref.shape[0]
    x = x_ref[...]
    h = (_rms(x, g_ref[...]) * (1.0 + _mod_rows(mod_ref, 1, tm)) + _mod_rows(mod_ref, 0, tm)).astype(BF16)

    @pl.when(pl.program_id(0) % rep == 0)
    def _():
        prev_scr[...] = jnp.zeros(prev_scr.shape, F32)

    def project(c0, c1):
        return jnp.dot(h, w_ref[:, c0:c1], preferred_element_type=F32)

    def conv_group(xc, grp):
        c0 = grp * WIDTH_A
        win = jnp.concatenate([prev_scr[:, c0:c0 + WIDTH_A], xc], axis=0)
        conv = xc * cw_ref[CONV_WIDTH - 1:CONV_WIDTH, c0:c0 + WIDTH_A]
        for s in range(1, CONV_WIDTH):
            conv = conv + pltpu.roll(win, s, 0)[8:] * cw_ref[CONV_WIDTH - 1 - s:CONV_WIDTH - s, c0:c0 + WIDTH_A]
        xs = _silu(conv)
        for hh in range(N_HEADS_A):
            blk = xs[:, hh * HEAD_DIM_A:(hh + 1) * HEAD_DIM_A]
            if grp < 2:
                scale = lax.rsqrt(jnp.sum(blk * blk, axis=-1, keepdims=True) + EPS)
                blk = blk * (scale * (HEAD_DIM_A ** -0.5) if grp == 0 else scale)
            o_ref[:, OFF_XC + c0 + hh * HEAD_DIM_A:OFF_XC + c0 + (hh + 1) * HEAD_DIM_A] = blk
        prev_scr[:, c0:c0 + WIDTH_A] = xc[tm - 8:, :]
        tail_ref[0, :, c0:c0 + WIDTH_A] = xc[tm - 8:, :]

    xq = project(OFF_XC, OFF_XC + WIDTH_A)
    xk = project(OFF_XC + WIDTH_A, OFF_XC + 2 * WIDTH_A)
    conv_group(xq, 0)
    xv = project(OFF_XC + 2 * WIDTH_A, OFF_XC + CONV_CH)
    conv_group(xk, 1)
    o_ref[:, OFF_Z:] = project(OFF_Z, PROJ_COLS)
    conv_group(xv, 2)


def _premix(x, mod, g, w, layer, tm, rep, conv_w=None):
    n = x.shape[0]
    in_specs = [
        pl.BlockSpec((tm, D_MODEL), lambda i: (i, 0)),
        _mod_spec(mod, layer, tm, rep),
        pl.BlockSpec((None, 1, D_MODEL), lambda i: (layer, 0, 0)),
        pl.BlockSpec((None, D_MODEL, PROJ_COLS), lambda i: (layer, 0, 0)),
    ]
    proj_spec = pl.BlockSpec((tm, PROJ_COLS), lambda i: (i, 0))
    proj_shape = jax.ShapeDtypeStruct((n, PROJ_COLS), F32)
    if conv_w is None:
        return pl.pallas_call(
            _premix_kernel, grid=(n // tm,), in_specs=in_specs, out_specs=proj_spec, out_shape=proj_shape,
            compiler_params=_cparams(("arbitrary",)), name="premix_proj",
        )(x, mod, g, w)
    nseq = n // (tm * rep)
    return pl.pallas_call(
        functools.partial(_premix_conv_kernel, rep=rep),
        grid=(n // tm,),
        in_specs=in_specs + [pl.BlockSpec((None, CONV_WIDTH, CONV_CH), lambda i: (layer, 0, 0))],
        out_specs=[proj_spec, pl.BlockSpec((1, 8, CONV_CH), lambda i: (i // rep, 0, 0))],
        out_shape=[proj_shape, jax.ShapeDtypeStruct((nseq, 8, CONV_CH), F32)],
        scratch_shapes=[pltpu.VMEM((8, CONV_CH), F32)],
        compiler_params=_cparams(("arbitrary",)),
        name="premix_conv_proj",
    )(x, mod, g, w, conv_w)


def _ffn_kernel(x_ref, oa_ref, ob_ref, mod_ref, gpm_ref, gpf_ref, gqf_ref, wo_ref, wup_ref, wdn_ref, o_ref):
    x = x_ref[...]
    o_mix = jnp.concatenate([oa_ref[...].astype(BF16), ob_ref[...].astype(BF16)], axis=1)
    mix = jnp.dot(o_mix, wo_ref[...], preferred_element_type=F32)
    tm = x.shape[0]
    x1 = x + _mod_rows(mod_ref, 2, tm) * _rms(mix, gpm_ref[...])
    h = (_rms(x1, gpf_ref[...]) * (1.0 + _mod_rows(mod_ref, 4, tm)) + _mod_rows(mod_ref, 3, tm)).astype(BF16)
    ck = 1024
    acc = jnp.zeros(x.shape, F32)
    for c in range(D_FF // ck):
        u = jnp.dot(h, wup_ref[:, c * ck:(c + 1) * ck], preferred_element_type=F32)
        a = jnp.square(jnp.maximum(u, 0.0)).astype(BF16)
        acc = acc + jnp.dot(a, wdn_ref[c * ck:(c + 1) * ck, :], preferred_element_type=F32)
    o_ref[...] = x1 + _mod_rows(mod_ref, 5, tm) * _rms(acc, gqf_ref[...])


def _ffn(x, oa, ob, mod, g_post_mix, g_pre_ffn, g_post_ffn, wo, wup, wdn, layer, tm, rep):
    n = x.shape[0]
    const = lambda i: (layer, 0, 0)
    return pl.pallas_call(
        _ffn_kernel,
        grid=(n // tm,),
        in_specs=[
            pl.BlockSpec((tm, D_MODEL), lambda i: (i, 0)),
            pl.BlockSpec((tm, WIDTH_A), lambda i: (i, 0)),
            pl.BlockSpec((tm, WIDTH_B), lambda i: (i, 0)),
            _mod_spec(mod, layer, tm, rep),
            pl.BlockSpec((None, 1, D_MODEL), const),
            pl.BlockSpec((None, 1, D_MODEL), const),
            pl.BlockSpec((None, 1, D_MODEL), const),
            pl.BlockSpec((None, D_MODEL, D_MODEL), const, pipeline_mode=pl.Buffered(1)),
            pl.BlockSpec((None, D_MODEL, D_FF), const, pipeline_mode=pl.Buffered(1)),
            pl.BlockSpec((None, D_FF, D_MODEL), const, pipeline_mode=pl.Buffered(1)),
        ],
        out_specs=pl.BlockSpec((tm, D_MODEL), lambda i: (i, 0)),
        out_shape=jax.ShapeDtypeStruct((n, D_MODEL), F32),
        compiler_params=_cparams(("arbitrary",)),
        name="outproj_ffn",
    )(x, oa, ob, mod, g_post_mix, g_pre_ffn, g_post_ffn, wo, wup, wdn)


def _lane_bcast(x, col, width=HEAD_DIM_A):
    return jnp.broadcast_to(x[:, col:col + 1], (x.shape[0], width))


def _gdn_prompt_kernel(xs_ref, z_ref, ba_ref, hp_ref, gnw_ref, o_ref, ssm_ref, s_scr, *, chunk, block, group):
    C = chunk
    G = group
    H = N_HEADS_A
    HD = HEAD_DIM_A
    PW = H * C
    j = pl.program_id(1)
    nj = pl.num_programs(1)

    @pl.when(j == 0)
    def _():
        s_scr[...] = jnp.zeros(s_scr.shape, F32)

    ri = lax.broadcasted_iota(jnp.int32, (C, PW), 0)
    li = lax.broadcasted_iota(jnp.int32, (C, PW), 1)
    lc = li % C
    lower = ri >= lc
    strict = ri > lc
    eye = ri == lc
    rb = lax.broadcasted_iota(jnp.int32, (PW, PW), 0) // C
    cb = lax.broadcasted_iota(jnp.int32, (PW, PW), 1) // C
    bd_mask = rb == cb
    kbd_mask = (lax.broadcasted_iota(jnp.int32, (PW, WIDTH_A), 0) // C
                == lax.broadcasted_iota(jnp.int32, (PW, WIDTH_A), 1) // HD)
    rhs_mask = (lax.broadcasted_iota(jnp.int32, (PW, 2 * WIDTH_A), 0) // C
                == (lax.broadcasted_iota(jnp.int32, (PW, 2 * WIDTH_A), 1) % WIDTH_A) // HD)
    tri = (lax.broadcasted_iota(jnp.int32, (C, C), 0)
           >= lax.broadcasted_iota(jnp.int32, (C, C), 1)).astype(BF16)
    lane128 = lax.broadcasted_iota(jnp.int32, (C, LANE), 1)

    neg_a = -jnp.exp(hp_ref[0:1, :])
    dt_b = hp_ref[1:2, :]
    gnw = gnw_ref[...]

    def bd(y):
        return jnp.where(bd_mask, jnp.concatenate([y] * H, axis=0), 0.0)

    def pmm(x, y):
        return _bdot(x, bd(y))

    def pack_cols(cols):
        per_tile = LANE // C
        tiles = []
        for t in range(H // per_tile):
            acc = cols[t * per_tile]
            for u in range(1, per_tile):
                acc = jnp.where(lane128 < u * C, acc, cols[t * per_tile + u])
            tiles.append(acc)
        return jnp.concatenate(tiles, axis=1)

    def prepare(r0):
        xs = xs_ref[pl.ds(r0, C), :]
        ba = ba_ref[pl.ds(r0, C), :]
        beta_all = jax.nn.sigmoid(ba)
        g_all = neg_a * _softplus(ba + dt_b)
        g_hi, g_mid, g_lo = _split3(g_all)
        g3 = jnp.dot(tri, jnp.concatenate([g_hi, g_mid, g_lo], axis=1), preferred_element_type=F32)
        gc_all = g3[:, 0:LANE] + g3[:, LANE:2 * LANE] + g3[:, 2 * LANE:3 * LANE]

        q, k, v, kb, beta_c, gc_c, eg_c = [], [], [], [], [], [], []
        for h in range(H):
            qh = xs[:, h * HD:(h + 1) * HD]
            kh = xs[:, WIDTH_A + h * HD:WIDTH_A + (h + 1) * HD]
            vh = xs[:, 2 * WIDTH_A + h * HD:2 * WIDTH_A + (h + 1) * HD]
            bh = _lane_bcast(beta_all, h)
            gh = _lane_bcast(gc_all, N_HEADS_A + h)
            q.append(qh)
            k.append(kh)
            v.append(vh)
            kb.append(kh * bh)
            beta_c.append(bh)
            gc_c.append(gh)
            eg_c.append(jnp.exp(gh))

        gcp = pack_cols(gc_c)
        gc_row = jnp.sum(jnp.where(eye, gcp, 0.0), axis=0, keepdims=True)
        decay = jnp.exp(jnp.where(lower, gcp - gc_row, NEG_BIG))

        k_p = jnp.concatenate(k, axis=1)
        kbd = jnp.where(kbd_mask, jnp.concatenate([k_p] * H, axis=0), 0.0)
        lhs = jnp.concatenate([jnp.concatenate(kb, axis=1), jnp.concatenate(q, axis=1)], axis=0)
        kq = _bdot_nt(lhs, kbd)
        lmat = jnp.where(strict, kq[0:C] * decay, 0.0)
        rhs = jnp.concatenate([v[h] * beta_c[h] for h in range(H)]
                              + [kb[h] * eg_c[h] for h in range(H)], axis=1)
        g_last = [gc_all[C - 1:C, N_HEADS_A + h:N_HEADS_A + h + 1] for h in range(H)]
        return dict(lmat=lmat, attn=kq[C:2 * C] * decay, rhs=rhs, g_last=g_last,
                    qg=[q[h] * eg_c[h] for h in range(H)],
                    kg=[k[h] * jnp.exp(g_last[h] - gc_c[h]) for h in range(H)])

    base_blk = min(16, C)
    same_base = (ri // base_blk) == (lc // base_blk)

    def prepare_phase(r0s, ch):
        return [lambda r0=r0: ch.append(prepare(r0)) for r0 in r0s]

    def solve_phase(ch, res):
        st = {}
        thunks = []

        def init():
            st["pw"] = [jnp.where(same_base, c["lmat"], 0.0) for c in ch]
            st["p"] = [jnp.where(eye, 1.0, 0.0) - x for x in st["pw"]]
        thunks.append(init)

        def square():
            st["pw"] = [pmm(x, x) for x in st["pw"]]

        def accumulate():
            st["p"] = [x + pmm(x, y) for x, y in zip(st["p"], st["pw"])]
        span = 2
        while span < base_blk:
            thunks += [square, accumulate]
            span *= 2
        blk = base_blk
        same = same_base
        while blk < C:
            nxt = 2 * blk
            same_n = (ri // nxt) == (lc // nxt)
            off = jnp.logical_and(same_n, jnp.logical_not(same))

            def left(off=off):
                st["t"] = [pmm(jnp.where(off, c["lmat"], 0.0), x) for c, x in zip(ch, st["p"])]

            def merge_blocks():
                st["p"] = [x - pmm(x, y) for x, y in zip(st["p"], st["t"])]
            thunks += [left, merge_blocks]
            same = same_n
            blk = nxt

        def solve():
            res["sol"] = [_bdot(x, jnp.where(rhs_mask, jnp.concatenate([c["rhs"]] * H, axis=0), 0.0))
                          for c, x in zip(ch, st["p"])]

        def attn_products():
            res["asol"] = [_bdot(c["attn"], jnp.where(rhs_mask, jnp.concatenate([x] * H, axis=0), 0.0))
                           for c, x in zip(ch, res["sol"])]

        def state_operands():
            res["ab"] = [[_bdot_tn(c["kg"][h], jnp.concatenate(
                              [x[:, WIDTH_A + h * HD:WIDTH_A + (h + 1) * HD], x[:, h * HD:(h + 1) * HD]], axis=1))
                          for h in range(H)] for c, x in zip(ch, res["sol"])]
        return thunks + [solve, attn_products, state_operands]

    def finish_phase(r0s, ch, res):
        def one(g):
            for h in range(H):
                asol = res["asol"][g]
                ab = res["ab"][g][h]
                q_eff = ch[g]["qg"][h] - asol[:, WIDTH_A + h * HD:WIDTH_A + (h + 1) * HD]
                sh = s_scr[h]
                o_h = _bdot(q_eff, sh) + asol[:, h * HD:(h + 1) * HD]
                s_scr[h] = sh * jnp.exp(ch[g]["g_last"][h]) - _bdot(ab[:, 0:HD], sh) + ab[:, HD:2 * HD]
                z_h = z_ref[pl.ds(r0s[g], C), h * HD:(h + 1) * HD]
                o_ref[pl.ds(r0s[g], C), h * HD:(h + 1) * HD] = (_rms(o_h, gnw) * _silu(z_h)).astype(o_ref.dtype)
        return [lambda g=g: one(g) for g in range(len(r0s))]

    def step(i, carry):
        r0s = [pl.multiple_of((i * G + g) * C, C) for g in range(G)]
        ch, res = [], {}
        for thunk in prepare_phase(r0s, ch) + solve_phase(ch, res) + finish_phase(r0s, ch, res):
            thunk()
        return carry

    lax.fori_loop(0, block // (G * C), step, 0)

    @pl.when(j == nj - 1)
    def _():
        ssm_ref[0] = s_scr[...]


def _gdn_prompt(proj, hp, gnw, layer, batch, seq):
    blk = min(GDN_BLOCK, seq)
    nblk = seq // blk
    chunk = min(GDN_CHUNK, seq)
    kern = functools.partial(_gdn_prompt_kernel, chunk=chunk, block=blk, group=min(GDN_UNROLL, blk // chunk))
    return pl.pallas_call(
        kern,
        grid=(batch, nblk),
        in_specs=[
            pl.BlockSpec((blk, CONV_CH), lambda b, j: (b * nblk + j, OFF_XC // CONV_CH)),
            pl.BlockSpec((blk, WIDTH_A), lambda b, j: (b * nblk + j, OFF_Z // WIDTH_A)),
            pl.BlockSpec((blk, LANE), lambda b, j: (b * nblk + j, OFF_BA // LANE)),
            pl.BlockSpec((None, 8, LANE), lambda b, j: (layer, 0, 0)),
            pl.BlockSpec((None, 1, HEAD_DIM_A), lambda b, j: (layer, 0, 0)),
        ],
        out_specs=[
            pl.BlockSpec((blk, WIDTH_A), lambda b, j: (b * nblk + j, 0)),
            pl.BlockSpec((1, N_HEADS_A, HEAD_DIM_A, HEAD_DIM_A), lambda b, j: (b, 0, 0, 0)),
        ],
        out_shape=[
            jax.ShapeDtypeStruct((batch * seq, WIDTH_A), BF16),
            jax.ShapeDtypeStruct((batch, N_HEADS_A, HEAD_DIM_A, HEAD_DIM_A), F32),
        ],
        scratch_shapes=[pltpu.VMEM((N_HEADS_A, HEAD_DIM_A, HEAD_DIM_A), F32)],
        compiler_params=_cparams(("arbitrary", "arbitrary")),
        name="gdn_prompt",
    )(proj, proj, proj, hp, gnw)


def _sink_column(sink_ref, layer, h, shape, axis, period):
    g = (lax.broadcasted_iota(jnp.int32, shape, axis) // period) % GQA_GROUP
    col = jnp.full(shape, sink_ref[layer, h * GQA_GROUP], F32)
    for gg in range(1, GQA_GROUP):
        col = jnp.where(g == gg, sink_ref[layer, h * GQA_GROUP + gg], col)
    return col


def _swa_prompt_kernel(sink_ref, q_ref, kc_ref, kp_ref, vc_ref, vp_ref, o_ref, *, layer, nsub):
    W = WINDOW
    j = pl.program_id(1)
    rows = GQA_GROUP * W
    q = (q_ref[...] * (HEAD_DIM_B ** -0.5)).astype(BF16)
    kall = jnp.concatenate([kp_ref[...], kc_ref[...]], axis=0).astype(BF16)
    vall = jnp.concatenate([vp_ref[...], vc_ref[...]], axis=0).astype(BF16)
    qi = lax.broadcasted_iota(jnp.int32, (rows, 2 * W), 0) % W
    kj = lax.broadcasted_iota(jnp.int32, (rows, 2 * W), 1)
    band = (kj > qi) & (kj <= qi + W)
    first_key = jnp.where(j > 0, 0, W)
    masks = [band & (kj >= first_key)] + [band] * (nsub - 1)
    probs = [(sb, h) for sb in range(nsub) for h in range(N_KV_HEADS_B)]
    s = []
    for sb, h in probs:
        k_h = kall[sb * W:(sb + 2) * W, h * HEAD_DIM_B:(h + 1) * HEAD_DIM_B]
        qs = jnp.concatenate(
            [q[sb * W:(sb + 1) * W, (h * GQA_GROUP + g) * HEAD_DIM_B:(h * GQA_GROUP + g + 1) * HEAD_DIM_B]
             for g in range(GQA_GROUP)], axis=0)
        s.append(jnp.where(masks[sb], _bdot_nt(qs, k_h), NEG_BIG))
    sk = [_sink_column(sink_ref, layer, h, (rows, 1), 0, W) for h in range(N_KV_HEADS_B)]
    m = [jnp.maximum(jnp.max(x, axis=-1, keepdims=True), sk[h]) for x, (sb, h) in zip(s, probs)]
    p = [jnp.exp(x - y) for x, y in zip(s, m)]
    inv = [1.0 / (jnp.sum(x, axis=-1, keepdims=True) + jnp.exp(sk[h] - y))
           for x, y, (sb, h) in zip(p, m, probs)]
    for x, r, (sb, h) in zip(p, inv, probs):
        v_h = vall[sb * W:(sb + 2) * W, h * HEAD_DIM_B:(h + 1) * HEAD_DIM_B]
        o = jnp.dot(x.astype(BF16), v_h, preferred_element_type=F32) * r
        for g in range(GQA_GROUP):
            c0 = (h * GQA_GROUP + g) * HEAD_DIM_B
            o_ref[sb * W:(sb + 1) * W, c0:c0 + HEAD_DIM_B] = o[g * W:(g + 1) * W, :].astype(o_ref.dtype)


def _swa_prompt(proj, sinks, layer, batch, seq):
    nsub = min(SWA_BLOCKS, seq // WINDOW)
    tq = nsub * WINDOW
    nb = seq // tq
    kcol = OFF_KB // KV_WIDTH_B
    vcol = OFF_VB // KV_WIDTH_B
    prev = lambda col: (lambda b, j: ((b * nb + j) * nsub - jnp.where(j > 0, 1, 0), col))
    return pl.pallas_call(
        functools.partial(_swa_prompt_kernel, layer=layer, nsub=nsub),
        grid=(batch, nb),
        in_specs=[
            pl.BlockSpec(memory_space=pltpu.SMEM),
            pl.BlockSpec((tq, WIDTH_B), lambda b, j: (b * nb + j, OFF_QB // WIDTH_B)),
            pl.BlockSpec((tq, KV_WIDTH_B), lambda b, j: (b * nb + j, kcol)),
            pl.BlockSpec((WINDOW, KV_WIDTH_B), prev(kcol)),
            pl.BlockSpec((tq, KV_WIDTH_B), lambda b, j: (b * nb + j, vcol)),
            pl.BlockSpec((WINDOW, KV_WIDTH_B), prev(vcol)),
        ],
        out_specs=pl.BlockSpec((tq, WIDTH_B), lambda b, j: (b * nb + j, 0)),
        out_shape=jax.ShapeDtypeStruct((batch * seq, WIDTH_B), BF16),
        compiler_params=_cparams(("arbitrary", "arbitrary")),
        name="swa_prompt",
    )(sinks, proj, proj, proj, proj, proj)


def _gdn_sample_kernel(xc_ref, z_ref, ba_ref, cs_ref, s0_ref, cw_ref, hp_ref, gnw_ref, acc_ref,
                       o_ref, s1_ref, lhs_scr, u_scr, kg_scr, res_scr, gl_scr, *, steps, group):
    del acc_ref
    T = steps
    GB = group
    H = N_HEADS_A
    HD = HEAD_DIM_A
    R = 2 * T

    u_scr[...] = jnp.zeros(u_scr.shape, F32)
    kg_scr[...] = jnp.zeros(kg_scr.shape, F32)

    xp = [cs_ref[i] for i in range(CONV_WIDTH - 1)] + [xc_ref[t] for t in range(T)]
    xs = []
    for t in range(T):
        acc = xp[t] * cw_ref[0:1, :]
        for i in range(1, CONV_WIDTH):
            acc = acc + xp[t + i] * cw_ref[i:i + 1, :]
        xs.append(_silu(acc))

    neg_a = -jnp.exp(hp_ref[0:1, :])
    dt_b = hp_ref[1:2, :]
    beta_all = [jax.nn.sigmoid(ba_ref[t]) for t in range(T)]
    g_all = [neg_a * _softplus(ba_ref[t] + dt_b) for t in range(T)]
    gc_all = [g_all[0]]
    for t in range(1, T):
        gc_all.append(gc_all[t - 1] + g_all[t])

    u_keep, attn_keep = [], []
    for h in range(H):
        q, k, v, kb, beta, gc = [], [], [], [], [], []
        for t in range(T):
            qh = xs[t][:, h * HD:(h + 1) * HD]
            kh = xs[t][:, WIDTH_A + h * HD:WIDTH_A + (h + 1) * HD]
            vh = xs[t][:, 2 * WIDTH_A + h * HD:2 * WIDTH_A + (h + 1) * HD]
            qh = qh * lax.rsqrt(jnp.sum(qh * qh, axis=-1, keepdims=True) + EPS) * (HD ** -0.5)
            kh = kh * lax.rsqrt(jnp.sum(kh * kh, axis=-1, keepdims=True) + EPS)
            bt = beta_all[t][:, h:h + 1]
            q.append(qh)
            k.append(kh)
            v.append(vh)
            kb.append(kh * bt)
            beta.append(bt)
            gc.append(gc_all[t][:, H + h:H + h + 1])
        u, w, attn = [], [], []
        for t in range(T):
            ut = v[t] * beta[t]
            wt = kb[t] * jnp.exp(gc[t])
            arow = []
            for s in range(t + 1):
                dec = jnp.exp(gc[t] - gc[s])
                arow.append(jnp.sum(q[t] * k[s], axis=-1, keepdims=True) * dec)
                if s < t:
                    l_ts = jnp.sum(kb[t] * k[s], axis=-1, keepdims=True) * dec
                    ut = ut - l_ts * u[s]
                    wt = wt - l_ts * w[s]
            u.append(ut)
            w.append(wt)
            attn.append(arow)
        g_last = gc[T - 1]
        for t in range(T):
            row0 = h * GB * R
            lhs_scr[pl.ds(row0 + t, GB, stride=R), :] = w[t]
            lhs_scr[pl.ds(row0 + T + t, GB, stride=R), :] = q[t] * jnp.exp(gc[t])
            u_scr[pl.ds(row0 + t, GB, stride=R), :] = u[t]
            kg_scr[pl.ds(row0 + t, GB, stride=R), :] = k[t] * jnp.exp(g_last - gc[t])
        gl_scr[h] = jnp.broadcast_to(jnp.exp(g_last), (GB, HD))
        u_keep.append(u)
        attn_keep.append(attn)

    inner = min(SAMPLE_INNER, GB)

    def body(i, carry):
        pairs = [(i * inner + g, h) for g in range(inner) for h in range(H)]
        rows = [pl.multiple_of((h * GB + b) * R, R) for b, h in pairs]
        res = [_bdot(lhs_scr[pl.ds(row, R), :], s0_ref[b, h]) for (b, h), row in zip(pairs, rows)]
        vn = []
        for row, r in zip(rows, res):
            res_scr[pl.ds(row, R), :] = r
            vn.append(u_scr[pl.ds(row, R), :] - r)
        upd = [_bdot_tn(kg_scr[pl.ds(row, R), :], x) for row, x in zip(rows, vn)]
        for (b, h), x in zip(pairs, upd):
            s1_ref[b, h] = s0_ref[b, h] * gl_scr[h, pl.ds(b, 1), :] + x
        return carry

    lax.fori_loop(0, GB // inner, body, 0)

    gnw = gnw_ref[...]
    for h in range(H):
        row0 = h * GB * R
        vn = []
        for t in range(T):
            ws_t = res_scr[pl.ds(row0 + t, GB, stride=R), :]
            vn.append(u_keep[h][t] - ws_t)
        for t in range(T):
            o_t = res_scr[pl.ds(row0 + T + t, GB, stride=R), :]
            for s in range(t + 1):
                o_t = o_t + attn_keep[h][t][s] * vn[s]
            z_t = z_ref[t][:, h * HD:(h + 1) * HD]
            o_ref[t, :, h * HD:(h + 1) * HD] = _rms(o_t, gnw) * _silu(z_t)


def _gdn_sample(proj3, cs_tm, s0, conv_w, hp, gnw, layer, acc):
    steps, nseq, _ = proj3.shape
    gb = min(GDN_SAMPLE_GROUP, nseq)
    rows = N_HEADS_A * gb * 2 * steps
    kern = functools.partial(_gdn_sample_kernel, steps=steps, group=gb)
    return pl.pallas_call(
        kern,
        grid=(nseq // gb,),
        input_output_aliases={8: 1},
        in_specs=[
            pl.BlockSpec((steps, gb, CONV_CH), lambda i: (0, i, OFF_XC // CONV_CH)),
            pl.BlockSpec((steps, gb, WIDTH_A), lambda i: (0, i, OFF_Z // WIDTH_A)),
            pl.BlockSpec((steps, gb, LANE), lambda i: (0, i, OFF_BA // LANE)),
            pl.BlockSpec((None, CONV_WIDTH - 1, gb, CONV_CH), lambda i: (layer, 0, i, 0)),
            pl.BlockSpec((None, gb, N_HEADS_A, HEAD_DIM_A, HEAD_DIM_A), lambda i: (layer, i, 0, 0, 0)),
            pl.BlockSpec((None, CONV_WIDTH, CONV_CH), lambda i: (layer, 0, 0)),
            pl.BlockSpec((None, 8, LANE), lambda i: (layer, 0, 0)),
            pl.BlockSpec((None, 1, HEAD_DIM_A), lambda i: (layer, 0, 0)),
            pl.BlockSpec(memory_space=pl.ANY),
        ],
        out_specs=[
            pl.BlockSpec((steps, gb, WIDTH_A), lambda i: (0, i, 0)),
            pl.BlockSpec((None, gb, N_HEADS_A, HEAD_DIM_A, HEAD_DIM_A), lambda i: (layer, i, 0, 0, 0)),
        ],
        out_shape=[
            jax.ShapeDtypeStruct((steps, nseq, WIDTH_A), F32),
            jax.ShapeDtypeStruct(s0.shape, F32),
        ],
        scratch_shapes=[
            pltpu.VMEM((rows, HEAD_DIM_A), F32),
            pltpu.VMEM((rows, HEAD_DIM_A), F32),
            pltpu.VMEM((rows, HEAD_DIM_A), F32),
            pltpu.VMEM((rows, HEAD_DIM_A), F32),
            pltpu.VMEM((N_HEADS_A, gb, HEAD_DIM_A), F32),
        ],
        compiler_params=_cparams(("arbitrary",)),
        name="gdn_sample",
    )(proj3, proj3, proj3, cs_tm, s0, conv_w, hp, gnw, acc)


def _swa_sample_kernel(sink_ref, q_ref, kc_ref, vc_ref, kn_ref, vn_ref, knt_ref, vnt_ref,
                       o_ref, ko_ref, vo_ref, *, steps, layer):
    T = steps
    W = WINDOW
    rows = T * GQA_GROUP
    t_of_row = lax.broadcasted_iota(jnp.int32, (1, rows, 1), 1) // GQA_GROUP
    kj = lax.broadcasted_iota(jnp.int32, (1, 1, W), 2)
    cmask = kj > t_of_row
    keep = lax.broadcasted_iota(jnp.int32, (1, 1, W), 2) < W - T
    for h in range(N_KV_HEADS_B):
        lo, hi = h * HEAD_DIM_B, (h + 1) * HEAD_DIM_B
        qh = q_ref[:, h]
        qh_r = qh.astype(BF16)
        kt = kc_ref[:, h]
        vt = vc_ref[:, h]
        ko_ref[:, h] = jnp.where(keep, pltpu.roll(kt, W - T, 2), knt_ref[:, h])
        vo_ref[:, h] = jnp.where(keep, pltpu.roll(vt, W - T, 2), vnt_ref[:, h])
        sc = jnp.einsum("bqd,bdk->bqk", qh_r, kt.astype(BF16),
                        preferred_element_type=F32) * (HEAD_DIM_B ** -0.5)
        sc = jnp.where(cmask, sc, NEG_BIG)
        sn = []
        for s in range(T):
            kn_s = kn_ref[:, s:s + 1, lo:hi]
            v = jnp.sum(qh * kn_s, axis=-1, keepdims=True) * (HEAD_DIM_B ** -0.5)
            sn.append(jnp.where(t_of_row >= s, v, NEG_BIG))
        sk = _sink_column(sink_ref, layer, h, (1, rows, 1), 1, 1)
        m = jnp.maximum(jnp.max(sc, axis=-1, keepdims=True), sk)
        for s in range(T):
            m = jnp.maximum(m, sn[s])
        pc = jnp.exp(sc - m)
        pn = [jnp.exp(sn[s] - m) for s in range(T)]
        den = jnp.sum(pc, axis=-1, keepdims=True) + jnp.exp(sk - m)
        for s in range(T):
            den = den + pn[s]
        inv = 1.0 / den
        o = jnp.einsum("bqk,bdk->bqd", (pc * inv).astype(BF16), vt.astype(BF16), preferred_element_type=F32)
        for s in range(T):
            o = o + (pn[s] * inv) * vn_ref[:, s:s + 1, lo:hi]
        o_ref[:, h] = o


def _swa_sample(q_r, kc_t, vc_t, kn, vn, knt, vnt, sinks, layer):
    nseq, _, rows, _ = q_r.shape
    steps = kn.shape[1]
    gb = min(SAMPLE_GROUP, nseq)
    kern = functools.partial(_swa_sample_kernel, steps=steps, layer=layer)
    cache_spec = pl.BlockSpec((None, gb, N_KV_HEADS_B, HEAD_DIM_B, WINDOW), lambda i: (layer, i, 0, 0, 0))
    win_spec = pl.BlockSpec((gb, N_KV_HEADS_B, HEAD_DIM_B, WINDOW), lambda i: (i, 0, 0, 0))
    new_spec = pl.BlockSpec((gb, steps, KV_WIDTH_B), lambda i: (i, 0, 0))
    q_spec = pl.BlockSpec((gb, N_KV_HEADS_B, rows, HEAD_DIM_B), lambda i: (i, 0, 0, 0))
    win_shape = jax.ShapeDtypeStruct((nseq, N_KV_HEADS_B, HEAD_DIM_B, WINDOW), F32)
    return pl.pallas_call(
        kern,
        grid=(nseq // gb,),
        in_specs=[pl.BlockSpec(memory_space=pltpu.SMEM), q_spec, cache_spec, cache_spec,
                  new_spec, new_spec, win_spec, win_spec],
        out_specs=[q_spec, win_spec, win_spec],
        out_shape=[jax.ShapeDtypeStruct(q_r.shape, F32), win_shape, win_shape],
        compiler_params=_cparams(("arbitrary",)),
        name="swa_sample",
    )(sinks, q_r, kc_t, vc_t, kn, vn, knt, vnt)


def _prompt_layer(x, p, layer, batch, seq):
    tm = min(TOKEN_TILE, seq)
    rep = seq // tm
    proj, tail = _premix(x, p["mod_p"], p["g_pre_mix"], p["w_in"], layer, tm, rep, conv_w=p["conv_w"])
    o_a, ssm = _gdn_prompt(proj, p["hp"], p["gdn_norm_w"], layer, batch, seq)
    o_b = _swa_prompt(proj, p["sinks"], layer, batch, seq)
    x = _ffn(x, o_a, o_b, p["mod_p"], p["g_post_mix"], p["g_pre_ffn"], p["g_post_ffn"],
             p["w_o"], p["w_up"], p["w_down"], layer, tm, rep)
    p3 = proj.reshape(batch, seq, PROJ_COLS)
    conv = tail[:, 8 - (CONV_WIDTH - 1):, :]
    k_new = p3[:, seq - WINDOW:, OFF_KB:OFF_KB + KV_WIDTH_B].reshape(batch, WINDOW, N_KV_HEADS_B, HEAD_DIM_B)
    v_new = p3[:, seq - WINDOW:, OFF_VB:OFF_VB + KV_WIDTH_B].reshape(batch, WINDOW, N_KV_HEADS_B, HEAD_DIM_B)
    return x, conv, ssm, k_new, v_new


def _sample_layer(x, p, layer, nseq, steps, ssm_acc):
    proj = _premix(x, p["mod_s"], p["g_pre_mix"], p["w_in"], layer, steps * nseq, steps)
    p3 = proj.reshape(steps, nseq, PROJ_COLS)
    o_a, ssm = _gdn_sample(p3, p["conv_tm"], p["state_ssm"], p["conv_w"], p["hp"], p["gdn_norm_w"], layer,
                           acc=ssm_acc)
    q_r = p3[:, :, OFF_QB:OFF_QB + WIDTH_B].reshape(steps, nseq, N_KV_HEADS_B, GQA_GROUP, HEAD_DIM_B)
    q_r = jnp.transpose(q_r, (1, 2, 0, 3, 4)).reshape(nseq, N_KV_HEADS_B, steps * GQA_GROUP, HEAD_DIM_B)
    k_tm = p3[:, :, OFF_KB:OFF_KB + KV_WIDTH_B]
    v_tm = p3[:, :, OFF_VB:OFF_VB + KV_WIDTH_B]

    def tail_lanes(a):
        a = jnp.transpose(a.reshape(steps, nseq, N_KV_HEADS_B, HEAD_DIM_B), (1, 2, 3, 0))
        return jnp.pad(a, ((0, 0), (0, 0), (0, 0), (WINDOW - steps, 0)))

    o_b, k_win, v_win = _swa_sample(q_r, p["cache_kt"], p["cache_vt"], jnp.transpose(k_tm, (1, 0, 2)),
                                    jnp.transpose(v_tm, (1, 0, 2)), tail_lanes(k_tm), tail_lanes(v_tm),
                                    p["sinks"], layer)
    o_b = o_b.reshape(nseq, N_KV_HEADS_B, steps, GQA_GROUP, HEAD_DIM_B)
    o_b = jnp.transpose(o_b, (2, 0, 1, 3, 4)).reshape(steps * nseq, WIDTH_B)
    x = _ffn(x, o_a.reshape(steps * nseq, WIDTH_A), o_b, p["mod_s"], p["g_post_mix"], p["g_pre_ffn"],
             p["g_post_ffn"], p["w_o"], p["w_up"], p["w_down"], layer, steps * nseq, steps)
    xp = jnp.concatenate([p["conv_tm"][layer], p3[:, :, OFF_XC:OFF_XC + CONV_CH]], axis=0)
    conv = jnp.transpose(xp[xp.shape[0] - (CONV_WIDTH - 1):], (1, 0, 2))
    return x, conv, ssm, k_win, v_win


def _pack_w_in(w_in):
    o1 = CONV_CH + WIDTH_A
    o2 = o1 + 2 * N_HEADS_A
    pad = jnp.zeros(w_in.shape[:-1] + (PROJ_COLS - OFF_BA - 2 * N_HEADS_A,), w_in.dtype)
    return jnp.concatenate([w_in[..., :o1], w_in[..., o2:], w_in[..., o1:o2], pad], axis=-1).astype(BF16)


def _head_params(a_log, dt_bias):
    depth = a_log.shape[0]
    hp = jnp.zeros((depth, 8, LANE), F32)
    hp = hp.at[:, 0, N_HEADS_A:2 * N_HEADS_A].set(a_log.astype(F32))
    hp = hp.at[:, 1, N_HEADS_A:2 * N_HEADS_A].set(dt_bias.astype(F32))
    return hp


def kernel(x_prompt, x_sample, state_conv, state_ssm, cache_swa_k, cache_swa_v, c_prompt, c_sample,
           g_pre_mix, g_post_mix, g_pre_ffn, g_post_ffn, w_ada, b_ada, w_in, conv_w, a_log, dt_bias,
           gdn_norm_w, sinks, w_o, w_up, w_down):
    depth = w_ada.shape[0]
    batch, seq, _ = x_prompt.shape
    nseq, steps, _ = x_sample.shape

    c_all = jnp.concatenate([c_sample, c_prompt], axis=0)
    rows = c_all.shape[0]
    rows_p = -(-rows // 8) * 8
    c_all = jnp.pad(c_all, ((0, rows_p - rows), (0, 0)))
    mod_all = _modulation(c_all, w_ada, b_ada)
    p = dict(
        mod_p=jnp.transpose(mod_all[:, :, nseq:nseq + batch], (0, 2, 1, 3)).reshape(depth, batch, N_MOD, 1, D_MODEL),
        mod_s=mod_all,
        g_pre_mix=g_pre_mix.reshape(depth, 1, D_MODEL), g_post_mix=g_post_mix.reshape(depth, 1, D_MODEL),
        g_pre_ffn=g_pre_ffn.reshape(depth, 1, D_MODEL), g_post_ffn=g_post_ffn.reshape(depth, 1, D_MODEL),
        w_in=_pack_w_in(w_in), conv_w=conv_w, hp=_head_params(a_log, dt_bias),
        gdn_norm_w=gdn_norm_w.reshape(depth, 1, HEAD_DIM_A), sinks=sinks,
        w_o=w_o.astype(BF16), w_up=w_up.astype(BF16), w_down=w_down.astype(BF16),
        conv_tm=jnp.transpose(state_conv, (0, 2, 1, 3)), state_ssm=state_ssm,
        cache_kt=jnp.transpose(cache_swa_k, (0, 1, 3, 4, 2)),
        cache_vt=jnp.transpose(cache_swa_v, (0, 1, 3, 4, 2)),
    )

    yp = x_prompt.reshape(batch * seq, D_MODEL)
    ys = jnp.transpose(x_sample, (1, 0, 2)).reshape(steps * nseq, D_MODEL)
    outs_p, outs_s = [], []
    ssm_s = jnp.zeros(state_ssm.shape, F32)
    for l in range(depth):
        yp, c1, s1, k1, v1 = _prompt_layer(yp, p, l, batch, seq)
        ys, c2, ssm_s, k2, v2 = _sample_layer(ys, p, l, nseq, steps, ssm_s)
        outs_p.append((c1, s1, k1, v1))
        outs_s.append((c2, None, k2, v2))

    y_prompt = yp.reshape(batch, seq, D_MODEL)
    y_sample = jnp.transpose(ys.reshape(steps, nseq, D_MODEL), (1, 0, 2))
    stack = lambda outs, i: jnp.stack([o[i] for o in outs])
    k_s = jnp.transpose(stack(outs_s, 2), (0, 1, 4, 2, 3))
    v_s = jnp.transpose(stack(outs_s, 3), (0, 1, 4, 2, 3))
    return (y_prompt, y_sample,
            stack(outs_p, 0), stack(outs_p, 1), stack(outs_p, 2), stack(outs_p, 3),
            stack(outs_s, 0), ssm_s, k_s, v_s)
```

```python
import functools

import jax
import jax.numpy as jnp
from jax import lax
from jax.experimental import pallas as pl
from jax.experimental.pallas import tpu as pltpu

F32 = jnp.float32
BF16 = jnp.bfloat16

D_MODEL = 1024
N_MOD = 6
N_HEADS_A = 4
HEAD_DIM_A = 128
WIDTH_A = N_HEADS_A * HEAD_DIM_A
CONV_WIDTH = 4
CONV_CH = 3 * WIDTH_A
N_Q_HEADS_B = 8
N_KV_HEADS_B = 2
GQA_GROUP = N_Q_HEADS_B // N_KV_HEADS_B
HEAD_DIM_B = 64
WIDTH_B = N_Q_HEADS_B * HEAD_DIM_B
KV_WIDTH_B = N_KV_HEADS_B * HEAD_DIM_B
WINDOW = 128
D_FF = 4 * D_MODEL
EPS = 1e-6
NEG_BIG = -1e30

OFF_XC = 0
OFF_Z = OFF_XC + CONV_CH
OFF_QB = OFF_Z + WIDTH_A
OFF_KB = OFF_QB + WIDTH_B
OFF_VB = OFF_KB + KV_WIDTH_B
OFF_BA = OFF_VB + KV_WIDTH_B
LANE = 128
SUBLANE = 8
PROJ_COLS = OFF_BA + LANE

GDN_CHUNK = 64
GDN_BLOCK = 512
GDN_UNROLL = 8
SWA_BLOCKS = 4
TOKEN_TILE = 512
FFN_CHUNK = 1024
MOD_COLS = 2048
SAMPLE_GROUP = 16
GDN_SAMPLE_GROUP = 32
SAMPLE_INNER = 4
VMEM_LIMIT = 56 * 1024 * 1024


def _silu(x):
    return x * jax.nn.sigmoid(x)


def _softplus(x):
    return jnp.maximum(x, 0.0) + jnp.log(1.0 + jnp.exp(-jnp.abs(x)))


def _rms(x, w):
    return x * lax.rsqrt(jnp.mean(x * x, axis=-1, keepdims=True) + EPS) * w


def _bdot(a, b):
    return jnp.dot(a.astype(BF16), b.astype(BF16), preferred_element_type=F32)


def _bdot_nt(a, b):
    return lax.dot_general(a.astype(BF16), b.astype(BF16), (((1,), (1,)), ((), ())),
                           preferred_element_type=F32)


def _bdot_tn(a, b):
    return lax.dot_general(a.astype(BF16), b.astype(BF16), (((0,), (0,)), ((), ())),
                           preferred_element_type=F32)


def _split3(x):
    hi = x.astype(BF16)
    r1 = x - hi.astype(F32)
    mid = r1.astype(BF16)
    lo = (r1 - mid.astype(F32)).astype(BF16)
    return hi, mid, lo


def _cparams(sem):
    return pltpu.CompilerParams(dimension_semantics=sem, vmem_limit_bytes=VMEM_LIMIT)


def _mod_kernel(c_ref, w_ref, b_ref, o_ref):
    s = _silu(c_ref[...]).astype(BF16)
    r = jnp.dot(s, w_ref[...].astype(BF16), preferred_element_type=F32) + b_ref[...]
    for k in range(o_ref.shape[0]):
        o_ref[k] = r[:, k * D_MODEL:(k + 1) * D_MODEL]


def _modulation(c_all, w_ada, b_ada):
    depth = w_ada.shape[0]
    rows = c_all.shape[0]
    per = MOD_COLS // D_MODEL
    return pl.pallas_call(
        _mod_kernel,
        grid=(depth, N_MOD // per),
        in_specs=[
            pl.BlockSpec((rows, D_MODEL), lambda l, j: (0, 0)),
            pl.BlockSpec((None, D_MODEL, MOD_COLS), lambda l, j: (l, 0, j)),
            pl.BlockSpec((None, 1, MOD_COLS), lambda l, j: (l, 0, j)),
        ],
        out_specs=pl.BlockSpec((None, per, rows, D_MODEL), lambda l, j: (l, j, 0, 0)),
        out_shape=jax.ShapeDtypeStruct((depth, N_MOD, rows, D_MODEL), F32),
        compiler_params=_cparams(("arbitrary", "arbitrary")),
        name="adaln_mod",
    )(c_all, w_ada, b_ada.reshape(depth, 1, N_MOD * D_MODEL))


def _mod_spec(mod, layer, tm, rep):
    if mod.ndim == 5:
        return pl.BlockSpec((None, None, N_MOD, 1, D_MODEL), lambda i: (layer, i // rep, 0, 0, 0))
    return pl.BlockSpec((None, N_MOD, tm // rep, D_MODEL), lambda i: (layer, 0, 0, 0))


def _mod_rows(mod_ref, k, tm):
    m = mod_ref[k]
    if m.shape[0] in (1, tm):
        return m
    return jnp.concatenate([m] * (tm // m.shape[0]), axis=0)


def _premix_kernel(x_ref, mod_ref, g_ref, w_ref, o_ref):
    x = x_ref[...]
    tm = x.shape[0]
    h = (_rms(x, g_ref[...]) * (1.0 + _mod_rows(mod_ref, 1, tm)) + _mod_rows(mod_ref, 0, tm)).astype(BF16)
    o_ref[...] = jnp.dot(h, w_ref[...], preferred_element_type=F32)


def _premix_conv_kernel(x_ref, mod_ref, g_ref, w_ref, cw_ref, o_ref, tail_ref, prev_scr, *, rep):
    tm = x_ref.shape[0]
    x = x_ref[...]
    h = (_rms(x, g_ref[...]) * (1.0 + _mod_rows(mod_ref, 1, tm)) + _mod_rows(mod_ref, 0, tm)).astype(BF16)

    @pl.when(pl.program_id(0) % rep == 0)
    def _():
        prev_scr[...] = jnp.zeros(prev_scr.shape, F32)

    def project(c0, c1):
        return jnp.dot(h, w_ref[:, c0:c1], preferred_element_type=F32)

    def conv_group(xc, grp):
        c0 = grp * WIDTH_A
        win = jnp.concatenate([prev_scr[:, c0:c0 + WIDTH_A], xc], axis=0)
        conv = xc * cw_ref[CONV_WIDTH - 1:CONV_WIDTH, c0:c0 + WIDTH_A]
        for s in range(1, CONV_WIDTH):
            conv = conv + pltpu.roll(win, s, 0)[SUBLANE:] * cw_ref[CONV_WIDTH - 1 - s:CONV_WIDTH - s, c0:c0 + WIDTH_A]
        xs = _silu(conv)
        for hh in range(N_HEADS_A):
            blk = xs[:, hh * HEAD_DIM_A:(hh + 1) * HEAD_DIM_A]
            if grp < 2:
                scale = lax.rsqrt(jnp.sum(blk * blk, axis=-1, keepdims=True) + EPS)
                blk = blk * (scale * (HEAD_DIM_A ** -0.5) if grp == 0 else scale)
            o_ref[:, OFF_XC + c0 + hh * HEAD_DIM_A:OFF_XC + c0 + (hh + 1) * HEAD_DIM_A] = blk
        prev_scr[:, c0:c0 + WIDTH_A] = xc[tm - SUBLANE:, :]
        tail_ref[0, :, c0:c0 + WIDTH_A] = xc[tm - SUBLANE:, :]

    xq = project(OFF_XC, OFF_XC + WIDTH_A)
    xk = project(OFF_XC + WIDTH_A, OFF_XC + 2 * WIDTH_A)
    conv_group(xq, 0)
    xv = project(OFF_XC + 2 * WIDTH_A, OFF_XC + CONV_CH)
    conv_group(xk, 1)
    o_ref[:, OFF_Z:] = project(OFF_Z, PROJ_COLS)
    conv_group(xv, 2)


def _premix(x, mod, g, w, layer, tm, rep, conv_w=None):
    n = x.shape[0]
    in_specs = [
        pl.BlockSpec((tm, D_MODEL), lambda i: (i, 0)),
        _mod_spec(mod, layer, tm, rep),
        pl.BlockSpec((None, 1, D_MODEL), lambda i: (layer, 0, 0)),
        pl.BlockSpec((None, D_MODEL, PROJ_COLS), lambda i: (layer, 0, 0)),
    ]
    proj_spec = pl.BlockSpec((tm, PROJ_COLS), lambda i: (i, 0))
    proj_shape = jax.ShapeDtypeStruct((n, PROJ_COLS), F32)
    if conv_w is None:
        return pl.pallas_call(
            _premix_kernel, grid=(n // tm,), in_specs=in_specs, out_specs=proj_spec, out_shape=proj_shape,
            compiler_params=_cparams(("arbitrary",)), name="premix_proj",
        )(x, mod, g, w)
    nseq = n // (tm * rep)
    return pl.pallas_call(
        functools.partial(_premix_conv_kernel, rep=rep),
        grid=(n // tm,),
        in_specs=in_specs + [pl.BlockSpec((None, CONV_WIDTH, CONV_CH), lambda i: (layer, 0, 0))],
        out_specs=[proj_spec, pl.BlockSpec((1, SUBLANE, CONV_CH), lambda i: (i // rep, 0, 0))],
        out_shape=[proj_shape, jax.ShapeDtypeStruct((nseq, SUBLANE, CONV_CH), F32)],
        scratch_shapes=[pltpu.VMEM((SUBLANE, CONV_CH), F32)],
        compiler_params=_cparams(("arbitrary",)),
        name="premix_conv_proj",
    )(x, mod, g, w, conv_w)


def _ffn_kernel(x_ref, oa_ref, ob_ref, mod_ref, gpm_ref, gpf_ref, gqf_ref, wo_ref, wup_ref, wdn_ref, o_ref):
    x = x_ref[...]
    o_mix = jnp.concatenate([oa_ref[...].astype(BF16), ob_ref[...].astype(BF16)], axis=1)
    mix = jnp.dot(o_mix, wo_ref[...], preferred_element_type=F32)
    tm = x.shape[0]
    x1 = x + _mod_rows(mod_ref, 2, tm) * _rms(mix, gpm_ref[...])
    h = (_rms(x1, gpf_ref[...]) * (1.0 + _mod_rows(mod_ref, 4, tm)) + _mod_rows(mod_ref, 3, tm)).astype(BF16)
    ck = FFN_CHUNK
    acc = jnp.zeros(x.shape, F32)
    for c in range(D_FF // ck):
        u = jnp.dot(h, wup_ref[:, c * ck:(c + 1) * ck], preferred_element_type=F32)
        a = jnp.square(jnp.maximum(u, 0.0)).astype(BF16)
        acc = acc + jnp.dot(a, wdn_ref[c * ck:(c + 1) * ck, :], preferred_element_type=F32)
    o_ref[...] = x1 + _mod_rows(mod_ref, 5, tm) * _rms(acc, gqf_ref[...])


def _ffn(x, oa, ob, mod, g_post_mix, g_pre_ffn, g_post_ffn, wo, wup, wdn, layer, tm, rep):
    n = x.shape[0]
    const = lambda i: (layer, 0, 0)
    return pl.pallas_call(
        _ffn_kernel,
        grid=(n // tm,),
        in_specs=[
            pl.BlockSpec((tm, D_MODEL), lambda i: (i, 0)),
            pl.BlockSpec((tm, WIDTH_A), lambda i: (i, 0)),
            pl.BlockSpec((tm, WIDTH_B), lambda i: (i, 0)),
            _mod_spec(mod, layer, tm, rep),
            pl.BlockSpec((None, 1, D_MODEL), const),
            pl.BlockSpec((None, 1, D_MODEL), const),
            pl.BlockSpec((None, 1, D_MODEL), const),
            pl.BlockSpec((None, D_MODEL, D_MODEL), const, pipeline_mode=pl.Buffered(1)),
            pl.BlockSpec((None, D_MODEL, D_FF), const, pipeline_mode=pl.Buffered(1)),
            pl.BlockSpec((None, D_FF, D_MODEL), const, pipeline_mode=pl.Buffered(1)),
        ],
        out_specs=pl.BlockSpec((tm, D_MODEL), lambda i: (i, 0)),
        out_shape=jax.ShapeDtypeStruct((n, D_MODEL), F32),
        compiler_params=_cparams(("arbitrary",)),
        name="outproj_ffn",
    )(x, oa, ob, mod, g_post_mix, g_pre_ffn, g_post_ffn, wo, wup, wdn)


def _lane_bcast(x, col):
    return jnp.broadcast_to(x[:, col:col + 1], (x.shape[0], HEAD_DIM_A))


def _gdn_prompt_kernel(xs_ref, z_ref, ba_ref, hp_ref, gnw_ref, o_ref, ssm_ref, s_scr, *, chunk, block, group):
    C = chunk
    G = group
    H = N_HEADS_A
    HD = HEAD_DIM_A
    PW = H * C
    j = pl.program_id(1)
    nj = pl.num_programs(1)

    @pl.when(j == 0)
    def _():
        s_scr[...] = jnp.zeros(s_scr.shape, F32)

    ri = lax.broadcasted_iota(jnp.int32, (C, PW), 0)
    li = lax.broadcasted_iota(jnp.int32, (C, PW), 1)
    lc = li % C
    lower = ri >= lc
    strict = ri > lc
    eye = ri == lc
    rb = lax.broadcasted_iota(jnp.int32, (PW, PW), 0) // C
    cb = lax.broadcasted_iota(jnp.int32, (PW, PW), 1) // C
    bd_mask = rb == cb
    kbd_mask = (lax.broadcasted_iota(jnp.int32, (PW, WIDTH_A), 0) // C
                == lax.broadcasted_iota(jnp.int32, (PW, WIDTH_A), 1) // HD)
    rhs_mask = (lax.broadcasted_iota(jnp.int32, (PW, 2 * WIDTH_A), 0) // C
                == (lax.broadcasted_iota(jnp.int32, (PW, 2 * WIDTH_A), 1) % WIDTH_A) // HD)
    tri = (lax.broadcasted_iota(jnp.int32, (C, C), 0)
           >= lax.broadcasted_iota(jnp.int32, (C, C), 1)).astype(BF16)
    lane128 = lax.broadcasted_iota(jnp.int32, (C, LANE), 1)

    neg_a = -jnp.exp(hp_ref[0:1, :])
    dt_b = hp_ref[1:2, :]
    gnw = gnw_ref[...]

    def bd(y):
        return jnp.where(bd_mask, jnp.concatenate([y] * H, axis=0), 0.0)

    def pmm(x, y):
        return _bdot(x, bd(y))

    def pack_cols(cols):
        per_tile = LANE // C
        tiles = []
        for t in range(H // per_tile):
            acc = cols[t * per_tile]
            for u in range(1, per_tile):
                acc = jnp.where(lane128 < u * C, acc, cols[t * per_tile + u])
            tiles.append(acc)
        return jnp.concatenate(tiles, axis=1)

    def prepare(r0):
        xs = xs_ref[pl.ds(r0, C), :]
        ba = ba_ref[pl.ds(r0, C), :]
        beta_all = jax.nn.sigmoid(ba)
        g_all = neg_a * _softplus(ba + dt_b)
        g_hi, g_mid, g_lo = _split3(g_all)
        g3 = jnp.dot(tri, jnp.concatenate([g_hi, g_mid, g_lo], axis=1), preferred_element_type=F32)
        gc_all = g3[:, 0:LANE] + g3[:, LANE:2 * LANE] + g3[:, 2 * LANE:3 * LANE]

        q, k, v, kb, beta_c, gc_c, eg_c = [], [], [], [], [], [], []
        for h in range(H):
            qh = xs[:, h * HD:(h + 1) * HD]
            kh = xs[:, WIDTH_A + h * HD:WIDTH_A + (h + 1) * HD]
            vh = xs[:, 2 * WIDTH_A + h * HD:2 * WIDTH_A + (h + 1) * HD]
            bh = _lane_bcast(beta_all, h)
            gh = _lane_bcast(gc_all, N_HEADS_A + h)
            q.append(qh)
            k.append(kh)
            v.append(vh)
            kb.append(kh * bh)
            beta_c.append(bh)
            gc_c.append(gh)
            eg_c.append(jnp.exp(gh))

        gcp = pack_cols(gc_c)
        gc_row = jnp.sum(jnp.where(eye, gcp, 0.0), axis=0, keepdims=True)
        decay = jnp.exp(jnp.where(lower, gcp - gc_row, NEG_BIG))

        k_p = jnp.concatenate(k, axis=1)
        kbd = jnp.where(kbd_mask, jnp.concatenate([k_p] * H, axis=0), 0.0)
        lhs = jnp.concatenate([jnp.concatenate(kb, axis=1), jnp.concatenate(q, axis=1)], axis=0)
        kq = _bdot_nt(lhs, kbd)
        lmat = jnp.where(strict, kq[0:C] * decay, 0.0)
        rhs = jnp.concatenate([v[h] * beta_c[h] for h in range(H)]
                              + [kb[h] * eg_c[h] for h in range(H)], axis=1)
        g_last = [gc_all[C - 1:C, N_HEADS_A + h:N_HEADS_A + h + 1] for h in range(H)]
        return dict(lmat=lmat, attn=kq[C:2 * C] * decay, rhs=rhs, g_last=g_last,
                    qg=[q[h] * eg_c[h] for h in range(H)],
                    kg=[k[h] * jnp.exp(g_last[h] - gc_c[h]) for h in range(H)])

    base_blk = min(16, C)
    same_base = (ri // base_blk) == (lc // base_blk)

    def prepare_phase(r0s, ch):
        return [lambda r0=r0: ch.append(prepare(r0)) for r0 in r0s]

    def solve_phase(ch, res):
        st = {}
        thunks = []

        def init():
            st["pw"] = [jnp.where(same_base, c["lmat"], 0.0) for c in ch]
            st["p"] = [jnp.where(eye, 1.0, 0.0) - x for x in st["pw"]]
        thunks.append(init)

        def square():
            st["pw"] = [pmm(x, x) for x in st["pw"]]

        def accumulate():
            st["p"] = [x + pmm(x, y) for x, y in zip(st["p"], st["pw"])]
        span = 2
        while span < base_blk:
            thunks += [square, accumulate]
            span *= 2
        blk = base_blk
        same = same_base
        while blk < C:
            nxt = 2 * blk
            same_n = (ri // nxt) == (lc // nxt)
            off = jnp.logical_and(same_n, jnp.logical_not(same))

            def left(off=off):
                st["t"] = [pmm(jnp.where(off, c["lmat"], 0.0), x) for c, x in zip(ch, st["p"])]

            def merge_blocks():
                st["p"] = [x - pmm(x, y) for x, y in zip(st["p"], st["t"])]
            thunks += [left, merge_blocks]
            same = same_n
            blk = nxt

        def solve():
            res["sol"] = [_bdot(x, jnp.where(rhs_mask, jnp.concatenate([c["rhs"]] * H, axis=0), 0.0))
                          for c, x in zip(ch, st["p"])]

        def attn_products():
            res["asol"] = [_bdot(c["attn"], jnp.where(rhs_mask, jnp.concatenate([x] * H, axis=0), 0.0))
                           for c, x in zip(ch, res["sol"])]

        def state_operands():
            res["ab"] = [[_bdot_tn(c["kg"][h], jnp.concatenate(
                              [x[:, WIDTH_A + h * HD:WIDTH_A + (h + 1) * HD], x[:, h * HD:(h + 1) * HD]], axis=1))
                          for h in range(H)] for c, x in zip(ch, res["sol"])]
        return thunks + [solve, attn_products, state_operands]

    def finish_phase(r0s, ch, res):
        def one(g):
            for h in range(H):
                asol = res["asol"][g]
                ab = res["ab"][g][h]
                q_eff = ch[g]["qg"][h] - asol[:, WIDTH_A + h * HD:WIDTH_A + (h + 1) * HD]
                sh = s_scr[h]
                o_h = _bdot(q_eff, sh) + asol[:, h * HD:(h + 1) * HD]
                s_scr[h] = sh * jnp.exp(ch[g]["g_last"][h]) - _bdot(ab[:, 0:HD], sh) + ab[:, HD:2 * HD]
                z_h = z_ref[pl.ds(r0s[g], C), h * HD:(h + 1) * HD]
                o_ref[pl.ds(r0s[g], C), h * HD:(h + 1) * HD] = (_rms(o_h, gnw) * _silu(z_h)).astype(o_ref.dtype)
        return [lambda g=g: one(g) for g in range(len(r0s))]

    def step(i, carry):
        r0s = [pl.multiple_of((i * G + g) * C, C) for g in range(G)]
        ch, res = [], {}
        for thunk in prepare_phase(r0s, ch) + solve_phase(ch, res) + finish_phase(r0s, ch, res):
            thunk()
        return carry

    lax.fori_loop(0, block // (G * C), step, 0)

    @pl.when(j == nj - 1)
    def _():
        ssm_ref[0] = s_scr[...]


def _gdn_prompt(proj, hp, gnw, layer, batch, seq):
    blk = min(GDN_BLOCK, seq)
    nblk = seq // blk
    chunk = min(GDN_CHUNK, seq)
    kern = functools.partial(_gdn_prompt_kernel, chunk=chunk, block=blk, group=min(GDN_UNROLL, blk // chunk))
    return pl.pallas_call(
        kern,
        grid=(batch, nblk),
        in_specs=[
            pl.BlockSpec((blk, CONV_CH), lambda b, j: (b * nblk + j, OFF_XC // CONV_CH)),
            pl.BlockSpec((blk, WIDTH_A), lambda b, j: (b * nblk + j, OFF_Z // WIDTH_A)),
            pl.BlockSpec((blk, LANE), lambda b, j: (b * nblk + j, OFF_BA // LANE)),
            pl.BlockSpec((None, SUBLANE, LANE), lambda b, j: (layer, 0, 0)),
            pl.BlockSpec((None, 1, HEAD_DIM_A), lambda b, j: (layer, 0, 0)),
        ],
        out_specs=[
            pl.BlockSpec((blk, WIDTH_A), lambda b, j: (b * nblk + j, 0)),
            pl.BlockSpec((1, N_HEADS_A, HEAD_DIM_A, HEAD_DIM_A), lambda b, j: (b, 0, 0, 0)),
        ],
        out_shape=[
            jax.ShapeDtypeStruct((batch * seq, WIDTH_A), BF16),
            jax.ShapeDtypeStruct((batch, N_HEADS_A, HEAD_DIM_A, HEAD_DIM_A), F32),
        ],
        scratch_shapes=[pltpu.VMEM((N_HEADS_A, HEAD_DIM_A, HEAD_DIM_A), F32)],
        compiler_params=_cparams(("arbitrary", "arbitrary")),
        name="gdn_prompt",
    )(proj, proj, proj, hp, gnw)


def _sink_column(sink_ref, layer, h, shape, axis, period):
    g = (lax.broadcasted_iota(jnp.int32, shape, axis) // period) % GQA_GROUP
    col = jnp.full(shape, sink_ref[layer, h * GQA_GROUP], F32)
    for gg in range(1, GQA_GROUP):
        col = jnp.where(g == gg, sink_ref[layer, h * GQA_GROUP + gg], col)
    return col


def _swa_prompt_kernel(sink_ref, q_ref, kc_ref, kp_ref, vc_ref, vp_ref, o_ref, *, layer, nsub):
    W = WINDOW
    j = pl.program_id(1)
    rows = GQA_GROUP * W
    q = (q_ref[...] * (HEAD_DIM_B ** -0.5)).astype(BF16)
    kall = jnp.concatenate([kp_ref[...], kc_ref[...]], axis=0).astype(BF16)
    vall = jnp.concatenate([vp_ref[...], vc_ref[...]], axis=0).astype(BF16)
    qi = lax.broadcasted_iota(jnp.int32, (rows, W), 0) % W
    kj = lax.broadcasted_iota(jnp.int32, (rows, W), 1)
    from_prev = kj > qi
    no_prev = from_prev & (kj > jnp.where(j == 0, -1, W))
    probs = [(sb, h) for sb in range(nsub) for h in range(N_KV_HEADS_B)]
    s = []
    for sb, h in probs:
        k_h = kall[sb * W:(sb + 2) * W, h * HEAD_DIM_B:(h + 1) * HEAD_DIM_B]
        qs = jnp.concatenate(
            [q[sb * W:(sb + 1) * W, (h * GQA_GROUP + g) * HEAD_DIM_B:(h * GQA_GROUP + g + 1) * HEAD_DIM_B]
             for g in range(GQA_GROUP)], axis=0)
        s2 = _bdot_nt(qs, k_h)
        fold = jnp.where(from_prev, s2[:, 0:W], s2[:, W:2 * W])
        s.append(jnp.where(no_prev, NEG_BIG, fold) if sb == 0 else fold)
    sk = [_sink_column(sink_ref, layer, h, (rows, 1), 0, W) for h in range(N_KV_HEADS_B)]
    m = [jnp.maximum(jnp.max(x, axis=-1, keepdims=True), sk[h]) for x, (sb, h) in zip(s, probs)]
    p = [jnp.exp(x - y) for x, y in zip(s, m)]
    inv = [1.0 / (jnp.sum(x, axis=-1, keepdims=True) + jnp.exp(sk[h] - y))
           for x, y, (sb, h) in zip(p, m, probs)]
    for x, r, (sb, h) in zip(p, inv, probs):
        v_h = vall[sb * W:(sb + 2) * W, h * HEAD_DIM_B:(h + 1) * HEAD_DIM_B]
        x2 = jnp.concatenate([jnp.where(from_prev, x, 0.0).astype(BF16),
                              jnp.where(from_prev, 0.0, x).astype(BF16)], axis=1)
        o = jnp.dot(x2, v_h, preferred_element_type=F32) * r
        for g in range(GQA_GROUP):
            c0 = (h * GQA_GROUP + g) * HEAD_DIM_B
            o_ref[sb * W:(sb + 1) * W, c0:c0 + HEAD_DIM_B] = o[g * W:(g + 1) * W, :].astype(o_ref.dtype)


def _swa_prompt(proj, sinks, layer, batch, seq):
    nsub = min(SWA_BLOCKS, seq // WINDOW)
    tq = nsub * WINDOW
    nb = seq // tq
    kcol = OFF_KB // KV_WIDTH_B
    vcol = OFF_VB // KV_WIDTH_B
    prev = lambda col: (lambda b, j: ((b * nb + j) * nsub - jnp.where(j > 0, 1, 0), col))
    return pl.pallas_call(
        functools.partial(_swa_prompt_kernel, layer=layer, nsub=nsub),
        grid=(batch, nb),
        in_specs=[
            pl.BlockSpec(memory_space=pltpu.SMEM),
            pl.BlockSpec((tq, WIDTH_B), lambda b, j: (b * nb + j, OFF_QB // WIDTH_B)),
            pl.BlockSpec((tq, KV_WIDTH_B), lambda b, j: (b * nb + j, kcol)),
            pl.BlockSpec((WINDOW, KV_WIDTH_B), prev(kcol)),
            pl.BlockSpec((tq, KV_WIDTH_B), lambda b, j: (b * nb + j, vcol)),
            pl.BlockSpec((WINDOW, KV_WIDTH_B), prev(vcol)),
        ],
        out_specs=pl.BlockSpec((tq, WIDTH_B), lambda b, j: (b * nb + j, 0)),
        out_shape=jax.ShapeDtypeStruct((batch * seq, WIDTH_B), BF16),
        compiler_params=_cparams(("arbitrary", "arbitrary")),
        name="swa_prompt",
    )(sinks, proj, proj, proj, proj, proj)


def _gdn_sample_kernel(xc_ref, z_ref, ba_ref, cs_ref, s0_ref, cw_ref, hp_ref, gnw_ref, acc_ref,
                       o_ref, s1_ref, lhs_scr, u_scr, kg_scr, res_scr, gl_scr, *, steps, group):
    del acc_ref
    T = steps
    GB = group
    H = N_HEADS_A
    HD = HEAD_DIM_A
    R = 2 * T

    u_scr[...] = jnp.zeros(u_scr.shape, F32)
    kg_scr[...] = jnp.zeros(kg_scr.shape, F32)

    xp = [cs_ref[i] for i in range(CONV_WIDTH - 1)] + [xc_ref[t] for t in range(T)]
    xs = []
    for t in range(T):
        acc = xp[t] * cw_ref[0:1, :]
        for i in range(1, CONV_WIDTH):
            acc = acc + xp[t + i] * cw_ref[i:i + 1, :]
        xs.append(_silu(acc))

    neg_a = -jnp.exp(hp_ref[0:1, :])
    dt_b = hp_ref[1:2, :]
    beta_all = [jax.nn.sigmoid(ba_ref[t]) for t in range(T)]
    g_all = [neg_a * _softplus(ba_ref[t] + dt_b) for t in range(T)]
    gc_all = [g_all[0]]
    for t in range(1, T):
        gc_all.append(gc_all[t - 1] + g_all[t])

    u_keep, attn_keep = [], []
    for h in range(H):
        q, k, v, kb, beta, gc = [], [], [], [], [], []
        for t in range(T):
            qh = xs[t][:, h * HD:(h + 1) * HD]
            kh = xs[t][:, WIDTH_A + h * HD:WIDTH_A + (h + 1) * HD]
            vh = xs[t][:, 2 * WIDTH_A + h * HD:2 * WIDTH_A + (h + 1) * HD]
            qh = qh * lax.rsqrt(jnp.sum(qh * qh, axis=-1, keepdims=True) + EPS) * (HD ** -0.5)
            kh = kh * lax.rsqrt(jnp.sum(kh * kh, axis=-1, keepdims=True) + EPS)
            bt = beta_all[t][:, h:h + 1]
            q.append(qh)
            k.append(kh)
            v.append(vh)
            kb.append(kh * bt)
            beta.append(bt)
            gc.append(gc_all[t][:, H + h:H + h + 1])
        u, w, attn = [], [], []
        for t in range(T):
            ut = v[t] * beta[t]
            wt = kb[t] * jnp.exp(gc[t])
            arow = []
            for s in range(t + 1):
                dec = jnp.exp(gc[t] - gc[s])
                arow.append(jnp.sum(q[t] * k[s], axis=-1, keepdims=True) * dec)
                if s < t:
                    l_ts = jnp.sum(kb[t] * k[s], axis=-1, keepdims=True) * dec
                    ut = ut - l_ts * u[s]
                    wt = wt - l_ts * w[s]
            u.append(ut)
            w.append(wt)
            attn.append(arow)
        g_last = gc[T - 1]
        for t in range(T):
            row0 = h * GB * R
            lhs_scr[pl.ds(row0 + t, GB, stride=R), :] = w[t]
            lhs_scr[pl.ds(row0 + T + t, GB, stride=R), :] = q[t] * jnp.exp(gc[t])
            u_scr[pl.ds(row0 + t, GB, stride=R), :] = u[t]
            kg_scr[pl.ds(row0 + t, GB, stride=R), :] = k[t] * jnp.exp(g_last - gc[t])
        gl_scr[h] = jnp.broadcast_to(jnp.exp(g_last), (GB, HD))
        u_keep.append(u)
        attn_keep.append(attn)

    inner = min(SAMPLE_INNER, GB)

    def body(i, carry):
        pairs = [(i * inner + g, h) for g in range(inner) for h in range(H)]
        rows = [pl.multiple_of((h * GB + b) * R, R) for b, h in pairs]
        res = [_bdot(lhs_scr[pl.ds(row, R), :], s0_ref[b, h]) for (b, h), row in zip(pairs, rows)]
        vn = []
        for row, r in zip(rows, res):
            res_scr[pl.ds(row, R), :] = r
            vn.append(u_scr[pl.ds(row, R), :] - r)
        upd = [_bdot_tn(kg_scr[pl.ds(row, R), :], x) for row, x in zip(rows, vn)]
        for (b, h), x in zip(pairs, upd):
            s1_ref[b, h] = s0_ref[b, h] * gl_scr[h, pl.ds(b, 1), :] + x
        return carry

    lax.fori_loop(0, GB // inner, body, 0)

    gnw = gnw_ref[...]
    for h in range(H):
        row0 = h * GB * R
        vn = []
        for t in range(T):
            ws_t = res_scr[pl.ds(row0 + t, GB, stride=R), :]
            vn.append(u_keep[h][t] - ws_t)
        for t in range(T):
            o_t = res_scr[pl.ds(row0 + T + t, GB, stride=R), :]
            for s in range(t + 1):
                o_t = o_t + attn_keep[h][t][s] * vn[s]
            z_t = z_ref[t][:, h * HD:(h + 1) * HD]
            o_ref[t, :, h * HD:(h + 1) * HD] = _rms(o_t, gnw) * _silu(z_t)


def _gdn_sample(proj3, cs_tm, s0, conv_w, hp, gnw, layer, acc):
    steps, nseq, _ = proj3.shape
    gb = min(GDN_SAMPLE_GROUP, nseq)
    rows = N_HEADS_A * gb * 2 * steps
    kern = functools.partial(_gdn_sample_kernel, steps=steps, group=gb)
    return pl.pallas_call(
        kern,
        grid=(nseq // gb,),
        input_output_aliases={8: 1},
        in_specs=[
            pl.BlockSpec((steps, gb, CONV_CH), lambda i: (0, i, OFF_XC // CONV_CH)),
            pl.BlockSpec((steps, gb, WIDTH_A), lambda i: (0, i, OFF_Z // WIDTH_A)),
            pl.BlockSpec((steps, gb, LANE), lambda i: (0, i, OFF_BA // LANE)),
            pl.BlockSpec((None, CONV_WIDTH - 1, gb, CONV_CH), lambda i: (layer, 0, i, 0)),
            pl.BlockSpec((None, gb, N_HEADS_A, HEAD_DIM_A, HEAD_DIM_A), lambda i: (layer, i, 0, 0, 0)),
            pl.BlockSpec((None, CONV_WIDTH, CONV_CH), lambda i: (layer, 0, 0)),
            pl.BlockSpec((None, SUBLANE, LANE), lambda i: (layer, 0, 0)),
            pl.BlockSpec((None, 1, HEAD_DIM_A), lambda i: (layer, 0, 0)),
            pl.BlockSpec(memory_space=pl.ANY),
        ],
        out_specs=[
            pl.BlockSpec((steps, gb, WIDTH_A), lambda i: (0, i, 0)),
            pl.BlockSpec((None, gb, N_HEADS_A, HEAD_DIM_A, HEAD_DIM_A), lambda i: (layer, i, 0, 0, 0)),
        ],
        out_shape=[
            jax.ShapeDtypeStruct((steps, nseq, WIDTH_A), F32),
            jax.ShapeDtypeStruct(s0.shape, F32),
        ],
        scratch_shapes=[
            pltpu.VMEM((rows, HEAD_DIM_A), F32),
            pltpu.VMEM((rows, HEAD_DIM_A), F32),
            pltpu.VMEM((rows, HEAD_DIM_A), F32),
            pltpu.VMEM((rows, HEAD_DIM_A), F32),
            pltpu.VMEM((N_HEADS_A, gb, HEAD_DIM_A), F32),
        ],
        compiler_params=_cparams(("arbitrary",)),
        name="gdn_sample",
    )(proj3, proj3, proj3, cs_tm, s0, conv_w, hp, gnw, acc)


def _swa_sample_kernel(sink_ref, q_ref, kc_ref, vc_ref, kn_ref, vn_ref, knt_ref, vnt_ref,
                       o_ref, ko_ref, vo_ref, *, steps, layer):
    T = steps
    W = WINDOW
    rows = T * GQA_GROUP
    t_of_row = lax.broadcasted_iota(jnp.int32, (1, rows, 1), 1) // GQA_GROUP
    kj = lax.broadcasted_iota(jnp.int32, (1, 1, W), 2)
    cmask = kj > t_of_row
    keep = lax.broadcasted_iota(jnp.int32, (1, 1, W), 2) < W - T
    for h in range(N_KV_HEADS_B):
        lo, hi = h * HEAD_DIM_B, (h + 1) * HEAD_DIM_B
        qh = q_ref[:, h]
        qh_r = qh.astype(BF16)
        kt = kc_ref[:, h]
        vt = vc_ref[:, h]
        ko_ref[:, h] = jnp.where(keep, pltpu.roll(kt, W - T, 2), knt_ref[:, h])
        vo_ref[:, h] = jnp.where(keep, pltpu.roll(vt, W - T, 2), vnt_ref[:, h])
        sc = jnp.einsum("bqd,bdk->bqk", qh_r, kt.astype(BF16),
                        preferred_element_type=F32) * (HEAD_DIM_B ** -0.5)
        sc = jnp.where(cmask, sc, NEG_BIG)
        sn = []
        for s in range(T):
            kn_s = kn_ref[:, s:s + 1, lo:hi]
            v = jnp.sum(qh * kn_s, axis=-1, keepdims=True) * (HEAD_DIM_B ** -0.5)
            sn.append(jnp.where(t_of_row >= s, v, NEG_BIG))
        sk = _sink_column(sink_ref, layer, h, (1, rows, 1), 1, 1)
        m = jnp.maximum(jnp.max(sc, axis=-1, keepdims=True), sk)
        for s in range(T):
            m = jnp.maximum(m, sn[s])
        pc = jnp.exp(sc - m)
        pn = [jnp.exp(sn[s] - m) for s in range(T)]
        den = jnp.sum(pc, axis=-1, keepdims=True) + jnp.exp(sk - m)
        for s in range(T):
            den = den + pn[s]
        inv = 1.0 / den
        o = jnp.einsum("bqk,bdk->bqd", (pc * inv).astype(BF16), vt.astype(BF16), preferred_element_type=F32)
        for s in range(T):
            o = o + (pn[s] * inv) * vn_ref[:, s:s + 1, lo:hi]
        o_ref[:, h] = o


def _swa_sample(q_r, kc_t, vc_t, kn, vn, knt, vnt, sinks, layer):
    nseq, _, rows, _ = q_r.shape
    steps = kn.shape[1]
    gb = min(SAMPLE_GROUP, nseq)
    kern = functools.partial(_swa_sample_kernel, steps=steps, layer=layer)
    cache_spec = pl.BlockSpec((None, gb, N_KV_HEADS_B, HEAD_DIM_B, WINDOW), lambda i: (layer, i, 0, 0, 0))
    win_spec = pl.BlockSpec((gb, N_KV_HEADS_B, HEAD_DIM_B, WINDOW), lambda i: (i, 0, 0, 0))
    new_spec = pl.BlockSpec((gb, steps, KV_WIDTH_B), lambda i: (i, 0, 0))
    q_spec = pl.BlockSpec((gb, N_KV_HEADS_B, rows, HEAD_DIM_B), lambda i: (i, 0, 0, 0))
    win_shape = jax.ShapeDtypeStruct((nseq, N_KV_HEADS_B, HEAD_DIM_B, WINDOW), F32)
    return pl.pallas_call(
        kern,
        grid=(nseq // gb,),
        in_specs=[pl.BlockSpec(memory_space=pltpu.SMEM), q_spec, cache_spec, cache_spec,
                  new_spec, new_spec, win_spec, win_spec],
        out_specs=[q_spec, win_spec, win_spec],
        out_shape=[jax.ShapeDtypeStruct(q_r.shape, F32), win_shape, win_shape],
        compiler_params=_cparams(("arbitrary",)),
        name="swa_sample",
    )(sinks, q_r, kc_t, vc_t, kn, vn, knt, vnt)


def _prompt_layer(x, p, layer, batch, seq):
    tm = min(TOKEN_TILE, seq)
    rep = seq // tm
    proj, tail = _premix(x, p["mod_p"], p["g_pre_mix"], p["w_in"], layer, tm, rep, conv_w=p["conv_w"])
    o_a, ssm = _gdn_prompt(proj, p["hp"], p["gdn_norm_w"], layer, batch, seq)
    o_b = _swa_prompt(proj, p["sinks"], layer, batch, seq)
    x = _ffn(x, o_a, o_b, p["mod_p"], p["g_post_mix"], p["g_pre_ffn"], p["g_post_ffn"],
             p["w_o"], p["w_up"], p["w_down"], layer, tm, rep)
    p3 = proj.reshape(batch, seq, PROJ_COLS)
    conv = tail[:, SUBLANE - (CONV_WIDTH - 1):, :]
    k_new = p3[:, seq - WINDOW:, OFF_KB:OFF_KB + KV_WIDTH_B].reshape(batch, WINDOW, N_KV_HEADS_B, HEAD_DIM_B)
    v_new = p3[:, seq - WINDOW:, OFF_VB:OFF_VB + KV_WIDTH_B].reshape(batch, WINDOW, N_KV_HEADS_B, HEAD_DIM_B)
    return x, conv, ssm, k_new, v_new


def _sample_layer(x, p, layer, nseq, steps, ssm_acc):
    proj = _premix(x, p["mod_s"], p["g_pre_mix"], p["w_in"], layer, steps * nseq, steps)
    p3 = proj.reshape(steps, nseq, PROJ_COLS)
    o_a, ssm = _gdn_sample(p3, p["conv_tm"], p["state_ssm"], p["conv_w"], p["hp"], p["gdn_norm_w"], layer,
                           acc=ssm_acc)
    q_r = p3[:, :, OFF_QB:OFF_QB + WIDTH_B].reshape(steps, nseq, N_KV_HEADS_B, GQA_GROUP, HEAD_DIM_B)
    q_r = jnp.transpose(q_r, (1, 2, 0, 3, 4)).reshape(nseq, N_KV_HEADS_B, steps * GQA_GROUP, HEAD_DIM_B)
    k_tm = p3[:, :, OFF_KB:OFF_KB + KV_WIDTH_B]
    v_tm = p3[:, :, OFF_VB:OFF_VB + KV_WIDTH_B]

    def tail_lanes(a):
        a = jnp.transpose(a.reshape(steps, nseq, N_KV_HEADS_B, HEAD_DIM_B), (1, 2, 3, 0))
        return jnp.pad(a, ((0, 0), (0, 0), (0, 0), (WINDOW - steps, 0)))

    o_b, k_win, v_win = _swa_sample(q_r, p["cache_kt"], p["cache_vt"], jnp.transpose(k_tm, (1, 0, 2)),
                                    jnp.transpose(v_tm, (1, 0, 2)), tail_lanes(k_tm), tail_lanes(v_tm),
                                    p["sinks"], layer)
    o_b = o_b.reshape(nseq, N_KV_HEADS_B, steps, GQA_GROUP, HEAD_DIM_B)
    o_b = jnp.transpose(o_b, (2, 0, 1, 3, 4)).reshape(steps * nseq, WIDTH_B)
    x = _ffn(x, o_a.reshape(steps * nseq, WIDTH_A), o_b, p["mod_s"], p["g_post_mix"], p["g_pre_ffn"],
             p["g_post_ffn"], p["w_o"], p["w_up"], p["w_down"], layer, steps * nseq, steps)
    xp = jnp.concatenate([p["conv_tm"][layer], p3[:, :, OFF_XC:OFF_XC + CONV_CH]], axis=0)
    conv = jnp.transpose(xp[xp.shape[0] - (CONV_WIDTH - 1):], (1, 0, 2))
    return x, conv, ssm, k_win, v_win


def _pack_w_in(w_in):
    o1 = CONV_CH + WIDTH_A
    o2 = o1 + 2 * N_HEADS_A
    pad = jnp.zeros(w_in.shape[:-1] + (PROJ_COLS - OFF_BA - 2 * N_HEADS_A,), w_in.dtype)
    return jnp.concatenate([w_in[..., :o1], w_in[..., o2:], w_in[..., o1:o2], pad], axis=-1).astype(BF16)


def _head_params(a_log, dt_bias):
    depth = a_log.shape[0]
    hp = jnp.zeros((depth, SUBLANE, LANE), F32)
    hp = hp.at[:, 0, N_HEADS_A:2 * N_HEADS_A].set(a_log.astype(F32))
    hp = hp.at[:, 1, N_HEADS_A:2 * N_HEADS_A].set(dt_bias.astype(F32))
    return hp


def kernel(x_prompt, x_sample, state_conv, state_ssm, cache_swa_k, cache_swa_v, c_prompt, c_sample,
           g_pre_mix, g_post_mix, g_pre_ffn, g_post_ffn, w_ada, b_ada, w_in, conv_w, a_log, dt_bias,
           gdn_norm_w, sinks, w_o, w_up, w_down):
    depth = w_ada.shape[0]
    batch, seq, _ = x_prompt.shape
    nseq, steps, _ = x_sample.shape

    c_all = jnp.concatenate([c_sample, c_prompt], axis=0)
    rows = c_all.shape[0]
    rows_p = -(-rows // 8) * 8
    c_all = jnp.pad(c_all, ((0, rows_p - rows), (0, 0)))
    mod_all = _modulation(c_all, w_ada, b_ada)
    p = dict(
        mod_p=jnp.transpose(mod_all[:, :, nseq:nseq + batch], (0, 2, 1, 3)).reshape(depth, batch, N_MOD, 1, D_MODEL),
        mod_s=mod_all,
        g_pre_mix=g_pre_mix.reshape(depth, 1, D_MODEL), g_post_mix=g_post_mix.reshape(depth, 1, D_MODEL),
        g_pre_ffn=g_pre_ffn.reshape(depth, 1, D_MODEL), g_post_ffn=g_post_ffn.reshape(depth, 1, D_MODEL),
        w_in=_pack_w_in(w_in), conv_w=conv_w, hp=_head_params(a_log, dt_bias),
        gdn_norm_w=gdn_norm_w.reshape(depth, 1, HEAD_DIM_A), sinks=sinks,
        w_o=w_o.astype(BF16), w_up=w_up.astype(BF16), w_down=w_down.astype(BF16),
        conv_tm=jnp.transpose(state_conv, (0, 2, 1, 3)), state_ssm=state_ssm,
        cache_kt=jnp.transpose(cache_swa_k, (0, 1, 3, 4, 2)),
        cache_vt=jnp.transpose(cache_swa_v, (0, 1, 3, 4, 2)),
    )

    yp = x_prompt.reshape(batch * seq, D_MODEL)
    ys = jnp.transpose(x_sample, (1, 0, 2)).reshape(steps * nseq, D_MODEL)
    outs_p, outs_s = [], []
    ssm_s = jnp.zeros(state_ssm.shape, F32)
    for l in range(depth):
        yp, c1, s1, k1, v1 = _prompt_layer(yp, p, l, batch, seq)
        ys, c2, ssm_s, k2, v2 = _sample_layer(ys, p, l, nseq, steps, ssm_s)
        outs_p.append((c1, s1, k1, v1))
        outs_s.append((c2, None, k2, v2))

    y_prompt = yp.reshape(batch, seq, D_MODEL)
    y_sample = jnp.transpose(ys.reshape(steps, nseq, D_MODEL), (1, 0, 2))
    stack = lambda outs, i: jnp.stack([o[i] for o in outs])
    k_s = jnp.transpose(stack(outs_s, 2), (0, 1, 4, 2, 3))
    v_s = jnp.transpose(stack(outs_s, 3), (0, 1, 4, 2, 3))
    return (y_prompt, y_sample,
            stack(outs_p, 0), stack(outs_p, 1), stack(outs_p, 2), stack(outs_p, 3),
            stack(outs_s, 0), ssm_s, k_s, v_s)
```

```python
import functools

import jax
import jax.numpy as jnp
from jax import lax
from jax.experimental import pallas as pl
from jax.experimental.pallas import tpu as pltpu

F32 = jnp.float32
BF16 = jnp.bfloat16

D_MODEL = 1024
N_MOD = 6
N_HEADS_A = 4
HEAD_DIM_A = 128
WIDTH_A = N_HEADS_A * HEAD_DIM_A
CONV_WIDTH = 4
CONV_CH = 3 * WIDTH_A
N_Q_HEADS_B = 8
N_KV_HEADS_B = 2
GQA_GROUP = N_Q_HEADS_B // N_KV_HEADS_B
HEAD_DIM_B = 64
WIDTH_B = N_Q_HEADS_B * HEAD_DIM_B
KV_WIDTH_B = N_KV_HEADS_B * HEAD_DIM_B
WINDOW = 128
D_FF = 4 * D_MODEL
EPS = 1e-6
NEG_BIG = -1e30

OFF_XC = 0
OFF_Z = OFF_XC + CONV_CH
OFF_QB = OFF_Z + WIDTH_A
OFF_KB = OFF_QB + WIDTH_B
OFF_VB = OFF_KB + KV_WIDTH_B
OFF_BA = OFF_VB + KV_WIDTH_B
LANE = 128
SUBLANE = 8
PROJ_COLS = OFF_BA + LANE

GDN_CHUNK = 64
GDN_BLOCK = 1024
GDN_UNROLL = 8
SWA_BLOCKS = 8
TOKEN_TILE = 512
FFN_CHUNK = 1024
MOD_COLS = 2048
SAMPLE_GROUP = 32
GDN_SAMPLE_GROUP = 32
SAMPLE_INNER = 4
VMEM_LIMIT = 56 * 1024 * 1024


def _silu(x):
    return x * jax.nn.sigmoid(x)


def _softplus(x):
    return jnp.maximum(x, 0.0) + jnp.log(1.0 + jnp.exp(-jnp.abs(x)))


def _rms(x, w):
    return x * lax.rsqrt(jnp.mean(x * x, axis=-1, keepdims=True) + EPS) * w


def _bdot(a, b):
    return jnp.dot(a.astype(BF16), b.astype(BF16), preferred_element_type=F32)


def _bdot_nt(a, b):
    return lax.dot_general(a.astype(BF16), b.astype(BF16), (((1,), (1,)), ((), ())),
                           preferred_element_type=F32)


def _bdot_tn(a, b):
    return lax.dot_general(a.astype(BF16), b.astype(BF16), (((0,), (0,)), ((), ())),
                           preferred_element_type=F32)


def _split3(x):
    hi = x.astype(BF16)
    r1 = x - hi.astype(F32)
    mid = r1.astype(BF16)
    lo = (r1 - mid.astype(F32)).astype(BF16)
    return hi, mid, lo


def _cparams(sem):
    return pltpu.CompilerParams(dimension_semantics=sem, vmem_limit_bytes=VMEM_LIMIT)


def _mod_kernel(c_ref, w_ref, b_ref, o_ref):
    s = _silu(c_ref[...]).astype(BF16)
    r = jnp.dot(s, w_ref[...].astype(BF16), preferred_element_type=F32) + b_ref[...]
    for k in range(o_ref.shape[0]):
        o_ref[k] = r[:, k * D_MODEL:(k + 1) * D_MODEL]


def _modulation(c_all, w_ada, b_ada):
    depth = w_ada.shape[0]
    rows = c_all.shape[0]
    per = MOD_COLS // D_MODEL
    return pl.pallas_call(
        _mod_kernel,
        grid=(depth, N_MOD // per),
        in_specs=[
            pl.BlockSpec((rows, D_MODEL), lambda l, j: (0, 0)),
            pl.BlockSpec((None, D_MODEL, MOD_COLS), lambda l, j: (l, 0, j)),
            pl.BlockSpec((None, 1, MOD_COLS), lambda l, j: (l, 0, j)),
        ],
        out_specs=pl.BlockSpec((None, per, rows, D_MODEL), lambda l, j: (l, j, 0, 0)),
        out_shape=jax.ShapeDtypeStruct((depth, N_MOD, rows, D_MODEL), F32),
        compiler_params=_cparams(("arbitrary", "arbitrary")),
        name="adaln_mod",
    )(c_all, w_ada, b_ada.reshape(depth, 1, N_MOD * D_MODEL))


def _mod_spec(mod, layer, tm, rep):
    if mod.ndim == 5:
        return pl.BlockSpec((None, None, N_MOD, 1, D_MODEL), lambda i: (layer, i // rep, 0, 0, 0))
    return pl.BlockSpec((None, N_MOD, tm // rep, D_MODEL), lambda i: (layer, 0, 0, 0))


def _mod_rows(mod_ref, k, tm):
    m = mod_ref[k]
    if m.shape[0] in (1, tm):
        return m
    return jnp.concatenate([m] * (tm // m.shape[0]), axis=0)


def _premix_kernel(x_ref, mod_ref, g_ref, w_ref, o_ref):
    x = x_ref[...]
    tm = x.shape[0]
    h = (_rms(x, g_ref[...]) * (1.0 + _mod_rows(mod_ref, 1, tm)) + _mod_rows(mod_ref, 0, tm)).astype(BF16)
    o_ref[...] = jnp.dot(h, w_ref[...], preferred_element_type=F32)


def _premix_conv_kernel(x_ref, mod_ref, g_ref, w_ref, cw_ref, o_ref, tail_ref, prev_scr, *, rep):
    tm = x_ref.shape[0]
    x = x_ref[...]
    h = (_rms(x, g_ref[...]) * (1.0 + _mod_rows(mod_ref, 1, tm)) + _mod_rows(mod_ref, 0, tm)).astype(BF16)

    @pl.when(pl.program_id(0) % rep == 0)
    def _():
        prev_scr[...] = jnp.zeros(prev_scr.shape, F32)

    def project(c0, c1):
        return jnp.dot(h, w_ref[:, c0:c1], preferred_element_type=F32)

    def conv_group(xc, grp):
        c0 = grp * WIDTH_A
        win = jnp.concatenate([prev_scr[:, c0:c0 + WIDTH_A], xc], axis=0)
        conv = xc * cw_ref[CONV_WIDTH - 1:CONV_WIDTH, c0:c0 + WIDTH_A]
        for s in range(1, CONV_WIDTH):
            conv = conv + pltpu.roll(win, s, 0)[SUBLANE:] * cw_ref[CONV_WIDTH - 1 - s:CONV_WIDTH - s, c0:c0 + WIDTH_A]
        xs = _silu(conv)
        for hh in range(N_HEADS_A):
            blk = xs[:, hh * HEAD_DIM_A:(hh + 1) * HEAD_DIM_A]
            if grp < 2:
                scale = lax.rsqrt(jnp.sum(blk * blk, axis=-1, keepdims=True) + EPS)
                blk = blk * (scale * (HEAD_DIM_A ** -0.5) if grp == 0 else scale)
            o_ref[:, OFF_XC + c0 + hh * HEAD_DIM_A:OFF_XC + c0 + (hh + 1) * HEAD_DIM_A] = blk
        prev_scr[:, c0:c0 + WIDTH_A] = xc[tm - SUBLANE:, :]
        tail_ref[0, :, c0:c0 + WIDTH_A] = xc[tm - SUBLANE:, :]

    xq = project(OFF_XC, OFF_XC + WIDTH_A)
    xk = project(OFF_XC + WIDTH_A, OFF_XC + 2 * WIDTH_A)
    conv_group(xq, 0)
    xv = project(OFF_XC + 2 * WIDTH_A, OFF_XC + CONV_CH)
    conv_group(xk, 1)
    o_ref[:, OFF_Z:] = project(OFF_Z, PROJ_COLS)
    conv_group(xv, 2)


def _premix(x, mod, g, w, layer, tm, rep, conv_w=None):
    n = x.shape[0]
    in_specs = [
        pl.BlockSpec((tm, D_MODEL), lambda i: (i, 0)),
        _mod_spec(mod, layer, tm, rep),
        pl.BlockSpec((None, 1, D_MODEL), lambda i: (layer, 0, 0)),
        pl.BlockSpec((None, D_MODEL, PROJ_COLS), lambda i: (layer, 0, 0)),
    ]
    proj_spec = pl.BlockSpec((tm, PROJ_COLS), lambda i: (i, 0))
    proj_shape = jax.ShapeDtypeStruct((n, PROJ_COLS), F32)
    if conv_w is None:
        return pl.pallas_call(
            _premix_kernel, grid=(n // tm,), in_specs=in_specs, out_specs=proj_spec, out_shape=proj_shape,
            compiler_params=_cparams(("arbitrary",)), name="premix_proj",
        )(x, mod, g, w)
    nseq = n // (tm * rep)
    return pl.pallas_call(
        functools.partial(_premix_conv_kernel, rep=rep),
        grid=(n // tm,),
        in_specs=in_specs + [pl.BlockSpec((None, CONV_WIDTH, CONV_CH), lambda i: (layer, 0, 0))],
        out_specs=[proj_spec, pl.BlockSpec((1, SUBLANE, CONV_CH), lambda i: (i // rep, 0, 0))],
        out_shape=[proj_shape, jax.ShapeDtypeStruct((nseq, SUBLANE, CONV_CH), F32)],
        scratch_shapes=[pltpu.VMEM((SUBLANE, CONV_CH), F32)],
        compiler_params=_cparams(("arbitrary",)),
        name="premix_conv_proj",
    )(x, mod, g, w, conv_w)


def _ffn_kernel(x_ref, oa_ref, ob_ref, mod_ref, gpm_ref, gpf_ref, gqf_ref, wo_ref, wup_ref, wdn_ref, o_ref):
    x = x_ref[...]
    o_mix = jnp.concatenate([oa_ref[...].astype(BF16), ob_ref[...].astype(BF16)], axis=1)
    mix = jnp.dot(o_mix, wo_ref[...], preferred_element_type=F32)
    tm = x.shape[0]
    x1 = x + _mod_rows(mod_ref, 2, tm) * _rms(mix, gpm_ref[...])
    h = (_rms(x1, gpf_ref[...]) * (1.0 + _mod_rows(mod_ref, 4, tm)) + _mod_rows(mod_ref, 3, tm)).astype(BF16)
    ck = FFN_CHUNK
    acc = jnp.zeros(x.shape, F32)
    for c in range(D_FF // ck):
        u = jnp.dot(h, wup_ref[:, c * ck:(c + 1) * ck], preferred_element_type=F32)
        a = jnp.square(jnp.maximum(u, 0.0)).astype(BF16)
        acc = acc + jnp.dot(a, wdn_ref[c * ck:(c + 1) * ck, :], preferred_element_type=F32)
    o_ref[...] = x1 + _mod_rows(mod_ref, 5, tm) * _rms(acc, gqf_ref[...])


def _ffn(x, oa, ob, mod, g_post_mix, g_pre_ffn, g_post_ffn, wo, wup, wdn, layer, tm, rep):
    n = x.shape[0]
    const = lambda i: (layer, 0, 0)
    return pl.pallas_call(
        _ffn_kernel,
        grid=(n // tm,),
        in_specs=[
            pl.BlockSpec((tm, D_MODEL), lambda i: (i, 0)),
            pl.BlockSpec((tm, WIDTH_A), lambda i: (i, 0)),
            pl.BlockSpec((tm, WIDTH_B), lambda i: (i, 0)),
            _mod_spec(mod, layer, tm, rep),
            pl.BlockSpec((None, 1, D_MODEL), const),
            pl.BlockSpec((None, 1, D_MODEL), const),
            pl.BlockSpec((None, 1, D_MODEL), const),
            pl.BlockSpec((None, D_MODEL, D_MODEL), const, pipeline_mode=pl.Buffered(1)),
            pl.BlockSpec((None, D_MODEL, D_FF), const, pipeline_mode=pl.Buffered(1)),
            pl.BlockSpec((None, D_FF, D_MODEL), const, pipeline_mode=pl.Buffered(1)),
        ],
        out_specs=pl.BlockSpec((tm, D_MODEL), lambda i: (i, 0)),
        out_shape=jax.ShapeDtypeStruct((n, D_MODEL), F32),
        compiler_params=_cparams(("arbitrary",)),
        name="outproj_ffn",
    )(x, oa, ob, mod, g_post_mix, g_pre_ffn, g_post_ffn, wo, wup, wdn)


def _lane_bcast(x, col):
    return jnp.broadcast_to(x[:, col:col + 1], (x.shape[0], HEAD_DIM_A))


def _gdn_prompt_kernel(xs_ref, z_ref, ba_ref, hp_ref, gnw_ref, o_ref, ssm_ref, s_scr, *, chunk, block, group):
    C = chunk
    G = group
    H = N_HEADS_A
    HD = HEAD_DIM_A
    PW = H * C
    j = pl.program_id(1)
    nj = pl.num_programs(1)

    @pl.when(j == 0)
    def _():
        s_scr[...] = jnp.zeros(s_scr.shape, F32)

    ri = lax.broadcasted_iota(jnp.int32, (C, PW), 0)
    li = lax.broadcasted_iota(jnp.int32, (C, PW), 1)
    lc = li % C
    lower = ri >= lc
    strict = ri > lc
    eye = ri == lc
    rb = lax.broadcasted_iota(jnp.int32, (PW, PW), 0) // C
    cb = lax.broadcasted_iota(jnp.int32, (PW, PW), 1) // C
    bd_mask = rb == cb
    kbd_mask = (lax.broadcasted_iota(jnp.int32, (PW, WIDTH_A), 0) // C
                == lax.broadcasted_iota(jnp.int32, (PW, WIDTH_A), 1) // HD)
    rhs_mask = (lax.broadcasted_iota(jnp.int32, (PW, 2 * WIDTH_A), 0) // C
                == (lax.broadcasted_iota(jnp.int32, (PW, 2 * WIDTH_A), 1) % WIDTH_A) // HD)
    tri = (lax.broadcasted_iota(jnp.int32, (C, C), 0)
           >= lax.broadcasted_iota(jnp.int32, (C, C), 1)).astype(BF16)
    lane128 = lax.broadcasted_iota(jnp.int32, (C, LANE), 1)

    neg_a = -jnp.exp(hp_ref[0:1, :])
    dt_b = hp_ref[1:2, :]
    gnw = gnw_ref[...]

    def bd(y):
        return jnp.where(bd_mask, jnp.concatenate([y] * H, axis=0), 0.0)

    def pmm(x, y):
        return _bdot(x, bd(y))

    def pack_cols(cols):
        per_tile = LANE // C
        tiles = []
        for t in range(H // per_tile):
            acc = cols[t * per_tile]
            for u in range(1, per_tile):
                acc = jnp.where(lane128 < u * C, acc, cols[t * per_tile + u])
            tiles.append(acc)
        return jnp.concatenate(tiles, axis=1)

    def prepare(r0):
        xs = xs_ref[pl.ds(r0, C), :]
        ba = ba_ref[pl.ds(r0, C), :]
        beta_all = jax.nn.sigmoid(ba)
        g_all = neg_a * _softplus(ba + dt_b)
        g_hi, g_mid, g_lo = _split3(g_all)
        g3 = jnp.dot(tri, jnp.concatenate([g_hi, g_mid, g_lo], axis=1), preferred_element_type=F32)
        gc_all = g3[:, 0:LANE] + g3[:, LANE:2 * LANE] + g3[:, 2 * LANE:3 * LANE]

        q, k, v, kb, beta_c, gc_c, eg_c = [], [], [], [], [], [], []
        for h in range(H):
            qh = xs[:, h * HD:(h + 1) * HD]
            kh = xs[:, WIDTH_A + h * HD:WIDTH_A + (h + 1) * HD]
            vh = xs[:, 2 * WIDTH_A + h * HD:2 * WIDTH_A + (h + 1) * HD]
            bh = _lane_bcast(beta_all, h)
            gh = _lane_bcast(gc_all, N_HEADS_A + h)
            q.append(qh)
            k.append(kh)
            v.append(vh)
            kb.append(kh * bh)
            beta_c.append(bh)
            gc_c.append(gh)
            eg_c.append(jnp.exp(gh))

        gcp = pack_cols(gc_c)
        gc_row = jnp.sum(jnp.where(eye, gcp, 0.0), axis=0, keepdims=True)
        decay = jnp.exp(jnp.where(lower, gcp - gc_row, NEG_BIG))

        k_p = jnp.concatenate(k, axis=1)
        kbd = jnp.where(kbd_mask, jnp.concatenate([k_p] * H, axis=0), 0.0)
        lhs = jnp.concatenate([jnp.concatenate(kb, axis=1), jnp.concatenate(q, axis=1)], axis=0)
        kq = _bdot_nt(lhs, kbd)
        lmat = jnp.where(strict, kq[0:C] * decay, 0.0)
        rhs = jnp.concatenate([v[h] * beta_c[h] for h in range(H)]
                              + [kb[h] * eg_c[h] for h in range(H)], axis=1)
        g_last = [gc_all[C - 1:C, N_HEADS_A + h:N_HEADS_A + h + 1] for h in range(H)]
        return dict(lmat=lmat, attn=kq[C:2 * C] * decay, rhs=rhs, g_last=g_last,
                    qg=[q[h] * eg_c[h] for h in range(H)],
                    kg=[k[h] * jnp.exp(g_last[h] - gc_c[h]) for h in range(H)])

    base_blk = min(16, C)
    same_base = (ri // base_blk) == (lc // base_blk)

    def prepare_phase(r0s, ch):
        return [lambda r0=r0: ch.append(prepare(r0)) for r0 in r0s]

    def solve_phase(ch, res):
        st = {}
        thunks = []

        def init():
            st["pw"] = [jnp.where(same_base, c["lmat"], 0.0) for c in ch]
            st["p"] = [jnp.where(eye, 1.0, 0.0) - x for x in st["pw"]]
        thunks.append(init)

        def square():
            st["pw"] = [pmm(x, x) for x in st["pw"]]

        def accumulate():
            st["p"] = [x + pmm(x, y) for x, y in zip(st["p"], st["pw"])]
        span = 2
        while span < base_blk:
            thunks += [square, accumulate]
            span *= 2
        blk = base_blk
        same = same_base
        while blk < C:
            nxt = 2 * blk
            same_n = (ri // nxt) == (lc // nxt)
            off = jnp.logical_and(same_n, jnp.logical_not(same))

            def left(off=off):
                st["t"] = [pmm(jnp.where(off, c["lmat"], 0.0), x) for c, x in zip(ch, st["p"])]

            def merge_blocks():
                st["p"] = [x - pmm(x, y) for x, y in zip(st["p"], st["t"])]
            thunks += [left, merge_blocks]
            same = same_n
            blk = nxt

        def solve():
            res["sol"] = [_bdot(x, jnp.where(rhs_mask, jnp.concatenate([c["rhs"]] * H, axis=0), 0.0))
                          for c, x in zip(ch, st["p"])]

        def attn_products():
            res["asol"] = [_bdot(c["attn"], jnp.where(rhs_mask, jnp.concatenate([x] * H, axis=0), 0.0))
                           for c, x in zip(ch, res["sol"])]

        def state_operands():
            res["ab"] = [[_bdot_tn(c["kg"][h], jnp.concatenate(
                              [x[:, WIDTH_A + h * HD:WIDTH_A + (h + 1) * HD], x[:, h * HD:(h + 1) * HD]], axis=1))
                          for h in range(H)] for c, x in zip(ch, res["sol"])]
        return thunks + [solve, attn_products, state_operands]

    def finish_phase(r0s, ch, res):
        def one(g):
            for h in range(H):
                asol = res["asol"][g]
                ab = res["ab"][g][h]
                q_eff = ch[g]["qg"][h] - asol[:, WIDTH_A + h * HD:WIDTH_A + (h + 1) * HD]
                sh = s_scr[h]
                o_h = _bdot(q_eff, sh) + asol[:, h * HD:(h + 1) * HD]
                s_scr[h] = sh * jnp.exp(ch[g]["g_last"][h]) - _bdot(ab[:, 0:HD], sh) + ab[:, HD:2 * HD]
                z_h = z_ref[pl.ds(r0s[g], C), h * HD:(h + 1) * HD]
                o_ref[pl.ds(r0s[g], C), h * HD:(h + 1) * HD] = (_rms(o_h, gnw) * _silu(z_h)).astype(o_ref.dtype)
        return [lambda g=g: one(g) for g in range(len(r0s))]

    def step(i, carry):
        r0s = [pl.multiple_of((i * G + g) * C, C) for g in range(G)]
        ch, res = [], {}
        for thunk in prepare_phase(r0s, ch) + solve_phase(ch, res) + finish_phase(r0s, ch, res):
            thunk()
        return carry

    lax.fori_loop(0, block // (G * C), step, 0)

    @pl.when(j == nj - 1)
    def _():
        ssm_ref[0] = s_scr[...]


def _gdn_prompt(proj, hp, gnw, layer, batch, seq):
    blk = min(GDN_BLOCK, seq)
    nblk = seq // blk
    chunk = min(GDN_CHUNK, seq)
    kern = functools.partial(_gdn_prompt_kernel, chunk=chunk, block=blk, group=min(GDN_UNROLL, blk // chunk))
    return pl.pallas_call(
        kern,
        grid=(batch, nblk),
        in_specs=[
            pl.BlockSpec((blk, CONV_CH), lambda b, j: (b * nblk + j, OFF_XC // CONV_CH)),
            pl.BlockSpec((blk, WIDTH_A), lambda b, j: (b * nblk + j, OFF_Z // WIDTH_A)),
            pl.BlockSpec((blk, LANE), lambda b, j: (b * nblk + j, OFF_BA // LANE)),
            pl.BlockSpec((None, SUBLANE, LANE), lambda b, j: (layer, 0, 0)),
            pl.BlockSpec((None, 1, HEAD_DIM_A), lambda b, j: (layer, 0, 0)),
        ],
        out_specs=[
            pl.BlockSpec((blk, WIDTH_A), lambda b, j: (b * nblk + j, 0)),
            pl.BlockSpec((1, N_HEADS_A, HEAD_DIM_A, HEAD_DIM_A), lambda b, j: (b, 0, 0, 0)),
        ],
        out_shape=[
            jax.ShapeDtypeStruct((batch * seq, WIDTH_A), BF16),
            jax.ShapeDtypeStruct((batch, N_HEADS_A, HEAD_DIM_A, HEAD_DIM_A), F32),
        ],
        scratch_shapes=[pltpu.VMEM((N_HEADS_A, HEAD_DIM_A, HEAD_DIM_A), F32)],
        compiler_params=_cparams(("arbitrary", "arbitrary")),
        name="gdn_prompt",
    )(proj, proj, proj, hp, gnw)


def _sink_column(sink_ref, layer, h, shape, axis, period):
    g = (lax.broadcasted_iota(jnp.int32, shape, axis) // period) % GQA_GROUP
    col = jnp.full(shape, sink_ref[layer, h * GQA_GROUP], F32)
    for gg in range(1, GQA_GROUP):
        col = jnp.where(g == gg, sink_ref[layer, h * GQA_GROUP + gg], col)
    return col


def _swa_prompt_kernel(sink_ref, q_ref, kc_ref, kp_ref, vc_ref, vp_ref, o_ref, *, layer, nsub):
    W = WINDOW
    j = pl.program_id(1)
    rows = GQA_GROUP * W
    q = (q_ref[...] * (HEAD_DIM_B ** -0.5)).astype(BF16)
    kall = jnp.concatenate([kp_ref[...], kc_ref[...]], axis=0).astype(BF16)
    vall = jnp.concatenate([vp_ref[...], vc_ref[...]], axis=0).astype(BF16)
    qi = lax.broadcasted_iota(jnp.int32, (rows, W), 0) % W
    kj = lax.broadcasted_iota(jnp.int32, (rows, W), 1)
    from_prev = kj > qi
    no_prev = from_prev & (kj > jnp.where(j == 0, -1, W))
    probs = [(sb, h) for sb in range(nsub) for h in range(N_KV_HEADS_B)]
    s = []
    for sb, h in probs:
        k_h = kall[sb * W:(sb + 2) * W, h * HEAD_DIM_B:(h + 1) * HEAD_DIM_B]
        qs = jnp.concatenate(
            [q[sb * W:(sb + 1) * W, (h * GQA_GROUP + g) * HEAD_DIM_B:(h * GQA_GROUP + g + 1) * HEAD_DIM_B]
             for g in range(GQA_GROUP)], axis=0)
        s2 = _bdot_nt(qs, k_h)
        fold = jnp.where(from_prev, s2[:, 0:W], s2[:, W:2 * W])
        s.append(jnp.where(no_prev, NEG_BIG, fold) if sb == 0 else fold)
    sk = [_sink_column(sink_ref, layer, h, (rows, 1), 0, W) for h in range(N_KV_HEADS_B)]
    m = [jnp.maximum(jnp.max(x, axis=-1, keepdims=True), sk[h]) for x, (sb, h) in zip(s, probs)]
    p = [jnp.exp(x - y) for x, y in zip(s, m)]
    inv = [1.0 / (jnp.sum(x, axis=-1, keepdims=True) + jnp.exp(sk[h] - y))
           for x, y, (sb, h) in zip(p, m, probs)]
    for x, r, (sb, h) in zip(p, inv, probs):
        v_h = vall[sb * W:(sb + 2) * W, h * HEAD_DIM_B:(h + 1) * HEAD_DIM_B]
        x2 = jnp.concatenate([jnp.where(from_prev, x, 0.0).astype(BF16),
                              jnp.where(from_prev, 0.0, x).astype(BF16)], axis=1)
        o = jnp.dot(x2, v_h, preferred_element_type=F32) * r
        for g in range(GQA_GROUP):
            c0 = (h * GQA_GROUP + g) * HEAD_DIM_B
            o_ref[sb * W:(sb + 1) * W, c0:c0 + HEAD_DIM_B] = o[g * W:(g + 1) * W, :].astype(o_ref.dtype)


def _swa_prompt(proj, sinks, layer, batch, seq):
    nsub = min(SWA_BLOCKS, seq // WINDOW)
    tq = nsub * WINDOW
    nb = seq // tq
    kcol = OFF_KB // KV_WIDTH_B
    vcol = OFF_VB // KV_WIDTH_B
    prev = lambda col: (lambda b, j: ((b * nb + j) * nsub - jnp.where(j > 0, 1, 0), col))
    return pl.pallas_call(
        functools.partial(_swa_prompt_kernel, layer=layer, nsub=nsub),
        grid=(batch, nb),
        in_specs=[
            pl.BlockSpec(memory_space=pltpu.SMEM),
            pl.BlockSpec((tq, WIDTH_B), lambda b, j: (b * nb + j, OFF_QB // WIDTH_B)),
            pl.BlockSpec((tq, KV_WIDTH_B), lambda b, j: (b * nb + j, kcol)),
            pl.BlockSpec((WINDOW, KV_WIDTH_B), prev(kcol)),
            pl.BlockSpec((tq, KV_WIDTH_B), lambda b, j: (b * nb + j, vcol)),
            pl.BlockSpec((WINDOW, KV_WIDTH_B), prev(vcol)),
        ],
        out_specs=pl.BlockSpec((tq, WIDTH_B), lambda b, j: (b * nb + j, 0)),
        out_shape=jax.ShapeDtypeStruct((batch * seq, WIDTH_B), BF16),
        compiler_params=_cparams(("arbitrary", "arbitrary")),
        name="swa_prompt",
    )(sinks, proj, proj, proj, proj, proj)


def _gdn_sample_kernel(xc_ref, z_ref, ba_ref, cs_ref, s0_ref, cw_ref, hp_ref, gnw_ref, acc_ref,
                       o_ref, s1_ref, lhs_scr, u_scr, kg_scr, res_scr, gl_scr, *, steps, group):
    del acc_ref
    T = steps
    GB = group
    H = N_HEADS_A
    HD = HEAD_DIM_A
    R = 2 * T

    u_scr[...] = jnp.zeros(u_scr.shape, F32)
    kg_scr[...] = jnp.zeros(kg_scr.shape, F32)

    xp = [cs_ref[i] for i in range(CONV_WIDTH - 1)] + [xc_ref[t] for t in range(T)]
    xs = []
    for t in range(T):
        acc = xp[t] * cw_ref[0:1, :]
        for i in range(1, CONV_WIDTH):
            acc = acc + xp[t + i] * cw_ref[i:i + 1, :]
        xs.append(_silu(acc))

    neg_a = -jnp.exp(hp_ref[0:1, :])
    dt_b = hp_ref[1:2, :]
    beta_all = [jax.nn.sigmoid(ba_ref[t]) for t in range(T)]
    g_all = [neg_a * _softplus(ba_ref[t] + dt_b) for t in range(T)]
    gc_all = [g_all[0]]
    for t in range(1, T):
        gc_all.append(gc_all[t - 1] + g_all[t])

    u_keep, attn_keep = [], []
    for h in range(H):
        q, k, v, kb, beta, gc = [], [], [], [], [], []
        for t in range(T):
            qh = xs[t][:, h * HD:(h + 1) * HD]
            kh = xs[t][:, WIDTH_A + h * HD:WIDTH_A + (h + 1) * HD]
            vh = xs[t][:, 2 * WIDTH_A + h * HD:2 * WIDTH_A + (h + 1) * HD]
            qh = qh * lax.rsqrt(jnp.sum(qh * qh, axis=-1, keepdims=True) + EPS) * (HD ** -0.5)
            kh = kh * lax.rsqrt(jnp.sum(kh * kh, axis=-1, keepdims=True) + EPS)
            bt = beta_all[t][:, h:h + 1]
            q.append(qh)
            k.append(kh)
            v.append(vh)
            kb.append(kh * bt)
            beta.append(bt)
            gc.append(gc_all[t][:, H + h:H + h + 1])
        u, w, attn = [], [], []
        for t in range(T):
            ut = v[t] * beta[t]
            wt = kb[t] * jnp.exp(gc[t])
            arow = []
            for s in range(t + 1):
                dec = jnp.exp(gc[t] - gc[s])
                arow.append(jnp.sum(q[t] * k[s], axis=-1, keepdims=True) * dec)
                if s < t:
                    l_ts = jnp.sum(kb[t] * k[s], axis=-1, keepdims=True) * dec
                    ut = ut - l_ts * u[s]
                    wt = wt - l_ts * w[s]
            u.append(ut)
            w.append(wt)
            attn.append(arow)
        g_last = gc[T - 1]
        for t in range(T):
            row0 = h * GB * R
            lhs_scr[pl.ds(row0 + t, GB, stride=R), :] = w[t]
            lhs_scr[pl.ds(row0 + T + t, GB, stride=R), :] = q[t] * jnp.exp(gc[t])
            u_scr[pl.ds(row0 + t, GB, stride=R), :] = u[t]
            kg_scr[pl.ds(row0 + t, GB, stride=R), :] = k[t] * jnp.exp(g_last - gc[t])
        gl_scr[h] = jnp.broadcast_to(jnp.exp(g_last), (GB, HD))
        u_keep.append(u)
        attn_keep.append(attn)

    inner = min(SAMPLE_INNER, GB)

    def body(i, carry):
        pairs = [(i * inner + g, h) for g in range(inner) for h in range(H)]
        rows = [pl.multiple_of((h * GB + b) * R, R) for b, h in pairs]
        res = [_bdot(lhs_scr[pl.ds(row, R), :], s0_ref[b, h]) for (b, h), row in zip(pairs, rows)]
        vn = []
        for row, r in zip(rows, res):
            res_scr[pl.ds(row, R), :] = r
            vn.append(u_scr[pl.ds(row, R), :] - r)
        upd = [_bdot_tn(kg_scr[pl.ds(row, R), :], x) for row, x in zip(rows, vn)]
        for (b, h), x in zip(pairs, upd):
            s1_ref[b, h] = s0_ref[b, h] * gl_scr[h, pl.ds(b, 1), :] + x
        return carry

    lax.fori_loop(0, GB // inner, body, 0)

    gnw = gnw_ref[...]
    for h in range(H):
        row0 = h * GB * R
        vn = []
        for t in range(T):
            ws_t = res_scr[pl.ds(row0 + t, GB, stride=R), :]
            vn.append(u_keep[h][t] - ws_t)
        for t in range(T):
            o_t = res_scr[pl.ds(row0 + T + t, GB, stride=R), :]
            for s in range(t + 1):
                o_t = o_t + attn_keep[h][t][s] * vn[s]
            z_t = z_ref[t][:, h * HD:(h + 1) * HD]
            o_ref[t, :, h * HD:(h + 1) * HD] = _rms(o_t, gnw) * _silu(z_t)


def _gdn_sample(proj3, cs_tm, s0, conv_w, hp, gnw, layer, acc):
    steps, nseq, _ = proj3.shape
    gb = min(GDN_SAMPLE_GROUP, nseq)
    rows = N_HEADS_A * gb * 2 * steps
    kern = functools.partial(_gdn_sample_kernel, steps=steps, group=gb)
    return pl.pallas_call(
        kern,
        grid=(nseq // gb,),
        input_output_aliases={8: 1},
        in_specs=[
            pl.BlockSpec((steps, gb, CONV_CH), lambda i: (0, i, OFF_XC // CONV_CH)),
            pl.BlockSpec((steps, gb, WIDTH_A), lambda i: (0, i, OFF_Z // WIDTH_A)),
            pl.BlockSpec((steps, gb, LANE), lambda i: (0, i, OFF_BA // LANE)),
            pl.BlockSpec((None, CONV_WIDTH - 1, gb, CONV_CH), lambda i: (layer, 0, i, 0)),
            pl.BlockSpec((None, gb, N_HEADS_A, HEAD_DIM_A, HEAD_DIM_A), lambda i: (layer, i, 0, 0, 0)),
            pl.BlockSpec((None, CONV_WIDTH, CONV_CH), lambda i: (layer, 0, 0)),
            pl.BlockSpec((None, SUBLANE, LANE), lambda i: (layer, 0, 0)),
            pl.BlockSpec((None, 1, HEAD_DIM_A), lambda i: (layer, 0, 0)),
            pl.BlockSpec(memory_space=pl.ANY),
        ],
        out_specs=[
            pl.BlockSpec((steps, gb, WIDTH_A), lambda i: (0, i, 0)),
            pl.BlockSpec((None, gb, N_HEADS_A, HEAD_DIM_A, HEAD_DIM_A), lambda i: (layer, i, 0, 0, 0)),
        ],
        out_shape=[
            jax.ShapeDtypeStruct((steps, nseq, WIDTH_A), F32),
            jax.ShapeDtypeStruct(s0.shape, F32),
        ],
        scratch_shapes=[
            pltpu.VMEM((rows, HEAD_DIM_A), F32),
            pltpu.VMEM((rows, HEAD_DIM_A), F32),
            pltpu.VMEM((rows, HEAD_DIM_A), F32),
            pltpu.VMEM((rows, HEAD_DIM_A), F32),
            pltpu.VMEM((N_HEADS_A, gb, HEAD_DIM_A), F32),
        ],
        compiler_params=_cparams(("arbitrary",)),
        name="gdn_sample",
    )(proj3, proj3, proj3, cs_tm, s0, conv_w, hp, gnw, acc)


def _swa_sample_kernel(sink_ref, q_ref, kc_ref, vc_ref, kn_ref, vn_ref, knt_ref, vnt_ref,
                       o_ref, ko_ref, vo_ref, *, steps, layer):
    T = steps
    W = WINDOW
    rows = T * GQA_GROUP
    t_of_row = lax.broadcasted_iota(jnp.int32, (1, rows, 1), 1) // GQA_GROUP
    kj = lax.broadcasted_iota(jnp.int32, (1, 1, W), 2)
    cmask = kj > t_of_row
    keep = lax.broadcasted_iota(jnp.int32, (1, 1, W), 2) < W - T
    for h in range(N_KV_HEADS_B):
        lo, hi = h * HEAD_DIM_B, (h + 1) * HEAD_DIM_B
        qh = q_ref[:, h]
        qh_r = qh.astype(BF16)
        kt = kc_ref[:, h]
        vt = vc_ref[:, h]
        ko_ref[:, h] = jnp.where(keep, pltpu.roll(kt, W - T, 2), knt_ref[:, h])
        vo_ref[:, h] = jnp.where(keep, pltpu.roll(vt, W - T, 2), vnt_ref[:, h])
        sc = jnp.einsum("bqd,bdk->bqk", qh_r, kt.astype(BF16),
                        preferred_element_type=F32) * (HEAD_DIM_B ** -0.5)
        sc = jnp.where(cmask, sc, NEG_BIG)
        sn = []
        for s in range(T):
            kn_s = kn_ref[:, s:s + 1, lo:hi]
            v = jnp.sum(qh * kn_s, axis=-1, keepdims=True) * (HEAD_DIM_B ** -0.5)
            sn.append(jnp.where(t_of_row >= s, v, NEG_BIG))
        sk = _sink_column(sink_ref, layer, h, (1, rows, 1), 1, 1)
        m = jnp.maximum(jnp.max(sc, axis=-1, keepdims=True), sk)
        for s in range(T):
            m = jnp.maximum(m, sn[s])
        pc = jnp.exp(sc - m)
        pn = [jnp.exp(sn[s] - m) for s in range(T)]
        den = jnp.sum(pc, axis=-1, keepdims=True) + jnp.exp(sk - m)
        for s in range(T):
            den = den + pn[s]
        inv = 1.0 / den
        o = jnp.einsum("bqk,bdk->bqd", (pc * inv).astype(BF16), vt.astype(BF16), preferred_element_type=F32)
        for s in range(T):
            o = o + (pn[s] * inv) * vn_ref[:, s:s + 1, lo:hi]
        o_ref[:, h] = o


def _swa_sample(q_r, kc_t, vc_t, kn, vn, knt, vnt, sinks, layer):
    nseq, _, rows, _ = q_r.shape
    steps = kn.shape[1]
    gb = min(SAMPLE_GROUP, nseq)
    kern = functools.partial(_swa_sample_kernel, steps=steps, layer=layer)
    cache_spec = pl.BlockSpec((None, gb, N_KV_HEADS_B, HEAD_DIM_B, WINDOW), lambda i: (layer, i, 0, 0, 0))
    win_spec = pl.BlockSpec((gb, N_KV_HEADS_B, HEAD_DIM_B, WINDOW), lambda i: (i, 0, 0, 0))
    new_spec = pl.BlockSpec((gb, steps, KV_WIDTH_B), lambda i: (i, 0, 0))
    q_spec = pl.BlockSpec((gb, N_KV_HEADS_B, rows, HEAD_DIM_B), lambda i: (i, 0, 0, 0))
    win_shape = jax.ShapeDtypeStruct((nseq, N_KV_HEADS_B, HEAD_DIM_B, WINDOW), F32)
    return pl.pallas_call(
        kern,
        grid=(nseq // gb,),
        in_specs=[pl.BlockSpec(memory_space=pltpu.SMEM), q_spec, cache_spec, cache_spec,
                  new_spec, new_spec, win_spec, win_spec],
        out_specs=[q_spec, win_spec, win_spec],
        out_shape=[jax.ShapeDtypeStruct(q_r.shape, F32), win_shape, win_shape],
        compiler_params=_cparams(("arbitrary",)),
        name="swa_sample",
    )(sinks, q_r, kc_t, vc_t, kn, vn, knt, vnt)


def _prompt_layer(x, p, layer, batch, seq):
    tm = min(TOKEN_TILE, seq)
    rep = seq // tm
    proj, tail = _premix(x, p["mod_p"], p["g_pre_mix"], p["w_in"], layer, tm, rep, conv_w=p["conv_w"])
    o_a, ssm = _gdn_prompt(proj, p["hp"], p["gdn_norm_w"], layer, batch, seq)
    o_b = _swa_prompt(proj, p["sinks"], layer, batch, seq)
    x = _ffn(x, o_a, o_b, p["mod_p"], p["g_post_mix"], p["g_pre_ffn"], p["g_post_ffn"],
             p["w_o"], p["w_up"], p["w_down"], layer, tm, rep)
    p3 = proj.reshape(batch, seq, PROJ_COLS)
    conv = tail[:, SUBLANE - (CONV_WIDTH - 1):, :]
    k_new = p3[:, seq - WINDOW:, OFF_KB:OFF_KB + KV_WIDTH_B].reshape(batch, WINDOW, N_KV_HEADS_B, HEAD_DIM_B)
    v_new = p3[:, seq - WINDOW:, OFF_VB:OFF_VB + KV_WIDTH_B].reshape(batch, WINDOW, N_KV_HEADS_B, HEAD_DIM_B)
    return x, conv, ssm, k_new, v_new


def _sample_layer(x, p, layer, nseq, steps, ssm_acc):
    proj = _premix(x, p["mod_s"], p["g_pre_mix"], p["w_in"], layer, steps * nseq, steps)
    p3 = proj.reshape(steps, nseq, PROJ_COLS)
    o_a, ssm = _gdn_sample(p3, p["conv_tm"], p["state_ssm"], p["conv_w"], p["hp"], p["gdn_norm_w"], layer,
                           acc=ssm_acc)
    q_r = p3[:, :, OFF_QB:OFF_QB + WIDTH_B].reshape(steps, nseq, N_KV_HEADS_B, GQA_GROUP, HEAD_DIM_B)
    q_r = jnp.transpose(q_r, (1, 2, 0, 3, 4)).reshape(nseq, N_KV_HEADS_B, steps * GQA_GROUP, HEAD_DIM_B)
    k_tm = p3[:, :, OFF_KB:OFF_KB + KV_WIDTH_B]
    v_tm = p3[:, :, OFF_VB:OFF_VB + KV_WIDTH_B]

    def tail_lanes(a):
        a = jnp.transpose(a.reshape(steps, nseq, N_KV_HEADS_B, HEAD_DIM_B), (1, 2, 3, 0))
        return jnp.pad(a, ((0, 0), (0, 0), (0, 0), (WINDOW - steps, 0)))

    o_b, k_win, v_win = _swa_sample(q_r, p["cache_kt"], p["cache_vt"], jnp.transpose(k_tm, (1, 0, 2)),
                                    jnp.transpose(v_tm, (1, 0, 2)), tail_lanes(k_tm), tail_lanes(v_tm),
                                    p["sinks"], layer)
    o_b = o_b.reshape(nseq, N_KV_HEADS_B, steps, GQA_GROUP, HEAD_DIM_B)
    o_b = jnp.transpose(o_b, (2, 0, 1, 3, 4)).reshape(steps * nseq, WIDTH_B)
    x = _ffn(x, o_a.reshape(steps * nseq, WIDTH_A), o_b, p["mod_s"], p["g_post_mix"], p["g_pre_ffn"],
             p["g_post_ffn"], p["w_o"], p["w_up"], p["w_down"], layer, steps * nseq, steps)
    xp = jnp.concatenate([p["conv_tm"][layer], p3[:, :, OFF_XC:OFF_XC + CONV_CH]], axis=0)
    conv = jnp.transpose(xp[xp.shape[0] - (CONV_WIDTH - 1):], (1, 0, 2))
    return x, conv, ssm, k_win, v_win


def _pack_w_in(w_in):
    o1 = CONV_CH + WIDTH_A
    o2 = o1 + 2 * N_HEADS_A
    pad = jnp.zeros(w_in.shape[:-1] + (PROJ_COLS - OFF_BA - 2 * N_HEADS_A,), w_in.dtype)
    return jnp.concatenate([w_in[..., :o1], w_in[..., o2:], w_in[..., o1:o2], pad], axis=-1).astype(BF16)


def _head_params(a_log, dt_bias):
    depth = a_log.shape[0]
    hp = jnp.zeros((depth, SUBLANE, LANE), F32)
    hp = hp.at[:, 0, N_HEADS_A:2 * N_HEADS_A].set(a_log.astype(F32))
    hp = hp.at[:, 1, N_HEADS_A:2 * N_HEADS_A].set(dt_bias.astype(F32))
    return hp


def kernel(x_prompt, x_sample, state_conv, state_ssm, cache_swa_k, cache_swa_v, c_prompt, c_sample,
           g_pre_mix, g_post_mix, g_pre_ffn, g_post_ffn, w_ada, b_ada, w_in, conv_w, a_log, dt_bias,
           gdn_norm_w, sinks, w_o, w_up, w_down):
    depth = w_ada.shape[0]
    batch, seq, _ = x_prompt.shape
    nseq, steps, _ = x_sample.shape

    c_all = jnp.concatenate([c_sample, c_prompt], axis=0)
    rows = c_all.shape[0]
    rows_p = -(-rows // 8) * 8
    c_all = jnp.pad(c_all, ((0, rows_p - rows), (0, 0)))
    mod_all = _modulation(c_all, w_ada, b_ada)
    p = dict(
        mod_p=jnp.transpose(mod_all[:, :, nseq:nseq + batch], (0, 2, 1, 3)).reshape(depth, batch, N_MOD, 1, D_MODEL),
        mod_s=mod_all,
        g_pre_mix=g_pre_mix.reshape(depth, 1, D_MODEL), g_post_mix=g_post_mix.reshape(depth, 1, D_MODEL),
        g_pre_ffn=g_pre_ffn.reshape(depth, 1, D_MODEL), g_post_ffn=g_post_ffn.reshape(depth, 1, D_MODEL),
        w_in=_pack_w_in(w_in), conv_w=conv_w, hp=_head_params(a_log, dt_bias),
        gdn_norm_w=gdn_norm_w.reshape(depth, 1, HEAD_DIM_A), sinks=sinks,
        w_o=w_o.astype(BF16), w_up=w_up.astype(BF16), w_down=w_down.astype(BF16),
        conv_tm=jnp.transpose(state_conv, (0, 2, 1, 3)), state_ssm=state_ssm,
        cache_kt=jnp.transpose(cache_swa_k, (0, 1, 3, 4, 2)),
        cache_vt=jnp.transpose(cache_swa_v, (0, 1, 3, 4, 2)),
    )

    yp = x_prompt.reshape(batch * seq, D_MODEL)
    ys = jnp.transpose(x_sample, (1, 0, 2)).reshape(steps * nseq, D_MODEL)
    outs_p, outs_s = [], []
    ssm_s = jnp.zeros(state_ssm.shape, F32)
    for l in range(depth):
        yp, c1, s1, k1, v1 = _prompt_layer(yp, p, l, batch, seq)
        ys, c2, ssm_s, k2, v2 = _sample_layer(ys, p, l, nseq, steps, ssm_s)
        outs_p.append((c1, s1, k1, v1))
        outs_s.append((c2, None, k2, v2))

    y_prompt = yp.reshape(batch, seq, D_MODEL)
    y_sample = jnp.transpose(ys.reshape(steps, nseq, D_MODEL), (1, 0, 2))
    stack = lambda outs, i: jnp.stack([o[i] for o in outs])
    k_s = jnp.transpose(stack(outs_s, 2), (0, 1, 4, 2, 3))
    v_s = jnp.transpose(stack(outs_s, 3), (0, 1, 4, 2, 3))
    return (y_prompt, y_sample,
            stack(outs_p, 0), stack(outs_p, 1), stack(outs_p, 2), stack(outs_p, 3),
            stack(outs_s, 0), ssm_s, k_s, v_s)
```

```python
import functools

import jax
import jax.numpy as jnp
from jax import lax
from jax.experimental import pallas as pl
from jax.experimental.pallas import tpu as pltpu

F32 = jnp.float32
BF16 = jnp.bfloat16

D_MODEL = 1024
N_MOD = 6
N_HEADS_A = 4
HEAD_DIM_A = 128
WIDTH_A = N_HEADS_A * HEAD_DIM_A
CONV_WIDTH = 4
CONV_CH = 3 * WIDTH_A
N_Q_HEADS_B = 8
N_KV_HEADS_B = 2
GQA_GROUP = N_Q_HEADS_B // N_KV_HEADS_B
HEAD_DIM_B = 64
WIDTH_B = N_Q_HEADS_B * HEAD_DIM_B
KV_WIDTH_B = N_KV_HEADS_B * HEAD_DIM_B
WINDOW = 128
D_FF = 4 * D_MODEL
EPS = 1e-6
NEG_BIG = -1e30

OFF_XC = 0
OFF_Z = OFF_XC + CONV_CH
OFF_QB = OFF_Z + WIDTH_A
OFF_KB = OFF_QB + WIDTH_B
OFF_VB = OFF_KB + KV_WIDTH_B
OFF_BA = OFF_VB + KV_WIDTH_B
LANE = 128
SUBLANE = 8
PROJ_COLS = OFF_BA + LANE

GDN_CHUNK = 64
GDN_BLOCK = 512
GDN_UNROLL = 8
SWA_BLOCKS = 8
TOKEN_TILE = 512
FFN_CHUNK = 1024
MOD_COLS = 2048
SAMPLE_GROUP = 32
GDN_SAMPLE_GROUP = 32
SAMPLE_INNER = 8
VMEM_LIMIT = 56 * 1024 * 1024


def _silu(x):
    return x * jax.nn.sigmoid(x)


def _softplus(x):
    return jnp.maximum(x, 0.0) + jnp.log(1.0 + jnp.exp(-jnp.abs(x)))


def _rms(x, w):
    return x * lax.rsqrt(jnp.mean(x * x, axis=-1, keepdims=True) + EPS) * w


def _bdot(a, b):
    return jnp.dot(a.astype(BF16), b.astype(BF16), preferred_element_type=F32)


def _bdot_nt(a, b):
    return lax.dot_general(a.astype(BF16), b.astype(BF16), (((1,), (1,)), ((), ())),
                           preferred_element_type=F32)


def _bdot_tn(a, b):
    return lax.dot_general(a.astype(BF16), b.astype(BF16), (((0,), (0,)), ((), ())),
                           preferred_element_type=F32)


def _split3(x):
    hi = x.astype(BF16)
    r1 = x - hi.astype(F32)
    mid = r1.astype(BF16)
    lo = (r1 - mid.astype(F32)).astype(BF16)
    return hi, mid, lo


def _cparams(sem):
    return pltpu.CompilerParams(dimension_semantics=sem, vmem_limit_bytes=VMEM_LIMIT)


def _mod_kernel(c_ref, w_ref, b_ref, o_ref):
    s = _silu(c_ref[...]).astype(BF16)
    r = jnp.dot(s, w_ref[...].astype(BF16), preferred_element_type=F32) + b_ref[...]
    for k in range(o_ref.shape[0]):
        o_ref[k] = r[:, k * D_MODEL:(k + 1) * D_MODEL]


def _modulation(c_all, w_ada, b_ada):
    depth = w_ada.shape[0]
    rows = c_all.shape[0]
    per = MOD_COLS // D_MODEL
    return pl.pallas_call(
        _mod_kernel,
        grid=(depth, N_MOD // per),
        in_specs=[
            pl.BlockSpec((rows, D_MODEL), lambda l, j: (0, 0)),
            pl.BlockSpec((None, D_MODEL, MOD_COLS), lambda l, j: (l, 0, j)),
            pl.BlockSpec((None, 1, MOD_COLS), lambda l, j: (l, 0, j)),
        ],
        out_specs=pl.BlockSpec((None, per, rows, D_MODEL), lambda l, j: (l, j, 0, 0)),
        out_shape=jax.ShapeDtypeStruct((depth, N_MOD, rows, D_MODEL), F32),
        compiler_params=_cparams(("arbitrary", "arbitrary")),
        name="adaln_mod",
    )(c_all, w_ada, b_ada.reshape(depth, 1, N_MOD * D_MODEL))


def _mod_spec(mod, layer, tm, rep):
    if mod.ndim == 5:
        return pl.BlockSpec((None, None, N_MOD, 1, D_MODEL), lambda i: (layer, i // rep, 0, 0, 0))
    return pl.BlockSpec((None, N_MOD, tm // rep, D_MODEL), lambda i: (layer, 0, 0, 0))


def _mod_rows(mod_ref, k, tm):
    m = mod_ref[k]
    if m.shape[0] in (1, tm):
        return m
    return jnp.concatenate([m] * (tm // m.shape[0]), axis=0)


def _premix_kernel(x_ref, mod_ref, g_ref, w_ref, o_ref):
    x = x_ref[...]
    tm = x.shape[0]
    h = (_rms(x, g_ref[...]) * (1.0 + _mod_rows(mod_ref, 1, tm)) + _mod_rows(mod_ref, 0, tm)).astype(BF16)
    o_ref[...] = jnp.dot(h, w_ref[...], preferred_element_type=F32)


def _premix_conv_kernel(x_ref, mod_ref, g_ref, w_ref, cw_ref, o_ref, tail_ref, prev_scr, *, rep):
    tm = x_ref.shape[0]
    x = x_ref[...]
    h = (_rms(x, g_ref[...]) * (1.0 + _mod_rows(mod_ref, 1, tm)) + _mod_rows(mod_ref, 0, tm)).astype(BF16)

    @pl.when(pl.program_id(0) % rep == 0)
    def _():
        prev_scr[...] = jnp.zeros(prev_scr.shape, F32)

    def project(c0, c1):
        return jnp.dot(h, w_ref[:, c0:c1], preferred_element_type=F32)

    def conv_group(xc, grp):
        c0 = grp * WIDTH_A
        win = jnp.concatenate([prev_scr[:, c0:c0 + WIDTH_A], xc], axis=0)
        conv = xc * cw_ref[CONV_WIDTH - 1:CONV_WIDTH, c0:c0 + WIDTH_A]
        for s in range(1, CONV_WIDTH):
            conv = conv + pltpu.roll(win, s, 0)[SUBLANE:] * cw_ref[CONV_WIDTH - 1 - s:CONV_WIDTH - s, c0:c0 + WIDTH_A]
        xs = _silu(conv)
        for hh in range(N_HEADS_A):
            blk = xs[:, hh * HEAD_DIM_A:(hh + 1) * HEAD_DIM_A]
            if grp < 2:
                scale = lax.rsqrt(jnp.sum(blk * blk, axis=-1, keepdims=True) + EPS)
                blk = blk * (scale * (HEAD_DIM_A ** -0.5) if grp == 0 else scale)
            o_ref[:, OFF_XC + c0 + hh * HEAD_DIM_A:OFF_XC + c0 + (hh + 1) * HEAD_DIM_A] = blk
        prev_scr[:, c0:c0 + WIDTH_A] = xc[tm - SUBLANE:, :]
        tail_ref[0, :, c0:c0 + WIDTH_A] = xc[tm - SUBLANE:, :]

    xq = project(OFF_XC, OFF_XC + WIDTH_A)
    xk = project(OFF_XC + WIDTH_A, OFF_XC + 2 * WIDTH_A)
    conv_group(xq, 0)
    xv = project(OFF_XC + 2 * WIDTH_A, OFF_XC + CONV_CH)
    conv_group(xk, 1)
    o_ref[:, OFF_Z:] = project(OFF_Z, PROJ_COLS)
    conv_group(xv, 2)


def _premix(x, mod, g, w, layer, tm, rep, conv_w=None):
    n = x.shape[0]
    in_specs = [
        pl.BlockSpec((tm, D_MODEL), lambda i: (i, 0)),
        _mod_spec(mod, layer, tm, rep),
        pl.BlockSpec((None, 1, D_MODEL), lambda i: (layer, 0, 0)),
        pl.BlockSpec((None, D_MODEL, PROJ_COLS), lambda i: (layer, 0, 0)),
    ]
    proj_spec = pl.BlockSpec((tm, PROJ_COLS), lambda i: (i, 0))
    proj_shape = jax.ShapeDtypeStruct((n, PROJ_COLS), F32)
    if conv_w is None:
        return pl.pallas_call(
            _premix_kernel, grid=(n // tm,), in_specs=in_specs, out_specs=proj_spec, out_shape=proj_shape,
            compiler_params=_cparams(("arbitrary",)), name="premix_proj",
        )(x, mod, g, w)
    nseq = n // (tm * rep)
    return pl.pallas_call(
        functools.partial(_premix_conv_kernel, rep=rep),
        grid=(n // tm,),
        in_specs=in_specs + [pl.BlockSpec((None, CONV_WIDTH, CONV_CH), lambda i: (layer, 0, 0))],
        out_specs=[proj_spec, pl.BlockSpec((1, SUBLANE, CONV_CH), lambda i: (i // rep, 0, 0))],
        out_shape=[proj_shape, jax.ShapeDtypeStruct((nseq, SUBLANE, CONV_CH), F32)],
        scratch_shapes=[pltpu.VMEM((SUBLANE, CONV_CH), F32)],
        compiler_params=_cparams(("arbitrary",)),
        name="premix_conv_proj",
    )(x, mod, g, w, conv_w)


def _ffn_kernel(x_ref, oa_ref, ob_ref, mod_ref, gpm_ref, gpf_ref, gqf_ref, wo_ref, wup_ref, wdn_ref, o_ref):
    x = x_ref[...]
    o_mix = jnp.concatenate([oa_ref[...].astype(BF16), ob_ref[...].astype(BF16)], axis=1)
    mix = jnp.dot(o_mix, wo_ref[...], preferred_element_type=F32)
    tm = x.shape[0]
    x1 = x + _mod_rows(mod_ref, 2, tm) * _rms(mix, gpm_ref[...])
    h = (_rms(x1, gpf_ref[...]) * (1.0 + _mod_rows(mod_ref, 4, tm)) + _mod_rows(mod_ref, 3, tm)).astype(BF16)
    ck = FFN_CHUNK
    acc = jnp.zeros(x.shape, F32)
    for c in range(D_FF // ck):
        u = jnp.dot(h, wup_ref[:, c * ck:(c + 1) * ck], preferred_element_type=F32)
        a = jnp.square(jnp.maximum(u, 0.0)).astype(BF16)
        acc = acc + jnp.dot(a, wdn_ref[c * ck:(c + 1) * ck, :], preferred_element_type=F32)
    o_ref[...] = x1 + _mod_rows(mod_ref, 5, tm) * _rms(acc, gqf_ref[...])


def _ffn(x, oa, ob, mod, g_post_mix, g_pre_ffn, g_post_ffn, wo, wup, wdn, layer, tm, rep):
    n = x.shape[0]
    const = lambda i: (layer, 0, 0)
    return pl.pallas_call(
        _ffn_kernel,
        grid=(n // tm,),
        in_specs=[
            pl.BlockSpec((tm, D_MODEL), lambda i: (i, 0)),
            pl.BlockSpec((tm, WIDTH_A), lambda i: (i, 0)),
            pl.BlockSpec((tm, WIDTH_B), lambda i: (i, 0)),
            _mod_spec(mod, layer, tm, rep),
            pl.BlockSpec((None, 1, D_MODEL), const),
            pl.BlockSpec((None, 1, D_MODEL), const),
            pl.BlockSpec((None, 1, D_MODEL), const),
            pl.BlockSpec((None, D_MODEL, D_MODEL), const, pipeline_mode=pl.Buffered(1)),
            pl.BlockSpec((None, D_MODEL, D_FF), const, pipeline_mode=pl.Buffered(1)),
            pl.BlockSpec((None, D_FF, D_MODEL), const, pipeline_mode=pl.Buffered(1)),
        ],
        out_specs=pl.BlockSpec((tm, D_MODEL), lambda i: (i, 0)),
        out_shape=jax.ShapeDtypeStruct((n, D_MODEL), F32),
        compiler_params=_cparams(("arbitrary",)),
        name="outproj_ffn",
    )(x, oa, ob, mod, g_post_mix, g_pre_ffn, g_post_ffn, wo, wup, wdn)


def _lane_bcast(x, col):
    return jnp.broadcast_to(x[:, col:col + 1], (x.shape[0], HEAD_DIM_A))


def _gdn_prompt_kernel(xs_ref, z_ref, ba_ref, hp_ref, gnw_ref, o_ref, ssm_ref, s_scr, *, chunk, block, group):
    C = chunk
    G = group
    H = N_HEADS_A
    HD = HEAD_DIM_A
    PW = H * C
    j = pl.program_id(1)
    nj = pl.num_programs(1)

    @pl.when(j == 0)
    def _():
        s_scr[...] = jnp.zeros(s_scr.shape, F32)

    ri = lax.broadcasted_iota(jnp.int32, (C, PW), 0)
    li = lax.broadcasted_iota(jnp.int32, (C, PW), 1)
    lc = li % C
    lower = ri >= lc
    strict = ri > lc
    eye = ri == lc
    rb = lax.broadcasted_iota(jnp.int32, (PW, PW), 0) // C
    cb = lax.broadcasted_iota(jnp.int32, (PW, PW), 1) // C
    bd_mask = rb == cb
    kbd_mask = (lax.broadcasted_iota(jnp.int32, (PW, WIDTH_A), 0) // C
                == lax.broadcasted_iota(jnp.int32, (PW, WIDTH_A), 1) // HD)
    rhs_mask = (lax.broadcasted_iota(jnp.int32, (PW, 2 * WIDTH_A), 0) // C
                == (lax.broadcasted_iota(jnp.int32, (PW, 2 * WIDTH_A), 1) % WIDTH_A) // HD)
    tri = (lax.broadcasted_iota(jnp.int32, (C, C), 0)
           >= lax.broadcasted_iota(jnp.int32, (C, C), 1)).astype(BF16)
    lane128 = lax.broadcasted_iota(jnp.int32, (C, LANE), 1)

    neg_a = -jnp.exp(hp_ref[0:1, :])
    dt_b = hp_ref[1:2, :]
    gnw = gnw_ref[...]

    def bd(y):
        return jnp.where(bd_mask, jnp.concatenate([y] * H, axis=0), 0.0)

    def pmm(x, y):
        return _bdot(x, bd(y))

    def pack_cols(cols):
        per_tile = LANE // C
        tiles = []
        for t in range(H // per_tile):
            acc = cols[t * per_tile]
            for u in range(1, per_tile):
                acc = jnp.where(lane128 < u * C, acc, cols[t * per_tile + u])
            tiles.append(acc)
        return jnp.concatenate(tiles, axis=1)

    def prepare(r0):
        xs = xs_ref[pl.ds(r0, C), :]
        ba = ba_ref[pl.ds(r0, C), :]
        beta_all = jax.nn.sigmoid(ba)
        g_all = neg_a * _softplus(ba + dt_b)
        g_hi, g_mid, g_lo = _split3(g_all)
        g3 = jnp.dot(tri, jnp.concatenate([g_hi, g_mid, g_lo], axis=1), preferred_element_type=F32)
        gc_all = g3[:, 0:LANE] + g3[:, LANE:2 * LANE] + g3[:, 2 * LANE:3 * LANE]

        q, k, v, kb, beta_c, gc_c, eg_c = [], [], [], [], [], [], []
        for h in range(H):
            qh = xs[:, h * HD:(h + 1) * HD]
            kh = xs[:, WIDTH_A + h * HD:WIDTH_A + (h + 1) * HD]
            vh = xs[:, 2 * WIDTH_A + h * HD:2 * WIDTH_A + (h + 1) * HD]
            bh = _lane_bcast(beta_all, h)
            gh = _lane_bcast(gc_all, N_HEADS_A + h)
            q.append(qh)
            k.append(kh)
            v.append(vh)
            kb.append(kh * bh)
            beta_c.append(bh)
            gc_c.append(gh)
            eg_c.append(jnp.exp(gh))

        gcp = pack_cols(gc_c)
        gc_row = jnp.sum(jnp.where(eye, gcp, 0.0), axis=0, keepdims=True)
        decay = jnp.exp(jnp.where(lower, gcp - gc_row, NEG_BIG))

        k_p = jnp.concatenate(k, axis=1)
        kbd = jnp.where(kbd_mask, jnp.concatenate([k_p] * H, axis=0), 0.0)
        lhs = jnp.concatenate([jnp.concatenate(kb, axis=1), jnp.concatenate(q, axis=1)], axis=0)
        kq = _bdot_nt(lhs, kbd)
        lmat = jnp.where(strict, kq[0:C] * decay, 0.0)
        rhs = jnp.concatenate([v[h] * beta_c[h] for h in range(H)]
                              + [kb[h] * eg_c[h] for h in range(H)], axis=1)
        g_last = [gc_all[C - 1:C, N_HEADS_A + h:N_HEADS_A + h + 1] for h in range(H)]
        return dict(lmat=lmat, attn=kq[C:2 * C] * decay, rhs=rhs, g_last=g_last,
                    qg=[q[h] * eg_c[h] for h in range(H)],
                    kg=[k[h] * jnp.exp(g_last[h] - gc_c[h]) for h in range(H)])

    base_blk = min(16, C)
    same_base = (ri // base_blk) == (lc // base_blk)

    def prepare_phase(r0s, ch):
        return [lambda r0=r0: ch.append(prepare(r0)) for r0 in r0s]

    def solve_phase(ch, res):
        st = {}
        thunks = []

        def init():
            st["pw"] = [jnp.where(same_base, c["lmat"], 0.0) for c in ch]
            st["p"] = [jnp.where(eye, 1.0, 0.0) - x for x in st["pw"]]
        thunks.append(init)

        def square():
            st["pw"] = [pmm(x, x) for x in st["pw"]]

        def accumulate():
            st["p"] = [x + pmm(x, y) for x, y in zip(st["p"], st["pw"])]
        span = 2
        while span < base_blk:
            thunks += [square, accumulate]
            span *= 2
        blk = base_blk
        same = same_base
        while blk < C:
            nxt = 2 * blk
            same_n = (ri // nxt) == (lc // nxt)
            off = jnp.logical_and(same_n, jnp.logical_not(same))

            def left(off=off):
                st["t"] = [pmm(jnp.where(off, c["lmat"], 0.0), x) for c, x in zip(ch, st["p"])]

            def merge_blocks():
                st["p"] = [x - pmm(x, y) for x, y in zip(st["p"], st["t"])]
            thunks += [left, merge_blocks]
            same = same_n
            blk = nxt

        def solve():
            res["sol"] = [_bdot(x, jnp.where(rhs_mask, jnp.concatenate([c["rhs"]] * H, axis=0), 0.0))
                          for c, x in zip(ch, st["p"])]

        def attn_products():
            res["asol"] = [_bdot(c["attn"], jnp.where(rhs_mask, jnp.concatenate([x] * H, axis=0), 0.0))
                           for c, x in zip(ch, res["sol"])]

        def state_operands():
            res["ab"] = [[_bdot_tn(c["kg"][h], jnp.concatenate(
                              [x[:, WIDTH_A + h * HD:WIDTH_A + (h + 1) * HD], x[:, h * HD:(h + 1) * HD]], axis=1))
                          for h in range(H)] for c, x in zip(ch, res["sol"])]
        return thunks + [solve, attn_products, state_operands]

    def finish_phase(r0s, ch, res):
        def one(g):
            for h in range(H):
                asol = res["asol"][g]
                ab = res["ab"][g][h]
                q_eff = ch[g]["qg"][h] - asol[:, WIDTH_A + h * HD:WIDTH_A + (h + 1) * HD]
                sh = s_scr[h]
                o_h = _bdot(q_eff, sh) + asol[:, h * HD:(h + 1) * HD]
                s_scr[h] = sh * jnp.exp(ch[g]["g_last"][h]) - _bdot(ab[:, 0:HD], sh) + ab[:, HD:2 * HD]
                z_h = z_ref[pl.ds(r0s[g], C), h * HD:(h + 1) * HD]
                o_ref[pl.ds(r0s[g], C), h * HD:(h + 1) * HD] = (_rms(o_h, gnw) * _silu(z_h)).astype(o_ref.dtype)
        return [lambda g=g: one(g) for g in range(len(r0s))]

    def step(i, carry):
        r0s = [pl.multiple_of((i * G + g) * C, C) for g in range(G)]
        ch, res = [], {}
        for thunk in prepare_phase(r0s, ch) + solve_phase(ch, res) + finish_phase(r0s, ch, res):
            thunk()
        return carry

    lax.fori_loop(0, block // (G * C), step, 0)

    @pl.when(j == nj - 1)
    def _():
        ssm_ref[0] = s_scr[...]


def _gdn_prompt(proj, hp, gnw, layer, batch, seq):
    blk = min(GDN_BLOCK, seq)
    nblk = seq // blk
    chunk = min(GDN_CHUNK, seq)
    kern = functools.partial(_gdn_prompt_kernel, chunk=chunk, block=blk, group=min(GDN_UNROLL, blk // chunk))
    return pl.pallas_call(
        kern,
        grid=(batch, nblk),
        in_specs=[
            pl.BlockSpec((blk, CONV_CH), lambda b, j: (b * nblk + j, OFF_XC // CONV_CH)),
            pl.BlockSpec((blk, WIDTH_A), lambda b, j: (b * nblk + j, OFF_Z // WIDTH_A)),
            pl.BlockSpec((blk, LANE), lambda b, j: (b * nblk + j, OFF_BA // LANE)),
            pl.BlockSpec((None, SUBLANE, LANE), lambda b, j: (layer, 0, 0)),
            pl.BlockSpec((None, 1, HEAD_DIM_A), lambda b, j: (layer, 0, 0)),
        ],
        out_specs=[
            pl.BlockSpec((blk, WIDTH_A), lambda b, j: (b * nblk + j, 0)),
            pl.BlockSpec((1, N_HEADS_A, HEAD_DIM_A, HEAD_DIM_A), lambda b, j: (b, 0, 0, 0)),
        ],
        out_shape=[
            jax.ShapeDtypeStruct((batch * seq, WIDTH_A), BF16),
            jax.ShapeDtypeStruct((batch, N_HEADS_A, HEAD_DIM_A, HEAD_DIM_A), F32),
        ],
        scratch_shapes=[pltpu.VMEM((N_HEADS_A, HEAD_DIM_A, HEAD_DIM_A), F32)],
        compiler_params=_cparams(("arbitrary", "arbitrary")),
        name="gdn_prompt",
    )(proj, proj, proj, hp, gnw)


def _sink_column(sink_ref, layer, h, shape, axis, period):
    g = (lax.broadcasted_iota(jnp.int32, shape, axis) // period) % GQA_GROUP
    col = jnp.full(shape, sink_ref[layer, h * GQA_GROUP], F32)
    for gg in range(1, GQA_GROUP):
        col = jnp.where(g == gg, sink_ref[layer, h * GQA_GROUP + gg], col)
    return col


def _swa_prompt_kernel(sink_ref, q_ref, kc_ref, kp_ref, vc_ref, vp_ref, o_ref, *, layer, nsub):
    W = WINDOW
    j = pl.program_id(1)
    rows = GQA_GROUP * W
    q = (q_ref[...] * (HEAD_DIM_B ** -0.5)).astype(BF16)
    kall = jnp.concatenate([kp_ref[...], kc_ref[...]], axis=0).astype(BF16)
    vall = jnp.concatenate([vp_ref[...], vc_ref[...]], axis=0).astype(BF16)
    qi = lax.broadcasted_iota(jnp.int32, (rows, W), 0) % W
    kj = lax.broadcasted_iota(jnp.int32, (rows, W), 1)
    from_prev = kj > qi
    no_prev = from_prev & (kj > jnp.where(j == 0, -1, W))
    probs = [(sb, h) for sb in range(nsub) for h in range(N_KV_HEADS_B)]
    s = []
    for sb, h in probs:
        k_h = kall[sb * W:(sb + 2) * W, h * HEAD_DIM_B:(h + 1) * HEAD_DIM_B]
        qs = jnp.concatenate(
            [q[sb * W:(sb + 1) * W, (h * GQA_GROUP + g) * HEAD_DIM_B:(h * GQA_GROUP + g + 1) * HEAD_DIM_B]
             for g in range(GQA_GROUP)], axis=0)
        s2 = _bdot_nt(qs, k_h)
        fold = jnp.where(from_prev, s2[:, 0:W], s2[:, W:2 * W])
        s.append(jnp.where(no_prev, NEG_BIG, fold) if sb == 0 else fold)
    sk = [_sink_column(sink_ref, layer, h, (rows, 1), 0, W) for h in range(N_KV_HEADS_B)]
    m = [jnp.maximum(jnp.max(x, axis=-1, keepdims=True), sk[h]) for x, (sb, h) in zip(s, probs)]
    p = [jnp.exp(x - y) for x, y in zip(s, m)]
    inv = [1.0 / (jnp.sum(x, axis=-1, keepdims=True) + jnp.exp(sk[h] - y))
           for x, y, (sb, h) in zip(p, m, probs)]
    for x, r, (sb, h) in zip(p, inv, probs):
        v_h = vall[sb * W:(sb + 2) * W, h * HEAD_DIM_B:(h + 1) * HEAD_DIM_B]
        x2 = jnp.concatenate([jnp.where(from_prev, x, 0.0).astype(BF16),
                              jnp.where(from_prev, 0.0, x).astype(BF16)], axis=1)
        o = jnp.dot(x2, v_h, preferred_element_type=F32) * r
        for g in range(GQA_GROUP):
            c0 = (h * GQA_GROUP + g) * HEAD_DIM_B
            o_ref[sb * W:(sb + 1) * W, c0:c0 + HEAD_DIM_B] = o[g * W:(g + 1) * W, :].astype(o_ref.dtype)


def _swa_prompt(proj, sinks, layer, batch, seq):
    nsub = min(SWA_BLOCKS, seq // WINDOW)
    tq = nsub * WINDOW
    nb = seq // tq
    kcol = OFF_KB // KV_WIDTH_B
    vcol = OFF_VB // KV_WIDTH_B
    prev = lambda col: (lambda b, j: ((b * nb + j) * nsub - jnp.where(j > 0, 1, 0), col))
    return pl.pallas_call(
        functools.partial(_swa_prompt_kernel, layer=layer, nsub=nsub),
        grid=(batch, nb),
        in_specs=[
            pl.BlockSpec(memory_space=pltpu.SMEM),
            pl.BlockSpec((tq, WIDTH_B), lambda b, j: (b * nb + j, OFF_QB // WIDTH_B)),
            pl.BlockSpec((tq, KV_WIDTH_B), lambda b, j: (b * nb + j, kcol)),
            pl.BlockSpec((WINDOW, KV_WIDTH_B), prev(kcol)),
            pl.BlockSpec((tq, KV_WIDTH_B), lambda b, j: (b * nb + j, vcol)),
            pl.BlockSpec((WINDOW, KV_WIDTH_B), prev(vcol)),
        ],
        out_specs=pl.BlockSpec((tq, WIDTH_B), lambda b, j: (b * nb + j, 0)),
        out_shape=jax.ShapeDtypeStruct((batch * seq, WIDTH_B), BF16),
        compiler_params=_cparams(("arbitrary", "arbitrary")),
        name="swa_prompt",
    )(sinks, proj, proj, proj, proj, proj)


def _gdn_sample_kernel(xc_ref, z_ref, ba_ref, cs_ref, s0_ref, cw_ref, hp_ref, gnw_ref, acc_ref,
                       o_ref, s1_ref, lhs_scr, u_scr, kg_scr, res_scr, gl_scr, *, steps, group):
    del acc_ref
    T = steps
    GB = group
    H = N_HEADS_A
    HD = HEAD_DIM_A
    R = 2 * T

    u_scr[...] = jnp.zeros(u_scr.shape, F32)
    kg_scr[...] = jnp.zeros(kg_scr.shape, F32)

    xp = [cs_ref[i] for i in range(CONV_WIDTH - 1)] + [xc_ref[t] for t in range(T)]
    xs = []
    for t in range(T):
        acc = xp[t] * cw_ref[0:1, :]
        for i in range(1, CONV_WIDTH):
            acc = acc + xp[t + i] * cw_ref[i:i + 1, :]
        xs.append(_silu(acc))

    neg_a = -jnp.exp(hp_ref[0:1, :])
    dt_b = hp_ref[1:2, :]
    beta_all = [jax.nn.sigmoid(ba_ref[t]) for t in range(T)]
    g_all = [neg_a * _softplus(ba_ref[t] + dt_b) for t in range(T)]
    gc_all = [g_all[0]]
    for t in range(1, T):
        gc_all.append(gc_all[t - 1] + g_all[t])

    u_keep, attn_keep = [], []
    for h in range(H):
        q, k, v, kb, beta, gc = [], [], [], [], [], []
        for t in range(T):
            qh = xs[t][:, h * HD:(h + 1) * HD]
            kh = xs[t][:, WIDTH_A + h * HD:WIDTH_A + (h + 1) * HD]
            vh = xs[t][:, 2 * WIDTH_A + h * HD:2 * WIDTH_A + (h + 1) * HD]
            qh = qh * lax.rsqrt(jnp.sum(qh * qh, axis=-1, keepdims=True) + EPS) * (HD ** -0.5)
            kh = kh * lax.rsqrt(jnp.sum(kh * kh, axis=-1, keepdims=True) + EPS)
            bt = beta_all[t][:, h:h + 1]
            q.append(qh)
            k.append(kh)
            v.append(vh)
            kb.append(kh * bt)
            beta.append(bt)
            gc.append(gc_all[t][:, H + h:H + h + 1])
        u, w, attn = [], [], []
        for t in range(T):
            ut = v[t] * beta[t]
            wt = kb[t] * jnp.exp(gc[t])
            arow = []
            for s in range(t + 1):
                dec = jnp.exp(gc[t] - gc[s])
                arow.append(jnp.sum(q[t] * k[s], axis=-1, keepdims=True) * dec)
                if s < t:
                    l_ts = jnp.sum(kb[t] * k[s], axis=-1, keepdims=True) * dec
                    ut = ut - l_ts * u[s]
                    wt = wt - l_ts * w[s]
            u.append(ut)
            w.append(wt)
            attn.append(arow)
        g_last = gc[T - 1]
        for t in range(T):
            row0 = h * GB * R
            lhs_scr[pl.ds(row0 + t, GB, stride=R), :] = w[t]
            lhs_scr[pl.ds(row0 + T + t, GB, stride=R), :] = q[t] * jnp.exp(gc[t])
            u_scr[pl.ds(row0 + t, GB, stride=R), :] = u[t]
            kg_scr[pl.ds(row0 + t, GB, stride=R), :] = k[t] * jnp.exp(g_last - gc[t])
        gl_scr[h] = jnp.broadcast_to(jnp.exp(g_last), (GB, HD))
        u_keep.append(u)
        attn_keep.append(attn)

    inner = min(SAMPLE_INNER, GB)

    def body(i, carry):
        pairs = [(i * inner + g, h) for g in range(inner) for h in range(H)]
        rows = [pl.multiple_of((h * GB + b) * R, R) for b, h in pairs]
        res = [_bdot(lhs_scr[pl.ds(row, R), :], s0_ref[b, h]) for (b, h), row in zip(pairs, rows)]
        vn = []
        for row, r in zip(rows, res):
            res_scr[pl.ds(row, R), :] = r
            vn.append(u_scr[pl.ds(row, R), :] - r)
        upd = [_bdot_tn(kg_scr[pl.ds(row, R), :], x) for row, x in zip(rows, vn)]
        for (b, h), x in zip(pairs, upd):
            s1_ref[b, h] = s0_ref[b, h] * gl_scr[h, pl.ds(b, 1), :] + x
        return carry

    lax.fori_loop(0, GB // inner, body, 0)

    gnw = gnw_ref[...]
    for h in range(H):
        row0 = h * GB * R
        vn = []
        for t in range(T):
            ws_t = res_scr[pl.ds(row0 + t, GB, stride=R), :]
            vn.append(u_keep[h][t] - ws_t)
        for t in range(T):
            o_t = res_scr[pl.ds(row0 + T + t, GB, stride=R), :]
            for s in range(t + 1):
                o_t = o_t + attn_keep[h][t][s] * vn[s]
            z_t = z_ref[t][:, h * HD:(h + 1) * HD]
            o_ref[t, :, h * HD:(h + 1) * HD] = _rms(o_t, gnw) * _silu(z_t)


def _gdn_sample(proj3, cs_tm, s0, conv_w, hp, gnw, layer, acc):
    steps, nseq, _ = proj3.shape
    gb = min(GDN_SAMPLE_GROUP, nseq)
    rows = N_HEADS_A * gb * 2 * steps
    kern = functools.partial(_gdn_sample_kernel, steps=steps, group=gb)
    return pl.pallas_call(
        kern,
        grid=(nseq // gb,),
        input_output_aliases={8: 1},
        in_specs=[
            pl.BlockSpec((steps, gb, CONV_CH), lambda i: (0, i, OFF_XC // CONV_CH)),
            pl.BlockSpec((steps, gb, WIDTH_A), lambda i: (0, i, OFF_Z // WIDTH_A)),
            pl.BlockSpec((steps, gb, LANE), lambda i: (0, i, OFF_BA // LANE)),
            pl.BlockSpec((None, CONV_WIDTH - 1, gb, CONV_CH), lambda i: (layer, 0, i, 0)),
            pl.BlockSpec((None, gb, N_HEADS_A, HEAD_DIM_A, HEAD_DIM_A), lambda i: (layer, i, 0, 0, 0)),
            pl.BlockSpec((None, CONV_WIDTH, CONV_CH), lambda i: (layer, 0, 0)),
            pl.BlockSpec((None, SUBLANE, LANE), lambda i: (layer, 0, 0)),
            pl.BlockSpec((None, 1, HEAD_DIM_A), lambda i: (layer, 0, 0)),
            pl.BlockSpec(memory_space=pl.ANY),
        ],
        out_specs=[
            pl.BlockSpec((steps, gb, WIDTH_A), lambda i: (0, i, 0)),
            pl.BlockSpec((None, gb, N_HEADS_A, HEAD_DIM_A, HEAD_DIM_A), lambda i: (layer, i, 0, 0, 0)),
        ],
        out_shape=[
            jax.ShapeDtypeStruct((steps, nseq, WIDTH_A), F32),
            jax.ShapeDtypeStruct(s0.shape, F32),
        ],
        scratch_shapes=[
            pltpu.VMEM((rows, HEAD_DIM_A), F32),
            pltpu.VMEM((rows, HEAD_DIM_A), F32),
            pltpu.VMEM((rows, HEAD_DIM_A), F32),
            pltpu.VMEM((rows, HEAD_DIM_A), F32),
            pltpu.VMEM((N_HEADS_A, gb, HEAD_DIM_A), F32),
        ],
        compiler_params=_cparams(("arbitrary",)),
        name="gdn_sample",
    )(proj3, proj3, proj3, cs_tm, s0, conv_w, hp, gnw, acc)


def _swa_sample_kernel(sink_ref, q_ref, kc_ref, vc_ref, kn_ref, vn_ref, knt_ref, vnt_ref,
                       o_ref, ko_ref, vo_ref, *, steps, layer):
    T = steps
    W = WINDOW
    rows = T * GQA_GROUP
    t_of_row = lax.broadcasted_iota(jnp.int32, (1, rows, 1), 1) // GQA_GROUP
    kj = lax.broadcasted_iota(jnp.int32, (1, 1, W), 2)
    cmask = kj > t_of_row
    keep = lax.broadcasted_iota(jnp.int32, (1, 1, W), 2) < W - T
    for h in range(N_KV_HEADS_B):
        lo, hi = h * HEAD_DIM_B, (h + 1) * HEAD_DIM_B
        qh = q_ref[:, h]
        qh_r = qh.astype(BF16)
        kt = kc_ref[:, h]
        vt = vc_ref[:, h]
        ko_ref[:, h] = jnp.where(keep, pltpu.roll(kt, W - T, 2), knt_ref[:, h])
        vo_ref[:, h] = jnp.where(keep, pltpu.roll(vt, W - T, 2), vnt_ref[:, h])
        sc = jnp.einsum("bqd,bdk->bqk", qh_r, kt.astype(BF16),
                        preferred_element_type=F32) * (HEAD_DIM_B ** -0.5)
        sc = jnp.where(cmask, sc, NEG_BIG)
        sn = []
        for s in range(T):
            kn_s = kn_ref[:, s:s + 1, lo:hi]
            v = jnp.sum(qh * kn_s, axis=-1, keepdims=True) * (HEAD_DIM_B ** -0.5)
            sn.append(jnp.where(t_of_row >= s, v, NEG_BIG))
        sk = _sink_column(sink_ref, layer, h, (1, rows, 1), 1, 1)
        m = jnp.maximum(jnp.max(sc, axis=-1, keepdims=True), sk)
        for s in range(T):
            m = jnp.maximum(m, sn[s])
        pc = jnp.exp(sc - m)
        pn = [jnp.exp(sn[s] - m) for s in range(T)]
        den = jnp.sum(pc, axis=-1, keepdims=True) + jnp.exp(sk - m)
        for s in range(T):
            den = den + pn[s]
        inv = 1.0 / den
        o = jnp.einsum("bqk,bdk->bqd", (pc * inv).astype(BF16), vt.astype(BF16), preferred_element_type=F32)
        for s in range(T):
            o = o + (pn[s] * inv) * vn_ref[:, s:s + 1, lo:hi]
        o_ref[:, h] = o


def _swa_sample(q_r, kc_t, vc_t, kn, vn, knt, vnt, sinks, layer):
    nseq, _, rows, _ = q_r.shape
    steps = kn.shape[1]
    gb = min(SAMPLE_GROUP, nseq)
    kern = functools.partial(_swa_sample_kernel, steps=steps, layer=layer)
    cache_spec = pl.BlockSpec((None, gb, N_KV_HEADS_B, HEAD_DIM_B, WINDOW), lambda i: (layer, i, 0, 0, 0))
    win_spec = pl.BlockSpec((gb, N_KV_HEADS_B, HEAD_DIM_B, WINDOW), lambda i: (i, 0, 0, 0))
    new_spec = pl.BlockSpec((gb, steps, KV_WIDTH_B), lambda i: (i, 0, 0))
    q_spec = pl.BlockSpec((gb, N_KV_HEADS_B, rows, HEAD_DIM_B), lambda i: (i, 0, 0, 0))
    win_shape = jax.ShapeDtypeStruct((nseq, N_KV_HEADS_B, HEAD_DIM_B, WINDOW), F32)
    return pl.pallas_call(
        kern,
        grid=(nseq // gb,),
        in_specs=[pl.BlockSpec(memory_space=pltpu.SMEM), q_spec, cache_spec, cache_spec,
                  new_spec, new_spec, win_spec, win_spec],
        out_specs=[q_spec, win_spec, win_spec],
        out_shape=[jax.ShapeDtypeStruct(q_r.shape, F32), win_shape, win_shape],
        compiler_params=_cparams(("arbitrary",)),
        name="swa_sample",
    )(sinks, q_r, kc_t, vc_t, kn, vn, knt, vnt)


def _prompt_layer(x, p, layer, batch, seq):
    tm = min(TOKEN_TILE, seq)
    rep = seq // tm
    proj, tail = _premix(x, p["mod_p"], p["g_pre_mix"], p["w_in"], layer, tm, rep, conv_w=p["conv_w"])
    o_a, ssm = _gdn_prompt(proj, p["hp"], p["gdn_norm_w"], layer, batch, seq)
    o_b = _swa_prompt(proj, p["sinks"], layer, batch, seq)
    x = _ffn(x, o_a, o_b, p["mod_p"], p["g_post_mix"], p["g_pre_ffn"], p["g_post_ffn"],
             p["w_o"], p["w_up"], p["w_down"], layer, tm, rep)
    p3 = proj.reshape(batch, seq, PROJ_COLS)
    conv = tail[:, SUBLANE - (CONV_WIDTH - 1):, :]
    k_new = p3[:, seq - WINDOW:, OFF_KB:OFF_KB + KV_WIDTH_B].reshape(batch, WINDOW, N_KV_HEADS_B, HEAD_DIM_B)
    v_new = p3[:, seq - WINDOW:, OFF_VB:OFF_VB + KV_WIDTH_B].reshape(batch, WINDOW, N_KV_HEADS_B, HEAD_DIM_B)
    return x, conv, ssm, k_new, v_new


def _sample_layer(x, p, layer, nseq, steps, ssm_acc):
    proj = _premix(x, p["mod_s"], p["g_pre_mix"], p["w_in"], layer, steps * nseq, steps)
    p3 = proj.reshape(steps, nseq, PROJ_COLS)
    o_a, ssm = _gdn_sample(p3, p["conv_tm"], p["state_ssm"], p["conv_w"], p["hp"], p["gdn_norm_w"], layer,
                           acc=ssm_acc)
    q_r = p3[:, :, OFF_QB:OFF_QB + WIDTH_B].reshape(steps, nseq, N_KV_HEADS_B, GQA_GROUP, HEAD_DIM_B)
    q_r = jnp.transpose(q_r, (1, 2, 0, 3, 4)).reshape(nseq, N_KV_HEADS_B, steps * GQA_GROUP, HEAD_DIM_B)
    k_tm = p3[:, :, OFF_KB:OFF_KB + KV_WIDTH_B]
    v_tm = p3[:, :, OFF_VB:OFF_VB + KV_WIDTH_B]

    def tail_lanes(a):
        a = jnp.transpose(a.reshape(steps, nseq, N_KV_HEADS_B, HEAD_DIM_B), (1, 2, 3, 0))
        return jnp.pad(a, ((0, 0), (0, 0), (0, 0), (WINDOW - steps, 0)))

    o_b, k_win, v_win = _swa_sample(q_r, p["cache_kt"], p["cache_vt"], jnp.transpose(k_tm, (1, 0, 2)),
                                    jnp.transpose(v_tm, (1, 0, 2)), tail_lanes(k_tm), tail_lanes(v_tm),
                                    p["sinks"], layer)
    o_b = o_b.reshape(nseq, N_KV_HEADS_B, steps, GQA_GROUP, HEAD_DIM_B)
    o_b = jnp.transpose(o_b, (2, 0, 1, 3, 4)).reshape(steps * nseq, WIDTH_B)
    x = _ffn(x, o_a.reshape(steps * nseq, WIDTH_A), o_b, p["mod_s"], p["g_post_mix"], p["g_pre_ffn"],
             p["g_post_ffn"], p["w_o"], p["w_up"], p["w_down"], layer, steps * nseq, steps)
    xp = jnp.concatenate([p["conv_tm"][layer], p3[:, :, OFF_XC:OFF_XC + CONV_CH]], axis=0)
    conv = jnp.transpose(xp[xp.shape[0] - (CONV_WIDTH - 1):], (1, 0, 2))
    return x, conv, ssm, k_win, v_win


def _pack_w_in(w_in):
    o1 = CONV_CH + WIDTH_A
    o2 = o1 + 2 * N_HEADS_A
    pad = jnp.zeros(w_in.shape[:-1] + (PROJ_COLS - OFF_BA - 2 * N_HEADS_A,), w_in.dtype)
    return jnp.concatenate([w_in[..., :o1], w_in[..., o2:], w_in[..., o1:o2], pad], axis=-1).astype(BF16)


def _head_params(a_log, dt_bias):
    depth = a_log.shape[0]
    hp = jnp.zeros((depth, SUBLANE, LANE), F32)
    hp = hp.at[:, 0, N_HEADS_A:2 * N_HEADS_A].set(a_log.astype(F32))
    hp = hp.at[:, 1, N_HEADS_A:2 * N_HEADS_A].set(dt_bias.astype(F32))
    return hp


def kernel(x_prompt, x_sample, state_conv, state_ssm, cache_swa_k, cache_swa_v, c_prompt, c_sample,
           g_pre_mix, g_post_mix, g_pre_ffn, g_post_ffn, w_ada, b_ada, w_in, conv_w, a_log, dt_bias,
           gdn_norm_w, sinks, w_o, w_up, w_down):
    depth = w_ada.shape[0]
    batch, seq, _ = x_prompt.shape
    nseq, steps, _ = x_sample.shape

    c_all = jnp.concatenate([c_sample, c_prompt], axis=0)
    rows = c_all.shape[0]
    rows_p = -(-rows // 8) * 8
    c_all = jnp.pad(c_all, ((0, rows_p - rows), (0, 0)))
    mod_all = _modulation(c_all, w_ada, b_ada)
    p = dict(
        mod_p=jnp.transpose(mod_all[:, :, nseq:nseq + batch], (0, 2, 1, 3)).reshape(depth, batch, N_MOD, 1, D_MODEL),
        mod_s=mod_all,
        g_pre_mix=g_pre_mix.reshape(depth, 1, D_MODEL), g_post_mix=g_post_mix.reshape(depth, 1, D_MODEL),
        g_pre_ffn=g_pre_ffn.reshape(depth, 1, D_MODEL), g_post_ffn=g_post_ffn.reshape(depth, 1, D_MODEL),
        w_in=_pack_w_in(w_in), conv_w=conv_w, hp=_head_params(a_log, dt_bias),
        gdn_norm_w=gdn_norm_w.reshape(depth, 1, HEAD_DIM_A), sinks=sinks,
        w_o=w_o.astype(BF16), w_up=w_up.astype(BF16), w_down=w_down.astype(BF16),
        conv_tm=jnp.transpose(state_conv, (0, 2, 1, 3)), state_ssm=state_ssm,
        cache_kt=jnp.transpose(cache_swa_k, (0, 1, 3, 4, 2)),
        cache_vt=jnp.transpose(cache_swa_v, (0, 1, 3, 4, 2)),
    )

    yp = x_prompt.reshape(batch * seq, D_MODEL)
    ys = jnp.transpose(x_sample, (1, 0, 2)).reshape(steps * nseq, D_MODEL)
    outs_p, outs_s = [], []
    ssm_s = jnp.zeros(state_ssm.shape, F32)
    for l in range(depth):
        yp, c1, s1, k1, v1 = _prompt_layer(yp, p, l, batch, seq)
        ys, c2, ssm_s, k2, v2 = _sample_layer(ys, p, l, nseq, steps, ssm_s)
        outs_p.append((c1, s1, k1, v1))
        outs_s.append((c2, None, k2, v2))

    y_prompt = yp.reshape(batch, seq, D_MODEL)
    y_sample = jnp.transpose(ys.reshape(steps, nseq, D_MODEL), (1, 0, 2))
    stack = lambda outs, i: jnp.stack([o[i] for o in outs])
    k_s = jnp.transpose(stack(outs_s, 2), (0, 1, 4, 2, 3))
    v_s = jnp.transpose(stack(outs_s, 3), (0, 1, 4, 2, 3))
    return (y_prompt, y_sample,
            stack(outs_p, 0), stack(outs_p, 1), stack(outs_p, 2), stack(outs_p, 3),
            stack(outs_s, 0), ssm_s, k_s, v_s)
```
